```python
import jax
import jax.numpy as jnp
from jax import lax
import numpy as np

D_MODEL = 1024
BATCH = 2
SEQ = 8192
DEPTH = 4
DEC_BATCH = 32
DEC_SEQ = 4
PAST_LEN = 8192
PAGE_SIZE = 128

D_LRU = D_MODEL // 4
N_LRU_BLOCKS = 4
LRU_BLOCK = D_LRU // N_LRU_BLOCKS
CONV_W = 4
LRU_C = 8.0
HGRN_DK = 64
HGRN_DV = 64
D_HGRN = D_MODEL // 4
N_HGRN = D_HGRN // HGRN_DK
HGRN_CHUNK = 64
ATT_DH = 64
D_ATT = D_MODEL // 2
N_ATT = D_ATT // ATT_DH
DILATED = ((128, 1), (512, 4), (2048, 16))
MAX_WINDOW = 2048
Q_BLOCK = 128
N_MEM = 256
N_XHEADS = 4
XHEAD_DIM = D_MODEL // N_XHEADS
D_FF = 2816
GROUP_COLS = (D_LRU, D_LRU, D_HGRN, D_HGRN, D_HGRN, D_HGRN, D_ATT, D_ATT, D_ATT)
D_IN = 2 * D_LRU + 4 * D_HGRN + 3 * D_ATT
EPS = 1e-6
F32 = jnp.float32

kernel_name = 'hymba_lru_hgrn2_dilated_swa_decoder_step'


def rmsnorm(x, g):
    xf = x.astype(F32)
    y = xf * lax.rsqrt(jnp.mean(xf * xf, axis=-1, keepdims=True) + EPS)
    return (y * g.astype(F32)).astype(x.dtype)


def swiglu(x, w_gate, w_up, w_down):
    return (jax.nn.silu(x @ w_gate) * (x @ w_up)) @ w_down


def causal_conv(x, buf, w, b):
    T = x.shape[1]
    xp = jnp.concatenate([buf.astype(x.dtype), x], axis=1)
    y = b.astype(x.dtype)
    for tap in range(CONV_W):
        y = y + w[tap].astype(x.dtype) * xp[:, tap:tap + T]
    return y, xp[:, -(CONV_W - 1):]


def rg_lru(x, h0, w_a, b_a, w_x, b_x, lam):
    B, T, _ = x.shape
    xf = x.astype(F32)
    xb = xf.reshape(B, T, N_LRU_BLOCKS, LRU_BLOCK)
    r = jax.nn.sigmoid(jnp.einsum('btnc,ncd->btnd', xb, w_a.astype(F32)).reshape(B, T, D_LRU) + b_a.astype(F32))
    i = jax.nn.sigmoid(jnp.einsum('btnc,ncd->btnd', xb, w_x.astype(F32)).reshape(B, T, D_LRU) + b_x.astype(F32))
    log_a = -LRU_C * r * jax.nn.softplus(-lam.astype(F32))
    a = jnp.exp(log_a)
    u = jnp.sqrt(-jnp.expm1(2.0 * log_a)) * (i * xf)

    def combine(lhs, rhs):
        a1, b1 = lhs
        a2, b2 = rhs
        return a1 * a2, a2 * b1 + b2

    a_cum, u_cum = lax.associative_scan(combine, (a, u), axis=1)
    h = a_cum * h0.astype(F32)[:, None] + u_cum
    return h.astype(x.dtype), h[:, -1].astype(h0.dtype)


def hgrn2(q, f_pre, v, g, s0, lb, norm_g):
    B, T, _ = q.shape
    H, DK, DV = N_HGRN, HGRN_DK, HGRN_DV
    lb = lb.astype(F32).reshape(H, DK)
    f = lb + (1.0 - lb) * jax.nn.sigmoid(f_pre.astype(F32).reshape(B, T, H, DK))
    log_f = jnp.log(f)
    k = 1.0 - f
    qf = q.astype(F32).reshape(B, T, H, DK)
    vf = v.astype(F32).reshape(B, T, H, DV)
    C = HGRN_CHUNK if T % HGRN_CHUNK == 0 else T
    N = T // C

    def to_chunks(z):
        return z.reshape(B, N, C, H, z.shape[-1]).transpose(1, 0, 3, 2, 4)

    causal = jnp.tril(jnp.ones((C, C), dtype=bool))

    def chunk_step(S, inp):
        qc, kc, vc, gc = inp
        cum = jnp.cumsum(gc, axis=2)
        rel = jnp.where(causal[None, None, :, :, None],
                        cum[:, :, :, None, :] - cum[:, :, None, :, :], -jnp.inf)
        scores = jnp.einsum('bhtd,bhsd,bhtsd->bhts', qc, kc, jnp.exp(rel))
        o = (jnp.einsum('bhts,bhsv->bhtv', scores, vc)
             + jnp.einsum('bhtd,bhdv->bhtv', qc * jnp.exp(cum), S))
        last = cum[:, :, -1]
        S = (jnp.exp(last)[..., None] * S
             + jnp.einsum('bhsd,bhsv->bhdv', kc * jnp.exp(last[:, :, None] - cum), vc))
        return S, o

    s_fin, o = lax.scan(chunk_step, s0.astype(F32),
                        (to_chunks(qf), to_chunks(k), to_chunks(vf), to_chunks(log_f)))
    o = o.transpose(1, 0, 3, 2, 4).reshape(B, T, H, DV)
    o = rmsnorm(o, norm_g).reshape(B, T, H * DV) * jax.nn.silu(g.astype(F32))
    return o.astype(q.dtype), s_fin.astype(s0.dtype)


def dilated_attention(q, k_all, v_all, q_idx):
    slopes = jnp.asarray(2.0 ** (-8.0 * np.arange(1, N_ATT + 1) / N_ATT), dtype=F32)
    scale = ATT_DH ** -0.5
    lses = []
    outs = []
    for window, dil in DILATED:
        n_keys = window // dil + 1
        offs = jnp.arange(n_keys, dtype=jnp.int32) * dil
        idx = q_idx[:, None] - offs[None, :]
        valid = idx >= 0
        idx = jnp.maximum(idx, 0)
        kg = k_all[:, idx]
        vg = v_all[:, idx]
        s = (jnp.einsum('bthd,btnhd->bthn', q, kg).astype(F32) * scale
             - slopes[:, None] * offs.astype(F32)[None, :])
        s = jnp.where(valid[None, :, None, :], s, -jnp.inf)
        m = jnp.max(s, axis=-1, keepdims=True)
        p = jnp.exp(s - m)
        den = jnp.sum(p, axis=-1)
        o = jnp.einsum('bthn,btnhd->bthd', p, vg.astype(F32)) / den[..., None]
        lses.append(m[..., 0] + jnp.log(den))
        outs.append(o)
    w = jax.nn.softmax(jnp.stack(lses, axis=-1), axis=-1)
    return jnp.einsum('bthg,gbthd->bthd', w, jnp.stack(outs, axis=0))


def dilated_attention_blocked(q, k, v):
    B, T, H, D = q.shape
    nb = T // Q_BLOCK
    qb = q.reshape(B, nb, Q_BLOCK, H, D).transpose(1, 0, 2, 3, 4)
    idx = jnp.arange(T, dtype=jnp.int32).reshape(nb, Q_BLOCK)
    ob = lax.map(lambda a: dilated_attention(a[0], k, v, a[1]), (qb, idx))
    return ob.transpose(1, 0, 2, 3, 4).reshape(B, T, H, D)


def cross_attend(h, mem_k, mem_v, wq, wo):
    B, T, _ = h.shape
    q = (h @ wq).reshape(B, T, N_XHEADS, XHEAD_DIM)
    s = jnp.einsum('bthd,bmhd->bhtm', q, mem_k).astype(F32) * (XHEAD_DIM ** -0.5)
    p = jax.nn.softmax(s, axis=-1)
    o = jnp.einsum('bhtm,bmhd->bthd', p.astype(h.dtype), mem_v)
    return o.reshape(B, T, D_MODEL) @ wo


def trunk_layer(x, lp, lru_h, lru_conv, hgrn_s, swa_k_buf, swa_v_buf, mem_k, mem_v):
    B, T, _ = x.shape
    x = x + 0.5 * swiglu(rmsnorm(x, lp['n_ffn1']), lp['ffn1_wg'], lp['ffn1_wu'], lp['ffn1_wd'])
    h = rmsnorm(x, lp['n_mix'])
    z = h @ lp['w_in']
    xa, ga, qb, fb, ib, gb, qc, kc, vc = jnp.split(z, np.cumsum(GROUP_COLS)[:-1].tolist(), axis=-1)
    xa, conv_new = causal_conv(xa, lru_conv, lp['lru_conv_w'], lp['lru_conv_b'])
    ya, h_last = rg_lru(xa, lru_h, lp['lru_wa'], lp['lru_ba'], lp['lru_wx'], lp['lru_bx'], lp['lru_lambda'])
    ya = rmsnorm(ya * jax.nn.gelu(ga), lp['gn_a'])
    yb, s_new = hgrn2(qb, fb, ib, gb, hgrn_s, lp['lb'], lp['hgrn_norm'])
    q = qc.reshape(B, T, N_ATT, ATT_DH)
    k = kc.reshape(B, T, N_ATT, ATT_DH)
    v = vc.reshape(B, T, N_ATT, ATT_DH)
    if swa_k_buf is None:
        yc = dilated_attention_blocked(q, k, v)
        keep = min(MAX_WINDOW, T)
        k_new = k[:, T - keep:]
        v_new = v[:, T - keep:]
    else:
        W = swa_k_buf.shape[1]
        k_all = jnp.concatenate([swa_k_buf.astype(k.dtype), k], axis=1)
        v_all = jnp.concatenate([swa_v_buf.astype(v.dtype), v], axis=1)
        yc = dilated_attention(q, k_all, v_all, W + jnp.arange(T, dtype=jnp.int32))
        k_new = k_all[:, T:]
        v_new = v_all[:, T:]
    yc = rmsnorm(yc.reshape(B, T, D_ATT).astype(x.dtype), lp['gn_c'])
    x = x + jnp.concatenate([ya, yb, yc], axis=-1) @ lp['w_out']
    x = x + cross_attend(rmsnorm(x, lp['n_cross']), mem_k, mem_v, lp['x_wq'], lp['x_wo'])
    x = x + 0.5 * swiglu(rmsnorm(x, lp['n_ffn2']), lp['ffn2_wg'], lp['ffn2_wu'], lp['ffn2_wd'])
    return x, h_last, conv_new, s_new, k_new, v_new


def setup_inputs(seed: int = 0) -> dict:
    key = jax.random.key(seed)
    keys = list(jax.random.split(key, 64))

    def nrm(shape, scale):
        return scale * jax.random.normal(keys.pop(), shape, jnp.float32)

    def gain(shape):
        return 1.0 + 0.02 * jax.random.normal(keys.pop(), shape, jnp.float32)

    w_buf = min(MAX_WINDOW, PAST_LEN)
    u = jax.random.uniform(keys.pop(), (DEPTH, D_LRU), jnp.float32, minval=0.9, maxval=0.999)
    sig = u ** (1.0 / LRU_C)
    lru_lambda = jnp.log(sig) - jnp.log1p(-sig)
    d_in_s = D_MODEL ** -0.5
    return {
        'x_prompt': nrm((BATCH, SEQ, D_MODEL), 1.0),
        'x_sample': nrm((DEC_BATCH, DEC_SEQ, D_MODEL), 1.0),
        'state_lru_h': nrm((DEPTH, DEC_BATCH, D_LRU), 0.5),
        'state_lru_conv': nrm((DEPTH, DEC_BATCH, CONV_W - 1, D_LRU), 1.0),
        'state_hgrn': nrm((DEPTH, DEC_BATCH, N_HGRN, HGRN_DK, HGRN_DV), 0.5),
        'cache_swa_k': nrm((DEPTH, DEC_BATCH, w_buf, N_ATT, ATT_DH), 1.0),
        'cache_swa_v': nrm((DEPTH, DEC_BATCH, w_buf, N_ATT, ATT_DH), 1.0),
        'cache_mem_k': nrm((DEPTH, DEC_BATCH, N_MEM, N_XHEADS, XHEAD_DIM), 1.0),
        'cache_mem_v': nrm((DEPTH, DEC_BATCH, N_MEM, N_XHEADS, XHEAD_DIM), 1.0),
        'mem_prompt': nrm((BATCH, N_MEM, D_MODEL), 1.0),
        'n_ffn1': gain((DEPTH, D_MODEL)),
        'ffn1_wg': nrm((DEPTH, D_MODEL, D_FF), d_in_s),
        'ffn1_wu': nrm((DEPTH, D_MODEL, D_FF), d_in_s),
        'ffn1_wd': nrm((DEPTH, D_FF, D_MODEL), D_FF ** -0.5),
        'n_mix': gain((DEPTH, D_MODEL)),
        'w_in': nrm((DEPTH, D_MODEL, D_IN), d_in_s),
        'lru_conv_w': nrm((DEPTH, CONV_W, D_LRU), CONV_W ** -0.5),
        'lru_conv_b': nrm((DEPTH, D_LRU), 0.01),
        'lru_wa': nrm((DEPTH, N_LRU_BLOCKS, LRU_BLOCK, LRU_BLOCK), LRU_BLOCK ** -0.5),
        'lru_ba': nrm((DEPTH, D_LRU), 0.01),
        'lru_wx': nrm((DEPTH, N_LRU_BLOCKS, LRU_BLOCK, LRU_BLOCK), LRU_BLOCK ** -0.5),
        'lru_bx': nrm((DEPTH, D_LRU), 0.01),
        'lru_lambda': lru_lambda,
        'hgrn_lb': nrm((DEPTH, D_HGRN), 0.1),
        'hgrn_norm': gain((DEPTH, HGRN_DV)),
        'gn_a': gain((DEPTH, D_LRU)),
        'gn_c': gain((DEPTH, D_ATT)),
        'w_out': nrm((DEPTH, D_MODEL, D_MODEL), d_in_s),
        'n_cross': gain((DEPTH, D_MODEL)),
        'x_wq': nrm((DEPTH, D_MODEL, D_MODEL), d_in_s),
        'x_wk': nrm((DEPTH, D_MODEL, D_MODEL), d_in_s),
        'x_wv': nrm((DEPTH, D_MODEL, D_MODEL), d_in_s),
        'x_wo': nrm((DEPTH, D_MODEL, D_MODEL), d_in_s),
        'n_ffn2': gain((DEPTH, D_MODEL)),
        'ffn2_wg': nrm((DEPTH, D_MODEL, D_FF), d_in_s),
        'ffn2_wu': nrm((DEPTH, D_MODEL, D_FF), d_in_s),
        'ffn2_wd': nrm((DEPTH, D_FF, D_MODEL), D_FF ** -0.5),
        'n_final': gain((D_MODEL,)),
    }


def reference(x_prompt, x_sample, state_lru_h, state_lru_conv, state_hgrn, cache_swa_k, cache_swa_v,
              cache_mem_k, cache_mem_v, mem_prompt, n_ffn1, ffn1_wg, ffn1_wu, ffn1_wd, n_mix, w_in,
              lru_conv_w, lru_conv_b, lru_wa, lru_ba, lru_wx, lru_bx, lru_lambda, hgrn_lb, hgrn_norm,
              gn_a, gn_c, w_out, n_cross, x_wq, x_wk, x_wv, x_wo, n_ffn2, ffn2_wg, ffn2_wu, ffn2_wd,
              n_final):
    lb_all = lax.cumsum(jax.nn.softmax(hgrn_lb.astype(F32), axis=0), axis=0)
    lb_all = lb_all - lb_all[0]
    B = x_prompt.shape[0]
    dt = x_prompt.dtype
    xp = x_prompt
    xs = x_sample
    p_h, p_c, p_s, p_k, p_v, p_mk, p_mv = [], [], [], [], [], [], []
    s_h, s_c, s_s, s_k, s_v = [], [], [], [], []
    for l in range(DEPTH):
        lp = {
            'n_ffn1': n_ffn1[l], 'ffn1_wg': ffn1_wg[l], 'ffn1_wu': ffn1_wu[l], 'ffn1_wd': ffn1_wd[l],
            'n_mix': n_mix[l], 'w_in': w_in[l],
            'lru_conv_w': lru_conv_w[l], 'lru_conv_b': lru_conv_b[l],
            'lru_wa': lru_wa[l], 'lru_ba': lru_ba[l], 'lru_wx': lru_wx[l], 'lru_bx': lru_bx[l],
            'lru_lambda': lru_lambda[l], 'lb': lb_all[l], 'hgrn_norm': hgrn_norm[l],
            'gn_a': gn_a[l], 'gn_c': gn_c[l], 'w_out': w_out[l],
            'n_cross': n_cross[l], 'x_wq': x_wq[l], 'x_wo': x_wo[l],
            'n_ffn2': n_ffn2[l], 'ffn2_wg': ffn2_wg[l], 'ffn2_wu': ffn2_wu[l], 'ffn2_wd': ffn2_wd[l],
        }
        mk = (mem_prompt @ x_wk[l]).reshape(B, N_MEM, N_XHEADS, XHEAD_DIM)
        mv = (mem_prompt @ x_wv[l]).reshape(B, N_MEM, N_XHEADS, XHEAD_DIM)
        xp, ph, pc, ps, pk, pv = trunk_layer(
            xp, lp, jnp.zeros((B, D_LRU), dt), jnp.zeros((B, CONV_W - 1, D_LRU), dt),
            jnp.zeros((B, N_HGRN, HGRN_DK, HGRN_DV), dt), None, None, mk, mv)
        xs, sh, sc, ss, sk, sv = trunk_layer(
            xs, lp, state_lru_h[l], state_lru_conv[l], state_hgrn[l],
            cache_swa_k[l], cache_swa_v[l], cache_mem_k[l], cache_mem_v[l])
        p_h.append(ph); p_c.append(pc); p_s.append(ps); p_k.append(pk); p_v.append(pv)
        p_mk.append(mk); p_mv.append(mv)
        s_h.append(sh); s_c.append(sc); s_s.append(ss); s_k.append(sk); s_v.append(sv)
    y_prompt = rmsnorm(xp, n_final)
    y_sample = rmsnorm(xs, n_final)
    return (y_prompt, y_sample,
            jnp.stack(p_h), jnp.stack(p_c), jnp.stack(p_s), jnp.stack(p_k), jnp.stack(p_v),
            jnp.stack(p_mk), jnp.stack(p_mv),
            jnp.stack(s_h), jnp.stack(s_c), jnp.stack(s_s), jnp.stack(s_k), jnp.stack(s_v))
```

```python
import functools

import numpy as np
import jax
import jax.numpy as jnp
from jax import lax
from jax.experimental import pallas as pl
from jax.experimental.pallas import tpu as pltpu

F32 = jnp.float32
BF16 = jnp.bfloat16

D_MODEL = 1024
DEPTH = 4
D_LRU = 256
N_LRU_BLOCKS = 4
CONV_W = 4
LRU_C = 8.0
HGRN_DK = 64
HGRN_DV = 64
D_HGRN = 256
N_HGRN = 4
HGRN_CHUNK = 64
ATT_DH = 64
D_ATT = 512
N_ATT = 8
DILATIONS = (1, 4, 16)
BAND = 128
MAX_WINDOW = 2048
N_MEM = 256
N_XHEADS = 4
XHEAD_DIM = 256
D_FF = 2816
D_IN = 3072
EPS = 1e-6

COL_XA, COL_GA, COL_QB, COL_FB, COL_IB, COL_GB = 0, 1, 2, 3, 4, 5
COL_QC, COL_KC, COL_VC = 3, 4, 5
ATT_COLS_PER_ROW = D_IN // D_ATT

SUBLANE = 8
SAMPLE_ROWS = 8
VMEM_LIMIT = 56 * 1024 * 1024
NEG_BIG = -1e30

ALIBI_SLOPES = tuple(float(2.0 ** (-8.0 * (h + 1) / N_ATT)) for h in range(N_ATT))


def _log2(n):
    assert n & (n - 1) == 0
    return n.bit_length() - 1


def _cparams(*sem):
    return pltpu.CompilerParams(dimension_semantics=sem, vmem_limit_bytes=VMEM_LIMIT)


def _rms(x, g):
    ms = jnp.mean(x * x, axis=-1, keepdims=True)
    return x * lax.rsqrt(ms + EPS) * g


def _expm1(x):
    e = jnp.exp(x)
    near = (e - 1.0) * x / jnp.log(e)
    return jnp.where(e == 1.0, x, jnp.where(jnp.abs(x) > 1.0, e - 1.0, near))


def _dot(a, b):
    return jnp.dot(a, b, preferred_element_type=F32)


def _dot_nt(a, b):
    return lax.dot_general(a, b, (((1,), (1,)), ((), ())), preferred_element_type=F32)


def _dot_tn(a, b):
    return lax.dot_general(a, b, (((0,), (0,)), ((), ())), preferred_element_type=F32)


def _dot_exact_lhs(a_bf16, x):
    hi = x.astype(BF16)
    r1 = x - hi.astype(F32)
    mid = r1.astype(BF16)
    lo = (r1 - mid.astype(F32)).astype(BF16)
    return _dot(a_bf16, hi) + _dot(a_bf16, mid) + _dot(a_bf16, lo)


def _ffn_kernel(*refs, final_norm):
    if final_norm:
        x_ref, g_ref, wg_ref, wu_ref, wd_ref, gf_ref, o_ref, y_ref, h_scr, acc_scr = refs
    else:
        x_ref, g_ref, wg_ref, wu_ref, wd_ref, o_ref, h_scr, acc_scr = refs
    j = pl.program_id(1)

    @pl.when(j == 0)
    def _():
        h_scr[...] = _rms(x_ref[...], g_ref[...]).astype(BF16)
        acc_scr[...] = jnp.zeros_like(acc_scr)

    h = h_scr[...]
    gate = _dot(h, wg_ref[...])
    up = _dot(h, wu_ref[...])
    act = (gate * jax.nn.sigmoid(gate) * up).astype(BF16)
    acc_scr[...] += _dot(act, wd_ref[...])

    @pl.when(j == pl.num_programs(1) - 1)
    def _():
        out = x_ref[...] + 0.5 * acc_scr[...]
        o_ref[...] = out
        if final_norm:
            y_ref[...] = _rms(out, gf_ref[...])


def _ffn(x, g, wg, wu, wd, tm, g_final=None):
    m = x.shape[0]
    tf = D_FF // 2
    final_norm = g_final is not None
    row = pl.BlockSpec((tm, D_MODEL), lambda i, j: (i, 0))
    vec = pl.BlockSpec((1, D_MODEL), lambda i, j: (0, 0))
    in_specs = [row, vec,
                pl.BlockSpec((D_MODEL, tf), lambda i, j: (0, j)),
                pl.BlockSpec((D_MODEL, tf), lambda i, j: (0, j)),
                pl.BlockSpec((tf, D_MODEL), lambda i, j: (j, 0))]
    args = [x, g.reshape(1, D_MODEL), wg, wu, wd]
    out_shape = jax.ShapeDtypeStruct((m, D_MODEL), F32)
    out_specs = row
    if final_norm:
        in_specs.append(vec)
        args.append(g_final.reshape(1, D_MODEL))
        out_shape = (out_shape, out_shape)
        out_specs = (row, row)
    return pl.pallas_call(
        functools.partial(_ffn_kernel, final_norm=final_norm),
        grid=(m // tm, D_FF // tf),
        in_specs=in_specs, out_specs=out_specs, out_shape=out_shape,
        scratch_shapes=[pltpu.VMEM((tm, D_MODEL), BF16), pltpu.VMEM((tm, D_MODEL), F32)],
        compiler_params=_cparams("parallel", "arbitrary"),
        name="ffn",
    )(*args)


def _proj_kernel(x_ref, g_ref, w_ref, o_ref, *, norm):
    x = x_ref[...]
    if norm:
        x = _rms(x, g_ref[...])
    o_ref[...] = _dot(x.astype(BF16), w_ref[...])


def _proj(x, g, w, tm, norm=True):
    m, k = x.shape
    n = w.shape[1]
    return pl.pallas_call(
        functools.partial(_proj_kernel, norm=norm),
        grid=(m // tm,),
        in_specs=[pl.BlockSpec((tm, k), lambda i: (i, 0)),
                  pl.BlockSpec((1, k), lambda i: (0, 0)),
                  pl.BlockSpec((k, n), lambda i: (0, 0))],
        out_specs=pl.BlockSpec((tm, n), lambda i: (i, 0)),
        out_shape=jax.ShapeDtypeStruct((m, n), F32),
        compiler_params=_cparams("parallel"),
        name="proj",
    )(x, g.reshape(1, k), w)


def _lru_kernel(xa_ref, ga_ref, cb_ref, h0_ref, cw_ref, cbias_ref, wa_ref, ba_ref, wx_ref, bx_ref,
                lam_ref, gn_ref, ya_ref, hl_ref, xbuf, abuf, ubuf, hc, *, tt, pad, valid):
    ti = pl.program_id(1)

    @pl.when(ti == 0)
    def _():
        xbuf[0:SUBLANE, :] = cb_ref[0]
        hc[...] = jnp.broadcast_to(h0_ref[0], (SUBLANE, D_LRU))
        for s in range(2):
            abuf[s, 0:pad, :] = jnp.ones((pad, D_LRU), F32)
            ubuf[s, 0:pad, :] = jnp.zeros((pad, D_LRU), F32)

    x = xa_ref[0]
    xbuf[SUBLANE:SUBLANE + tt, :] = x
    y = cbias_ref[...] + cw_ref[CONV_W - 1:CONV_W, :] * x
    for tap in range(CONV_W - 1):
        back = CONV_W - 1 - tap
        y = y + cw_ref[tap:tap + 1, :] * xbuf[SUBLANE - back:SUBLANE - back + tt, :]
    xbuf[0:SUBLANE, :] = x[tt - SUBLANE:tt, :]

    yb = y.astype(BF16)
    r = jax.nn.sigmoid(_dot(yb, wa_ref[...]) + ba_ref[...])
    i = jax.nn.sigmoid(_dot(yb, wx_ref[...]) + bx_ref[...])
    lam = lam_ref[...]
    softplus_neg_lam = jnp.maximum(-lam, 0.0) + jnp.log1p(jnp.exp(-jnp.abs(lam)))
    log_a = -LRU_C * r * softplus_neg_lam
    a = jnp.exp(log_a)
    u = jnp.sqrt(-_expm1(2.0 * log_a)) * (i * y)
    if valid is not None:
        keep = lax.broadcasted_iota(jnp.int32, (tt, D_LRU), 0) < valid
        a = jnp.where(keep, a, 1.0)
        u = jnp.where(keep, u, 0.0)

    abuf[0, pad:pad + tt, :] = a
    ubuf[0, pad:pad + tt, :] = u
    cur, k = 0, 1
    while k < tt:
        a_c = abuf[cur, pad:pad + tt, :]
        u_c = ubuf[cur, pad:pad + tt, :]
        a_s = abuf[cur, pad - k:pad - k + tt, :]
        u_s = ubuf[cur, pad - k:pad - k + tt, :]
        abuf[1 - cur, pad:pad + tt, :] = a_c * a_s
        ubuf[1 - cur, pad:pad + tt, :] = a_c * u_s + u_c
        cur, k = 1 - cur, 2 * k
    h = abuf[cur, pad:pad + tt, :] * hc[SUBLANE - 1:SUBLANE, :] + ubuf[cur, pad:pad + tt, :]
    hc[...] = h[tt - SUBLANE:tt, :]
    hl_ref[0] = h[tt - SUBLANE:tt, :]
    ya_ref[0] = _rms(h * jax.nn.gelu(ga_ref[0]), gn_ref[...])


def _lru(z3, conv_buf, h0, lp, tt, valid=None):
    b, t, _ = z3.shape
    pad = max(tt // 2, SUBLANE)
    vec = pl.BlockSpec((1, D_LRU), lambda bi, ti: (0, 0))
    mat = pl.BlockSpec((D_LRU, D_LRU), lambda bi, ti: (0, 0))
    return pl.pallas_call(
        functools.partial(_lru_kernel, tt=tt, pad=pad, valid=valid),
        grid=(b, t // tt),
        in_specs=[pl.BlockSpec((1, tt, D_LRU), lambda bi, ti: (bi, ti, COL_XA)),
                  pl.BlockSpec((1, tt, D_LRU), lambda bi, ti: (bi, ti, COL_GA)),
                  pl.BlockSpec((1, SUBLANE, D_LRU), lambda bi, ti: (bi, 0, 0)),
                  pl.BlockSpec((1, 1, D_LRU), lambda bi, ti: (bi, 0, 0)),
                  pl.BlockSpec((CONV_W, D_LRU), lambda bi, ti: (0, 0)),
                  vec, mat, vec, mat, vec, vec, vec],
        out_specs=(pl.BlockSpec((1, tt, D_LRU), lambda bi, ti: (bi, ti, 0)),
                   pl.BlockSpec((1, SUBLANE, D_LRU), lambda bi, ti: (bi, 0, 0))),
        out_shape=(jax.ShapeDtypeStruct((b, t, D_LRU), F32),
                   jax.ShapeDtypeStruct((b, SUBLANE, D_LRU), F32)),
        scratch_shapes=[pltpu.VMEM((SUBLANE + tt, D_LRU), F32),
                        pltpu.VMEM((2, pad + tt, D_LRU), F32),
                        pltpu.VMEM((2, pad + tt, D_LRU), F32),
                        pltpu.VMEM((SUBLANE, D_LRU), F32)],
        compiler_params=_cparams("parallel", "arbitrary"),
        name="lru",
    )(z3, z3, conv_buf, h0, lp["conv_w"], lp["conv_b"], lp["wa"], lp["ba"], lp["wx"], lp["bx"],
      lp["lam"], lp["gn_a"])


@functools.lru_cache(maxsize=None)
def _hgrn_consts(c):
    t = np.arange(c)[:, None]
    j = np.arange(c)[None, :]
    blocks = [j <= t, j > t]
    masks = []
    w = c // 2
    while w >= 1:
        p = t % (2 * w)
        ref = t - p + w - 1
        blocks.append(np.where(p >= w, (j > ref) & (j <= t), (j > t) & (j <= ref)))
        masks.append((t // (2 * w) == j // (2 * w)) & (t % (2 * w) >= w) & (j % (2 * w) < w))
        w //= 2
    masks.append(t == j)
    dstack = np.concatenate(blocks, axis=0).astype(np.float32)
    masks = np.stack([np.tile(m, (N_HGRN, 1)) for m in masks]).astype(np.float32)
    lane_head = np.arange(D_HGRN)[None, :] // HGRN_DK
    head_rows = np.repeat(np.arange(N_HGRN), c)[:, None]
    head_mask = (head_rows == lane_head).astype(np.float32)
    same_head = (np.arange(D_HGRN)[:, None] // HGRN_DV == lane_head).astype(np.float32)
    return dstack, masks, head_mask, same_head


def _hgrn_kernel(q_ref, f_ref, v_ref, g_ref, lbraw_ref, ng_ref, s0_ref, dst_ref, msk_ref, hm_ref, sh_ref,
                 y_ref, st_ref, st_scr, *, c, chunks, layer, valid):
    ti = pl.program_id(1)

    @pl.when(ti == 0)
    def _():
        st_scr[...] = s0_ref[0]

    lbraw = lbraw_ref[...]
    e = jnp.exp(lbraw - jnp.max(lbraw, axis=0, keepdims=True))
    sm = e / jnp.sum(e, axis=0, keepdims=True)
    lrow = lax.broadcasted_iota(jnp.int32, sm.shape, 0)
    lb = jnp.sum(jnp.where((lrow >= 1) & (lrow <= layer), sm, 0.0), axis=0, keepdims=True)

    n_levels = msk_ref.shape[0] - 1
    hm = hm_ref[...]
    same_head = sh_ref[...]
    for ci in range(chunks):
        rows = slice(ci * c, (ci + 1) * c)
        q = q_ref[0, rows, :]
        v = v_ref[0, rows, :]
        g = g_ref[0, rows, :]
        f = lb + (1.0 - lb) * jax.nn.sigmoid(f_ref[0, rows, :])
        log_f = jnp.log(f)
        k = 1.0 - f
        if valid is not None:
            keep = lax.broadcasted_iota(jnp.int32, (c, D_HGRN), 0) < valid
            log_f = jnp.where(keep, log_f, 0.0)
            k = jnp.where(keep, k, 0.0)
        ex = _dot_exact_lhs(dst_ref[...], log_f)
        cum = ex[0:c]
        rem = ex[c:2 * c]

        sc = jnp.zeros((N_HGRN * c, c), F32)
        for li in range(n_levels + 1):
            if li < n_levels:
                decay = jnp.exp(ex[(2 + li) * c:(3 + li) * c])
                qt, kt = q * decay, k * decay
            else:
                qt, kt = q, k
            q4 = (jnp.concatenate([qt] * N_HGRN, axis=0) * hm).astype(BF16)
            sc = sc + _dot_nt(q4, kt.astype(BF16)) * msk_ref[li]
        vb = v.astype(BF16)
        o4 = _dot(sc.astype(BF16), vb) * hm
        o = o4[0:c]
        for hd in range(1, N_HGRN):
            o = o + o4[hd * c:(hd + 1) * c]

        st = st_scr[...]
        o = o + _dot_nt((q * jnp.exp(cum)).astype(BF16), st.astype(BF16))
        upd = _dot_tn(vb, (k * jnp.exp(rem)).astype(BF16))
        st_scr[...] = st * jnp.exp(cum[c - 1:c, :]) + upd * same_head

        o2 = o * o
        o2_hi = o2.astype(BF16)
        o2_lo = (o2 - o2_hi.astype(F32)).astype(BF16)
        shb = same_head.astype(BF16)
        ms = (_dot(o2_hi, shb) + _dot(o2_lo, shb)) * (1.0 / HGRN_DV)
        y_ref[0, rows, :] = o * lax.rsqrt(ms + EPS) * ng_ref[...] * (g * jax.nn.sigmoid(g))
    st_ref[0] = st_scr[...]


def _hgrn(z3, s0t, lb_raw, norm_g, layer, c, chunks, valid=None):
    b, t, _ = z3.shape
    tt = c * chunks
    dstack, masks, head_mask, same_head = _hgrn_consts(c)
    const2 = lambda bi, ti: (0, 0)
    col = lambda cidx: pl.BlockSpec((1, tt, D_HGRN), lambda bi, ti: (bi, ti, cidx))
    return pl.pallas_call(
        functools.partial(_hgrn_kernel, c=c, chunks=chunks, layer=layer, valid=valid),
        grid=(b, t // tt),
        in_specs=[col(COL_QB), col(COL_FB), col(COL_IB), col(COL_GB),
                  pl.BlockSpec((DEPTH, D_HGRN), const2),
                  pl.BlockSpec((1, D_HGRN), const2),
                  pl.BlockSpec((1, D_HGRN, D_HGRN), lambda bi, ti: (bi, 0, 0)),
                  pl.BlockSpec(dstack.shape, const2),
                  pl.BlockSpec(masks.shape, lambda bi, ti: (0, 0, 0)),
                  pl.BlockSpec(head_mask.shape, const2),
                  pl.BlockSpec(same_head.shape, const2)],
        out_specs=(pl.BlockSpec((1, tt, D_HGRN), lambda bi, ti: (bi, ti, 0)),
                   pl.BlockSpec((1, D_HGRN, D_HGRN), lambda bi, ti: (bi, 0, 0))),
        out_shape=(jax.ShapeDtypeStruct((b, t, D_HGRN), F32),
                   jax.ShapeDtypeStruct((b, D_HGRN, D_HGRN), F32)),
        scratch_shapes=[pltpu.VMEM((D_HGRN, D_HGRN), F32)],
        compiler_params=_cparams("parallel", "arbitrary"),
        name="hgrn",
    )(z3, z3, z3, z3, lb_raw, jnp.tile(norm_g, N_HGRN).reshape(1, D_HGRN), s0t,
      jnp.asarray(dstack, BF16), jnp.asarray(masks), jnp.asarray(head_mask), jnp.asarray(same_head))


def _state_to_blockdiag_t(s):
    b = s.shape[0]
    eye = jnp.eye(N_HGRN, dtype=s.dtype)
    return jnp.einsum("bhkv,hg->bhvgk", s, eye).reshape(b, D_HGRN, D_HGRN)


def _blockdiag_t_to_state(st):
    b = st.shape[0]
    x = st.reshape(b, N_HGRN, HGRN_DV, N_HGRN, HGRN_DK)
    diag = jnp.stack([x[:, h, :, h, :] for h in range(N_HGRN)], axis=1)
    return jnp.swapaxes(diag, 2, 3)


def _band_attn_kernel(q_ref, kp_ref, kc_ref, vp_ref, vc_ref, o_ref, l_ref, *, dil):
    blk = pl.program_id(2)
    q = q_ref[0]
    k = jnp.concatenate([kp_ref[0], kc_ref[0]], axis=0).astype(BF16)
    v = jnp.concatenate([vp_ref[0], vc_ref[0]], axis=0).astype(BF16)
    n2 = 2 * BAND
    row = lax.broadcasted_iota(jnp.int32, (n2, n2), 0)
    col = lax.broadcasted_iota(jnp.int32, (n2, n2), 1)
    back = BAND + (row & (BAND - 1)) - col
    valid = (back >= 0) & (back <= BAND) & ((col >= BAND) | (blk > 0))
    dist = (back * dil).astype(F32)
    lane = lax.broadcasted_iota(jnp.int32, (BAND, 2 * ATT_DH), 1)
    first = lane < ATT_DH
    scale = ATT_DH ** -0.5
    for pair in range(N_ATT // 2):
        lanes = slice(pair * 2 * ATT_DH, (pair + 1) * 2 * ATT_DH)
        qp = q[:, lanes]
        q2 = jnp.concatenate([jnp.where(first, qp, 0.0), jnp.where(first, 0.0, qp)], axis=0).astype(BF16)
        slope = jnp.where(row < BAND, ALIBI_SLOPES[2 * pair], ALIBI_SLOPES[2 * pair + 1])
        s = _dot_nt(q2, k[:, lanes]) * scale - slope * dist
        s = jnp.where(valid, s, NEG_BIG)
        m = jnp.max(s, axis=1, keepdims=True)
        e = jnp.exp(s - m)
        den = jnp.sum(e, axis=1, keepdims=True)
        pv = _dot(e.astype(BF16), v[:, lanes]) / den
        lse = jnp.broadcast_to(m + jnp.log(den), (n2, 2 * ATT_DH))
        o_ref[0, :, lanes] = jnp.where(first, pv[0:BAND], pv[BAND:n2])
        l_ref[0, :, lanes] = jnp.where(first, lse[0:BAND], lse[BAND:n2])


def _band_attn(z3, dil):
    b, t, _ = z3.shape
    ts = t // dil
    zv = z3.reshape(b, ts, dil * D_IN)
    blk = lambda colblk, prev: pl.BlockSpec(
        (1, BAND, D_ATT),
        (lambda bi, r, i: (bi, jnp.maximum(i - 1, 0), r * ATT_COLS_PER_ROW + colblk)) if prev
        else (lambda bi, r, i: (bi, i, r * ATT_COLS_PER_ROW + colblk)))
    out_spec = pl.BlockSpec((1, BAND, D_ATT), lambda bi, r, i: (bi, i, r))
    out_sds = jax.ShapeDtypeStruct((b, ts, dil * D_ATT), F32)
    o, lse = pl.pallas_call(
        functools.partial(_band_attn_kernel, dil=dil),
        grid=(b, dil, ts // BAND),
        in_specs=[blk(COL_QC, False), blk(COL_KC, True), blk(COL_KC, False),
                  blk(COL_VC, True), blk(COL_VC, False)],
        out_specs=(out_spec, out_spec),
        out_shape=(out_sds, out_sds),
        compiler_params=_cparams("parallel", "parallel", "parallel"),
        name="band_attn",
    )(zv, zv, zv, zv, zv)
    return o.reshape(b * t, D_ATT), lse.reshape(b * t, D_ATT)


def _cache_attn_kernel(q_ref, kn_ref, vn_ref, kc_ref, vc_ref, o_ref, ko_ref, vo_ref, *, n_new):
    w = kc_ref.shape[1]
    kn = kn_ref[0]
    vn = vn_ref[0]
    ko_ref[0, 0:w - n_new, :] = kc_ref[0, n_new:w, :]
    ko_ref[0, w - n_new:w, :] = kn[0:n_new]
    vo_ref[0, 0:w - n_new, :] = vc_ref[0, n_new:w, :]
    vo_ref[0, w - n_new:w, :] = vn[0:n_new]

    rows = N_ATT * SAMPLE_ROWS
    lane_head = lax.broadcasted_iota(jnp.int32, (rows, D_ATT), 1) >> _log2(ATT_DH)
    row_head = lax.broadcasted_iota(jnp.int32, (rows, D_ATT), 0) >> _log2(SAMPLE_ROWS)
    own = lane_head == row_head
    q8 = jnp.where(own, jnp.concatenate([q_ref[0]] * N_ATT, axis=0), 0.0).astype(BF16)
    scale = ATT_DH ** -0.5
    head_col = lax.broadcasted_iota(jnp.int32, (rows, 1), 0) >> _log2(SAMPLE_ROWS)
    slope = jnp.zeros((rows, 1), F32)
    for hd in range(N_ATT):
        slope = jnp.where(head_col == hd, ALIBI_SLOPES[hd], slope)

    def weights(dist):
        mult = jnp.zeros(dist.shape, F32)
        for dil in DILATIONS:
            hit = (dist >= 0) & (dist <= BAND * dil) & ((dist & (dil - 1)) == 0)
            mult = mult + jnp.where(hit, 1.0, 0.0)
        return mult

    def scores(keys, dist):
        s = _dot_nt(q8, keys.astype(BF16)) * scale - slope * dist.astype(F32)
        mult = weights(dist)
        return jnp.where(mult > 0.0, s, NEG_BIG), mult

    t_c = lax.broadcasted_iota(jnp.int32, (rows, w), 0) & (SAMPLE_ROWS - 1)
    pos_c = lax.broadcasted_iota(jnp.int32, (rows, w), 1)
    s_c, mult_c = scores(kc_ref[0], w + t_c - pos_c)
    t_n = lax.broadcasted_iota(jnp.int32, (rows, SAMPLE_ROWS), 0) & (SAMPLE_ROWS - 1)
    pos_n = lax.broadcasted_iota(jnp.int32, (rows, SAMPLE_ROWS), 1)
    s_n, mult_n = scores(kn, t_n - pos_n)

    m = jnp.maximum(jnp.max(s_c, axis=1, keepdims=True), jnp.max(s_n, axis=1, keepdims=True))
    e_c = jnp.exp(s_c - m) * mult_c
    e_n = jnp.exp(s_n - m) * mult_n
    den = jnp.sum(e_c, axis=1, keepdims=True) + jnp.sum(e_n, axis=1, keepdims=True)
    o8 = (_dot(e_c.astype(BF16), vc_ref[0].astype(BF16)) + _dot(e_n.astype(BF16), vn.astype(BF16))) / den
    o8 = jnp.where(own, o8, 0.0)
    o = o8[0:SAMPLE_ROWS]
    for hd in range(1, N_ATT):
        o = o + o8[hd * SAMPLE_ROWS:(hd + 1) * SAMPLE_ROWS]
    o_ref[0] = o


def _cache_attn(z3, cache_k, cache_v, n_new):
    b, w = cache_k.shape[0], cache_k.shape[1]
    new = lambda colblk: pl.BlockSpec((1, SAMPLE_ROWS, D_ATT), lambda bi: (bi, 0, colblk))
    cache = pl.BlockSpec((1, w, D_ATT), lambda bi: (bi, 0, 0))
    return pl.pallas_call(
        functools.partial(_cache_attn_kernel, n_new=n_new),
        grid=(b,),
        in_specs=[new(COL_QC), new(COL_KC), new(COL_VC), cache, cache],
        out_specs=(pl.BlockSpec((1, SAMPLE_ROWS, D_ATT), lambda bi: (bi, 0, 0)), cache, cache),
        out_shape=(jax.ShapeDtypeStruct((b, SAMPLE_ROWS, D_ATT), F32),
                   jax.ShapeDtypeStruct((b, w, D_ATT), F32),
                   jax.ShapeDtypeStruct((b, w, D_ATT), F32)),
        compiler_params=_cparams("parallel"),
        name="cache_attn",
    )(z3, z3, z3, cache_k.reshape(b, w, D_ATT), cache_v.reshape(b, w, D_ATT))


def _wout_kernel(*refs, n_pat):
    x_ref, ya_ref, yb_ref = refs[0:3]
    n_lse = n_pat if n_pat > 1 else 0
    o_refs = refs[3:3 + n_pat]
    l_refs = refs[3 + n_pat:3 + n_pat + n_lse]
    gn_ref, w_ref, out_ref = refs[3 + n_pat + n_lse:]
    if n_pat == 1:
        yc = o_refs[0][...]
    else:
        lses = [l[...] for l in l_refs]
        m = functools.reduce(jnp.maximum, lses)
        ws = [jnp.exp(l - m) for l in lses]
        yc = sum(wgt * o[...] for wgt, o in zip(ws, o_refs)) / sum(ws)
    ycn = _rms(yc, gn_ref[...])
    acc = _dot(ya_ref[...].astype(BF16), w_ref[0:D_LRU, :])
    acc = acc + _dot(yb_ref[...].astype(BF16), w_ref[D_LRU:D_LRU + D_HGRN, :])
    acc = acc + _dot(ycn.astype(BF16), w_ref[D_LRU + D_HGRN:D_MODEL, :])
    out_ref[...] = x_ref[...] + acc


def _wout(x, ya, yb, outs, lses, gn_c, w_out, tm):
    m = x.shape[0]
    n_pat = len(outs)
    lses = list(lses) if n_pat > 1 else []
    spec = lambda n: pl.BlockSpec((tm, n), lambda i: (i, 0))
    return pl.pallas_call(
        functools.partial(_wout_kernel, n_pat=n_pat),
        grid=(m // tm,),
        in_specs=[spec(D_MODEL), spec(D_LRU), spec(D_HGRN)] + [spec(D_ATT)] * (n_pat + len(lses))
        + [pl.BlockSpec((1, D_ATT), lambda i: (0, 0)), pl.BlockSpec((D_MODEL, D_MODEL), lambda i: (0, 0))],
        out_specs=spec(D_MODEL),
        out_shape=jax.ShapeDtypeStruct((m, D_MODEL), F32),
        compiler_params=_cparams("parallel"),
        name="wout",
    )(x, ya, yb, *outs, *lses, gn_c.reshape(1, D_ATT), w_out)


def _xattn_kernel(x_ref, g_ref, wq_ref, mk_ref, mv_ref, wo_ref, o_ref):
    x = x_ref[0]
    q = _dot(_rms(x, g_ref[...]).astype(BF16), wq_ref[...])
    mk = mk_ref[0].astype(BF16)
    mv = mv_ref[0].astype(BF16)
    acc = x
    scale = XHEAD_DIM ** -0.5
    for hd in range(N_XHEADS):
        lanes = slice(hd * XHEAD_DIM, (hd + 1) * XHEAD_DIM)
        s = _dot_nt(q[:, lanes].astype(BF16), mk[:, lanes]) * scale
        e = jnp.exp(s - jnp.max(s, axis=1, keepdims=True))
        p = e / jnp.sum(e, axis=1, keepdims=True)
        oh = _dot(p.astype(BF16), mv[:, lanes])
        acc = acc + _dot(oh.astype(BF16), wo_ref[lanes, :])
    o_ref[0] = acc


def _xattn(x3, g, wq, mem, k_col, v_col, wo, tm):
    b, t, _ = x3.shape
    const = lambda bi, ti: (0, 0)
    mat = pl.BlockSpec((D_MODEL, D_MODEL), const)
    mem_k, mem_v = mem
    return pl.pallas_call(
        _xattn_kernel,
        grid=(b, t // tm),
        in_specs=[pl.BlockSpec((1, tm, D_MODEL), lambda bi, ti: (bi, ti, 0)),
                  pl.BlockSpec((1, D_MODEL), const), mat,
                  pl.BlockSpec((1, N_MEM, D_MODEL), lambda bi, ti: (bi, 0, k_col)),
                  pl.BlockSpec((1, N_MEM, D_MODEL), lambda bi, ti: (bi, 0, v_col)),
                  mat],
        out_specs=pl.BlockSpec((1, tm, D_MODEL), lambda bi, ti: (bi, ti, 0)),
        out_shape=jax.ShapeDtypeStruct((b, t, D_MODEL), F32),
        compiler_params=_cparams("parallel", "parallel"),
        name="xattn",
    )(x3, g.reshape(1, D_MODEL), wq, mem_k, mem_v, wo)


def _block_diag(w):
    n, c, d = w.shape
    eye = jnp.eye(n, dtype=w.dtype)
    return jnp.einsum("ncd,nm->ncmd", w, eye).reshape(n * c, n * d)


def kernel(x_prompt, x_sample, state_lru_h, state_lru_conv, state_hgrn, cache_swa_k, cache_swa_v, cache_mem_k, cache_mem_v, mem_prompt, n_ffn1, ffn1_wg, ffn1_wu, ffn1_wd, n_mix, w_in, lru_conv_w, lru_conv_b, lru_wa, lru_ba, lru_wx, lru_bx, lru_lambda, hgrn_lb, hgrn_norm, gn_a, gn_c, w_out, n_cross, x_wq, x_wk, x_wv, x_wo, n_ffn2, ffn2_wg, ffn2_wu, ffn2_wd, n_final):
    bp, tp, _ = x_prompt.shape
    bs, ts_new, _ = x_sample.shape
    w_buf = cache_swa_k.shape[2]
    assert ts_new <= SAMPLE_ROWS and tp % (BAND * max(DILATIONS)) == 0 and tp >= MAX_WINDOW
    tm_p = 512
    tm_s = bs * SAMPLE_ROWS

    xp = x_prompt.reshape(bp * tp, D_MODEL)
    xs = jnp.pad(x_sample, ((0, 0), (0, SAMPLE_ROWS - ts_new), (0, 0))).reshape(tm_s, D_MODEL)
    mem2 = mem_prompt.reshape(bp * N_MEM, D_MODEL)
    ones = jnp.ones((D_MODEL,), F32)

    outs = {name: [] for name in ("p_h", "p_c", "p_s", "p_k", "p_v", "p_mk", "p_mv",
                                  "s_h", "s_c", "s_s", "s_k", "s_v")}
    yp = ys = None
    for l in range(DEPTH):
        bf = lambda w: w[l].astype(BF16)
        wg1, wu1, wd1 = bf(ffn1_wg), bf(ffn1_wu), bf(ffn1_wd)
        wg2, wu2, wd2 = bf(ffn2_wg), bf(ffn2_wu), bf(ffn2_wd)
        win, wo_mix, wq, wo_x = bf(w_in), bf(w_out), bf(x_wq), bf(x_wo)
        wkv = jnp.concatenate([x_wk[l], x_wv[l]], axis=1).astype(BF16)
        lp = dict(conv_w=lru_conv_w[l], conv_b=lru_conv_b[l].reshape(1, D_LRU),
                  wa=_block_diag(lru_wa[l]).astype(BF16), ba=lru_ba[l].reshape(1, D_LRU),
                  wx=_block_diag(lru_wx[l]).astype(BF16), bx=lru_bx[l].reshape(1, D_LRU),
                  lam=lru_lambda[l].reshape(1, D_LRU), gn_a=gn_a[l].reshape(1, D_LRU))
        last = l == DEPTH - 1

        mkv = _proj(mem2, ones, wkv, tm=bp * N_MEM, norm=False).reshape(bp, N_MEM, 2 * D_MODEL)
        xp = _ffn(xp, n_ffn1[l], wg1, wu1, wd1, tm_p)
        z3 = _proj(xp, n_mix[l], win, tm_p).reshape(bp, tp, D_IN)
        ya, hl = _lru(z3, jnp.zeros((bp, SUBLANE, D_LRU), F32), jnp.zeros((bp, 1, D_LRU), F32), lp, tt=256)
        yb, st = _hgrn(z3, jnp.zeros((bp, D_HGRN, D_HGRN), F32), hgrn_lb, hgrn_norm[l], l,
                       c=HGRN_CHUNK, chunks=4)
        att = [_band_attn(z3, dil) for dil in DILATIONS]
        xp = _wout(xp, ya.reshape(bp * tp, D_LRU), yb.reshape(bp * tp, D_HGRN),
                   [a[0] for a in att], [a[1] for a in att], gn_c[l], wo_mix, tm_p)
        xp = _xattn(xp.reshape(bp, tp, D_MODEL), n_cross[l], wq, (mkv, mkv), 0, 1, wo_x, tm_p)
        xp = xp.reshape(bp * tp, D_MODEL)
        if last:
            xp, yp = _ffn(xp, n_ffn2[l], wg2, wu2, wd2, tm_p, g_final=n_final)
        else:
            xp = _ffn(xp, n_ffn2[l], wg2, wu2, wd2, tm_p)
        outs["p_h"].append(hl[:, SUBLANE - 1])
        outs["p_c"].append(z3[:, tp - (CONV_W - 1):, 0:D_LRU])
        outs["p_s"].append(_blockdiag_t_to_state(st))
        keep = min(MAX_WINDOW, tp)
        kcol = COL_KC * D_ATT
        vcol = COL_VC * D_ATT
        outs["p_k"].append(z3[:, tp - keep:, kcol:kcol + D_ATT].reshape(bp, keep, N_ATT, ATT_DH))
        outs["p_v"].append(z3[:, tp - keep:, vcol:vcol + D_ATT].reshape(bp, keep, N_ATT, ATT_DH))
        outs["p_mk"].append(mkv[:, :, 0:D_MODEL].reshape(bp, N_MEM, N_XHEADS, XHEAD_DIM))
        outs["p_mv"].append(mkv[:, :, D_MODEL:].reshape(bp, N_MEM, N_XHEADS, XHEAD_DIM))

        xs = _ffn(xs, n_ffn1[l], wg1, wu1, wd1, tm_s)
        zs3 = _proj(xs, n_mix[l], win, tm_s).reshape(bs, SAMPLE_ROWS, D_IN)
        conv_buf = jnp.pad(state_lru_conv[l], ((0, 0), (SUBLANE - (CONV_W - 1), 0), (0, 0)))
        ya, hl = _lru(zs3, conv_buf, state_lru_h[l].reshape(bs, 1, D_LRU), lp, tt=SAMPLE_ROWS, valid=ts_new)
        yb, st = _hgrn(zs3, _state_to_blockdiag_t(state_hgrn[l]), hgrn_lb, hgrn_norm[l], l,
                       c=SAMPLE_ROWS, chunks=1, valid=ts_new)
        yc, ko, vo = _cache_attn(zs3, cache_swa_k[l], cache_swa_v[l], ts_new)
        xs = _wout(xs, ya.reshape(tm_s, D_LRU), yb.reshape(tm_s, D_HGRN), [yc.reshape(tm_s, D_ATT)], [],
                   gn_c[l], wo_mix, tm_s)
        mem_k = cache_mem_k[l].reshape(bs, N_MEM, D_MODEL)
        mem_v = cache_mem_v[l].reshape(bs, N_MEM, D_MODEL)
        xs = _xattn(xs.reshape(bs, SAMPLE_ROWS, D_MODEL), n_cross[l], wq, (mem_k, mem_v), 0, 0, wo_x,
                    SAMPLE_ROWS).reshape(tm_s, D_MODEL)
        if last:
            xs, ys = _ffn(xs, n_ffn2[l], wg2, wu2, wd2, tm_s, g_final=n_final)
        else:
            xs = _ffn(xs, n_ffn2[l], wg2, wu2, wd2, tm_s)
        outs["s_h"].append(hl[:, SUBLANE - 1])
        outs["s_c"].append(zs3[:, ts_new - (CONV_W - 1):ts_new, 0:D_LRU])
        outs["s_s"].append(_blockdiag_t_to_state(st))
        outs["s_k"].append(ko.reshape(bs, w_buf, N_ATT, ATT_DH))
        outs["s_v"].append(vo.reshape(bs, w_buf, N_ATT, ATT_DH))

    y_prompt = yp.reshape(bp, tp, D_MODEL)
    y_sample = ys.reshape(bs, SAMPLE_ROWS, D_MODEL)[:, :ts_new]
    stack = lambda name: jnp.stack(outs[name])
    return (y_prompt, y_sample,
            stack("p_h"), stack("p_c"), stack("p_s"), stack("p_k"), stack("p_v"), stack("p_mk"), stack("p_mv"),
            stack("s_h"), stack("s_c"), stack("s_s"), stack("s_k"), stack("s_v"))
```

```python
import functools

import numpy as np
import jax
import jax.numpy as jnp
from jax import lax
from jax.experimental import pallas as pl
from jax.experimental.pallas import tpu as pltpu

F32 = jnp.float32
BF16 = jnp.bfloat16

D_MODEL = 1024
DEPTH = 4
D_LRU = 256
N_LRU_BLOCKS = 4
CONV_W = 4
LRU_C = 8.0
HGRN_DK = 64
HGRN_DV = 64
D_HGRN = 256
N_HGRN = 4
HGRN_CHUNK = 64
ATT_DH = 64
D_ATT = 512
N_ATT = 8
DILATIONS = (1, 4, 16)
BAND = 128
SUPER = BAND * max(DILATIONS)
PAIR = 2 * ATT_DH
MAX_WINDOW = 2048
N_MEM = 256
N_XHEADS = 4
XHEAD_DIM = 256
D_FF = 2816
D_IN = 3072
EPS = 1e-6

COL_XA, COL_GA, COL_QB, COL_FB, COL_IB, COL_GB = 0, 1, 2, 3, 4, 5
COL_QC, COL_KC, COL_VC = 3, 4, 5
ATT_COLS_PER_ROW = D_IN // D_ATT

HALF_ATT = D_ATT // 2
SUBLANE = 8
LANE = 128
SAMPLE_ROWS = 8
VMEM_LIMIT = 56 * 1024 * 1024
NEG_BIG = -1e30

ALIBI_SLOPES = tuple(float(2.0 ** (-8.0 * (h + 1) / N_ATT)) for h in range(N_ATT))


def _log2(n):
    assert n & (n - 1) == 0
    return n.bit_length() - 1


def _cparams(*sem):
    return pltpu.CompilerParams(dimension_semantics=sem, vmem_limit_bytes=VMEM_LIMIT)


def _rms(x, g):
    ms = jnp.mean(x * x, axis=-1, keepdims=True)
    return x * lax.rsqrt(ms + EPS) * g


def _expm1(x):
    e = jnp.exp(x)
    near = (e - 1.0) * x / jnp.log(e)
    return jnp.where(e == 1.0, x, jnp.where(jnp.abs(x) > 1.0, e - 1.0, near))


def _dot(a, b):
    return jnp.dot(a, b, preferred_element_type=F32)


def _dot_nt(a, b):
    return lax.dot_general(a, b, (((1,), (1,)), ((), ())), preferred_element_type=F32)


def _dot_tn(a, b):
    return lax.dot_general(a, b, (((0,), (0,)), ((), ())), preferred_element_type=F32)


def _split3(x):
    hi = x.astype(BF16)
    r1 = x - hi.astype(F32)
    mid = r1.astype(BF16)
    lo = (r1 - mid.astype(F32)).astype(BF16)
    return hi, mid, lo


def _dot_exact_lhs(a_bf16, x):
    return sum(_dot(a_bf16, p) for p in _split3(x))


def _dot_exact_tn(x, a_bf16):
    return sum(_dot_tn(p, a_bf16) for p in _split3(x))


def _ffn_kernel(*refs, final_norm):
    if final_norm:
        x_ref, g_ref, wg_ref, wu_ref, wd_ref, gf_ref, o_ref, y_ref, h_scr, acc_scr = refs
    else:
        x_ref, g_ref, wg_ref, wu_ref, wd_ref, o_ref, h_scr, acc_scr = refs
    j = pl.program_id(1)

    @pl.when(j == 0)
    def _():
        h_scr[...] = _rms(x_ref[...], g_ref[...]).astype(BF16)
        acc_scr[...] = jnp.zeros_like(acc_scr)

    h = h_scr[...]
    gate = _dot(h, wg_ref[...])
    up = _dot(h, wu_ref[...])
    act = (gate * jax.nn.sigmoid(gate) * up).astype(BF16)
    acc_scr[...] += _dot(act, wd_ref[...])

    @pl.when(j == pl.num_programs(1) - 1)
    def _():
        out = x_ref[...] + 0.5 * acc_scr[...]
        o_ref[...] = out
        if final_norm:
            y_ref[...] = _rms(out, gf_ref[...])


def _ffn(x, g, wg, wu, wd, tm, g_final=None):
    m = x.shape[0]
    tf = D_FF // 2
    final_norm = g_final is not None
    row = pl.BlockSpec((tm, D_MODEL), lambda i, j: (i, 0))
    vec = pl.BlockSpec((1, D_MODEL), lambda i, j: (0, 0))
    in_specs = [row, vec,
                pl.BlockSpec((D_MODEL, tf), lambda i, j: (0, j)),
                pl.BlockSpec((D_MODEL, tf), lambda i, j: (0, j)),
                pl.BlockSpec((tf, D_MODEL), lambda i, j: (j, 0))]
    args = [x, g.reshape(1, D_MODEL), wg, wu, wd]
    out_shape = jax.ShapeDtypeStruct((m, D_MODEL), F32)
    out_specs = row
    if final_norm:
        in_specs.append(vec)
        args.append(g_final.reshape(1, D_MODEL))
        out_shape = (out_shape, out_shape)
        out_specs = (row, row)
    return pl.pallas_call(
        functools.partial(_ffn_kernel, final_norm=final_norm),
        grid=(m // tm, D_FF // tf),
        in_specs=in_specs, out_specs=out_specs, out_shape=out_shape,
        scratch_shapes=[pltpu.VMEM((tm, D_MODEL), BF16), pltpu.VMEM((tm, D_MODEL), F32)],
        compiler_params=_cparams("parallel", "arbitrary"),
        name="ffn",
    )(*args)


def _proj_kernel(x_ref, g_ref, w_ref, o_ref, *, norm):
    x = x_ref[...]
    if norm:
        x = _rms(x, g_ref[...])
    o_ref[...] = _dot(x.astype(BF16), w_ref[...])


def _proj(x, g, w, tm, norm=True):
    m, k = x.shape
    n = w.shape[1]
    return pl.pallas_call(
        functools.partial(_proj_kernel, norm=norm),
        grid=(m // tm,),
        in_specs=[pl.BlockSpec((tm, k), lambda i: (i, 0)),
                  pl.BlockSpec((1, k), lambda i: (0, 0)),
                  pl.BlockSpec((k, n), lambda i: (0, 0))],
        out_specs=pl.BlockSpec((tm, n), lambda i: (i, 0)),
        out_shape=jax.ShapeDtypeStruct((m, n), F32),
        compiler_params=_cparams("parallel"),
        name="proj",
    )(x, g.reshape(1, k), w)


def _lru_kernel(xa_ref, ga_ref, cb_ref, h0_ref, cw_ref, cbias_ref, wa_ref, ba_ref, wx_ref, bx_ref,
                lam_ref, gn_ref, ya_ref, hl_ref, xbuf, abuf, ubuf, hc, *, tt, pad, valid):
    ti = pl.program_id(1)

    @pl.when(ti == 0)
    def _():
        xbuf[0:SUBLANE, :] = cb_ref[0]
        hc[...] = jnp.broadcast_to(h0_ref[0], (SUBLANE, D_LRU))
        for s in range(2):
            abuf[s, 0:pad, :] = jnp.ones((pad, D_LRU), F32)
            ubuf[s, 0:pad, :] = jnp.zeros((pad, D_LRU), F32)

    x = xa_ref[0]
    xbuf[SUBLANE:SUBLANE + tt, :] = x
    y = cbias_ref[...] + cw_ref[CONV_W - 1:CONV_W, :] * x
    for tap in range(CONV_W - 1):
        back = CONV_W - 1 - tap
        y = y + cw_ref[tap:tap + 1, :] * xbuf[SUBLANE - back:SUBLANE - back + tt, :]
    xbuf[0:SUBLANE, :] = x[tt - SUBLANE:tt, :]

    yb = y.astype(BF16)
    r = jax.nn.sigmoid(_dot(yb, wa_ref[...]) + ba_ref[...])
    i = jax.nn.sigmoid(_dot(yb, wx_ref[...]) + bx_ref[...])
    lam = lam_ref[...]
    softplus_neg_lam = jnp.maximum(-lam, 0.0) + jnp.log1p(jnp.exp(-jnp.abs(lam)))
    log_a = -LRU_C * r * softplus_neg_lam
    a = jnp.exp(log_a)
    u = jnp.sqrt(-_expm1(2.0 * log_a)) * (i * y)
    if valid is not None:
        keep = lax.broadcasted_iota(jnp.int32, (tt, D_LRU), 0) < valid
        a = jnp.where(keep, a, 1.0)
        u = jnp.where(keep, u, 0.0)

    abuf[0, pad:pad + tt, :] = a
    ubuf[0, pad:pad + tt, :] = u
    cur, k = 0, 1
    while k < tt:
        a_c = abuf[cur, pad:pad + tt, :]
        u_c = ubuf[cur, pad:pad + tt, :]
        a_s = abuf[cur, pad - k:pad - k + tt, :]
        u_s = ubuf[cur, pad - k:pad - k + tt, :]
        abuf[1 - cur, pad:pad + tt, :] = a_c * a_s
        ubuf[1 - cur, pad:pad + tt, :] = a_c * u_s + u_c
        cur, k = 1 - cur, 2 * k
    h = abuf[cur, pad:pad + tt, :] * hc[SUBLANE - 1:SUBLANE, :] + ubuf[cur, pad:pad + tt, :]
    hc[...] = h[tt - SUBLANE:tt, :]
    hl_ref[0] = h[tt - SUBLANE:tt, :]
    ya_ref[0] = _rms(h * jax.nn.gelu(ga_ref[0]), gn_ref[...])


def _lru(z3, conv_buf, h0, lp, tt, valid=None):
    b, t, _ = z3.shape
    pad = max(tt // 2, SUBLANE)
    vec = pl.BlockSpec((1, D_LRU), lambda bi, ti: (0, 0))
    mat = pl.BlockSpec((D_LRU, D_LRU), lambda bi, ti: (0, 0))
    return pl.pallas_call(
        functools.partial(_lru_kernel, tt=tt, pad=pad, valid=valid),
        grid=(b, t // tt),
        in_specs=[pl.BlockSpec((1, tt, D_LRU), lambda bi, ti: (bi, ti, COL_XA)),
                  pl.BlockSpec((1, tt, D_LRU), lambda bi, ti: (bi, ti, COL_GA)),
                  pl.BlockSpec((1, SUBLANE, D_LRU), lambda bi, ti: (bi, 0, 0)),
                  pl.BlockSpec((1, 1, D_LRU), lambda bi, ti: (bi, 0, 0)),
                  pl.BlockSpec((CONV_W, D_LRU), lambda bi, ti: (0, 0)),
                  vec, mat, vec, mat, vec, vec, vec],
        out_specs=(pl.BlockSpec((1, tt, D_LRU), lambda bi, ti: (bi, ti, 0)),
                   pl.BlockSpec((1, SUBLANE, D_LRU), lambda bi, ti: (bi, 0, 0))),
        out_shape=(jax.ShapeDtypeStruct((b, t, D_LRU), F32),
                   jax.ShapeDtypeStruct((b, SUBLANE, D_LRU), F32)),
        scratch_shapes=[pltpu.VMEM((SUBLANE + tt, D_LRU), F32),
                        pltpu.VMEM((2, pad + tt, D_LRU), F32),
                        pltpu.VMEM((2, pad + tt, D_LRU), F32),
                        pltpu.VMEM((SUBLANE, D_LRU), F32)],
        compiler_params=_cparams("parallel", "arbitrary"),
        name="lru",
    )(z3, z3, conv_buf, h0, lp["conv_w"], lp["conv_b"], lp["wa"], lp["ba"], lp["wx"], lp["bx"],
      lp["lam"], lp["gn_a"])


@functools.lru_cache(maxsize=None)
def _hgrn_consts(c):
    t = np.arange(c)[:, None]
    j = np.arange(c)[None, :]
    blocks = [j <= t, j > t]
    masks = []
    w = c // 2
    while w >= 1:
        p = t % (2 * w)
        ref = t - p + w - 1
        blocks.append(np.where(p >= w, (j > ref) & (j <= t), (j > t) & (j <= ref)))
        masks.append((t // (2 * w) == j // (2 * w)) & (t % (2 * w) >= w) & (j % (2 * w) < w))
        w //= 2
    masks.append(t == j)
    dstack = np.concatenate(blocks, axis=0).astype(np.float32)
    masks = np.stack([np.tile(m, (N_HGRN, 1)) for m in masks]).astype(np.float32)
    lane_head = np.arange(D_HGRN)[None, :] // HGRN_DK
    head_rows = np.repeat(np.arange(N_HGRN), c)[:, None]
    head_mask = (head_rows == lane_head).astype(np.float32)
    same_head = (np.arange(D_HGRN)[:, None] // HGRN_DV == lane_head).astype(np.float32)
    return dstack, masks, head_mask, same_head


def _hgrn_kernel(q_ref, f_ref, v_ref, g_ref, lbraw_ref, ng_ref, s0_ref, dst_ref, msk_ref, hm_ref, sh_ref,
                 y_ref, st_ref, st_scr, *, c, chunks, layer, valid):
    ti = pl.program_id(1)

    @pl.when(ti == 0)
    def _():
        st_scr[...] = s0_ref[0]

    lbraw = lbraw_ref[...]
    e = jnp.exp(lbraw - jnp.max(lbraw, axis=0, keepdims=True))
    sm = e / jnp.sum(e, axis=0, keepdims=True)
    lrow = lax.broadcasted_iota(jnp.int32, sm.shape, 0)
    lb = jnp.sum(jnp.where((lrow >= 1) & (lrow <= layer), sm, 0.0), axis=0, keepdims=True)

    n_levels = msk_ref.shape[0] - 1
    hm = hm_ref[...]
    same_head = sh_ref[...]
    for ci in range(chunks):
        rows = slice(ci * c, (ci + 1) * c)
        q = q_ref[0, rows, :]
        v = v_ref[0, rows, :]
        g = g_ref[0, rows, :]
        f = lb + (1.0 - lb) * jax.nn.sigmoid(f_ref[0, rows, :])
        log_f = jnp.log(f)
        k = 1.0 - f
        if valid is not None:
            keep = lax.broadcasted_iota(jnp.int32, (c, D_HGRN), 0) < valid
            log_f = jnp.where(keep, log_f, 0.0)
            k = jnp.where(keep, k, 0.0)
        ex = _dot_exact_lhs(dst_ref[...], log_f)
        cum = ex[0:c]
        rem = ex[c:2 * c]

        sc = jnp.zeros((N_HGRN * c, c), F32)
        for li in range(n_levels + 1):
            if li < n_levels:
                decay = jnp.exp(ex[(2 + li) * c:(3 + li) * c])
                qt, kt = q * decay, k * decay
            else:
                qt, kt = q, k
            q4 = (jnp.concatenate([qt] * N_HGRN, axis=0) * hm).astype(BF16)
            sc = sc + _dot_nt(q4, kt.astype(BF16)) * msk_ref[li]
        vb = v.astype(BF16)
        o4 = _dot(sc.astype(BF16), vb) * hm
        o = o4[0:c]
        for hd in range(1, N_HGRN):
            o = o + o4[hd * c:(hd + 1) * c]

        st = st_scr[...]
        o = o + _dot_nt((q * jnp.exp(cum)).astype(BF16), st.astype(BF16))
        upd = _dot_tn(vb, (k * jnp.exp(rem)).astype(BF16))
        st_scr[...] = st * jnp.exp(cum[c - 1:c, :]) + upd * same_head

        o2 = o * o
        o2_hi = o2.astype(BF16)
        o2_lo = (o2 - o2_hi.astype(F32)).astype(BF16)
        shb = same_head.astype(BF16)
        ms = (_dot(o2_hi, shb) + _dot(o2_lo, shb)) * (1.0 / HGRN_DV)
        y_ref[0, rows, :] = o * lax.rsqrt(ms + EPS) * ng_ref[...] * (g * jax.nn.sigmoid(g))
    st_ref[0] = st_scr[...]


def _hgrn(z3, s0t, lb_raw, norm_g, layer, c, chunks, valid=None):
    b, t, _ = z3.shape
    tt = c * chunks
    dstack, masks, head_mask, same_head = _hgrn_consts(c)
    const2 = lambda bi, ti: (0, 0)
    col = lambda cidx: pl.BlockSpec((1, tt, D_HGRN), lambda bi, ti: (bi, ti, cidx))
    return pl.pallas_call(
        functools.partial(_hgrn_kernel, c=c, chunks=chunks, layer=layer, valid=valid),
        grid=(b, t // tt),
        in_specs=[col(COL_QB), col(COL_FB), col(COL_IB), col(COL_GB),
                  pl.BlockSpec((DEPTH, D_HGRN), const2),
                  pl.BlockSpec((1, D_HGRN), const2),
                  pl.BlockSpec((1, D_HGRN, D_HGRN), lambda bi, ti: (bi, 0, 0)),
                  pl.BlockSpec(dstack.shape, const2),
                  pl.BlockSpec(masks.shape, lambda bi, ti: (0, 0, 0)),
                  pl.BlockSpec(head_mask.shape, const2),
                  pl.BlockSpec(same_head.shape, const2)],
        out_specs=(pl.BlockSpec((1, tt, D_HGRN), lambda bi, ti: (bi, ti, 0)),
                   pl.BlockSpec((1, D_HGRN, D_HGRN), lambda bi, ti: (bi, 0, 0))),
        out_shape=(jax.ShapeDtypeStruct((b, t, D_HGRN), F32),
                   jax.ShapeDtypeStruct((b, D_HGRN, D_HGRN), F32)),
        scratch_shapes=[pltpu.VMEM((D_HGRN, D_HGRN), F32)],
        compiler_params=_cparams("parallel", "arbitrary"),
        name="hgrn",
    )(z3, z3, z3, z3, lb_raw, jnp.tile(norm_g, N_HGRN).reshape(1, D_HGRN), s0t,
      jnp.asarray(dstack, BF16), jnp.asarray(masks), jnp.asarray(head_mask), jnp.asarray(same_head))


def _state_to_blockdiag_t(s):
    b = s.shape[0]
    eye = jnp.eye(N_HGRN, dtype=s.dtype)
    return jnp.einsum("bhkv,hg->bhvgk", s, eye).reshape(b, D_HGRN, D_HGRN)


def _blockdiag_t_to_state(st):
    b = st.shape[0]
    x = st.reshape(b, N_HGRN, HGRN_DV, N_HGRN, HGRN_DK)
    diag = jnp.stack([x[:, h, :, h, :] for h in range(N_HGRN)], axis=1)
    return jnp.swapaxes(diag, 2, 3)


def _strided_rows(start, dil):
    return pl.ds(start, BAND, stride=dil) if dil > 1 else pl.ds(start, BAND)


def _dil_attn_kernel(q_ref, kp_ref, kc_ref, vp_ref, vc_ref, slope_ref, o_ref, o_scr, l_scr):
    first_block = pl.program_id(1) == 0
    n2 = 2 * BAND
    row = lax.broadcasted_iota(jnp.int32, (n2, BAND), 0)
    col = lax.broadcasted_iota(jnp.int32, (n2, BAND), 1)
    tq = row & (BAND - 1)
    upper = col > tq
    diag = col == tq
    steps = (tq - col + jnp.where(upper, BAND, 0)).astype(F32)
    slope = slope_ref[0]
    slope_col = slope[:, 0:1]
    lane_first = lax.broadcasted_iota(jnp.int32, (BAND, PAIR), 1) < ATT_DH
    no_prev = jnp.where(first_block, NEG_BIG, 0.0)
    scale = ATT_DH ** -0.5

    for g, dil in enumerate(DILATIONS):
        bias = slope * steps * float(dil)
        diag_bias = slope_col * float(BAND * dil)
        span = BAND * dil
        for j in range(SUPER // span):
            for r in range(dil):
                cur = _strided_rows(j * span + r, dil)
                if j == 0:
                    prev = _strided_rows(SUPER - span + r, dil)
                    kp, vp = kp_ref[0, prev, :], vp_ref[0, prev, :]
                else:
                    prev = _strided_rows((j - 1) * span + r, dil)
                    kp, vp = kc_ref[0, prev, :], vc_ref[0, prev, :]
                q = q_ref[0, cur, :]
                kc, vc = kc_ref[0, cur, :], vc_ref[0, cur, :]
                q2 = jnp.concatenate([jnp.where(lane_first, q, 0.0), jnp.where(lane_first, 0.0, q)],
                                     axis=0).astype(BF16)
                sp = _dot_nt(q2, kp.astype(BF16))
                sc = _dot_nt(q2, kc.astype(BF16))
                s = jnp.where(upper, sp, sc) * scale - bias
                sd = jnp.sum(jnp.where(diag, sp, 0.0), axis=1, keepdims=True) * scale - diag_bias
                if j == 0:
                    s = s + jnp.where(upper, no_prev, 0.0)
                    sd = sd + no_prev
                m = jnp.maximum(jnp.max(s, axis=1, keepdims=True), sd)
                e = jnp.exp(s - m)
                ed = jnp.exp(sd - m)
                den = jnp.sum(e, axis=1, keepdims=True) + ed
                pv = (_dot(jnp.where(upper, e, 0.0).astype(BF16), vp.astype(BF16))
                      + _dot(jnp.where(upper, 0.0, e).astype(BF16), vc.astype(BF16))
                      + ed * jnp.concatenate([vp, vp], axis=0))
                o2 = pv / den
                lse = jnp.broadcast_to(m + jnp.log(den), (n2, PAIR))
                o_scr[g, cur, :] = jnp.where(lane_first, o2[0:BAND], o2[BAND:n2])
                l_scr[g, cur, :] = jnp.where(lane_first, lse[0:BAND], lse[BAND:n2])

    rows_per_iter = 2 * BAND

    def mix(i, carry):
        rows = pl.ds(pl.multiple_of(i * rows_per_iter, rows_per_iter), rows_per_iter)
        lses = [l_scr[g, rows, :] for g in range(len(DILATIONS))]
        m = functools.reduce(jnp.maximum, lses)
        ws = [jnp.exp(l - m) for l in lses]
        num = sum(wgt * o_scr[g, rows, :] for g, wgt in enumerate(ws))
        o_ref[0, rows, :] = num / sum(ws)
        return carry

    lax.fori_loop(0, SUPER // rows_per_iter, mix, 0)


def _dil_attn(z3):
    b, t, _ = z3.shape
    n_pairs = N_ATT // 2
    lanes_per_col = D_ATT // PAIR
    blk = lambda colblk, prev: pl.BlockSpec(
        (1, SUPER, PAIR),
        (lambda bi, sb, p: (bi, jnp.maximum(sb - 1, 0), colblk * lanes_per_col + p)) if prev
        else (lambda bi, sb, p: (bi, sb, colblk * lanes_per_col + p)))
    slopes = np.asarray(ALIBI_SLOPES, np.float32).reshape(n_pairs, 2, 1, 1)
    slope_tbl = np.broadcast_to(slopes, (n_pairs, 2, BAND, PAIR)).reshape(n_pairs, 2 * BAND, PAIR)
    return pl.pallas_call(
        _dil_attn_kernel,
        grid=(b, t // SUPER, n_pairs),
        in_specs=[blk(COL_QC, False), blk(COL_KC, True), blk(COL_KC, False),
                  blk(COL_VC, True), blk(COL_VC, False),
                  pl.BlockSpec((1, 2 * BAND, PAIR), lambda bi, sb, p: (p, 0, 0))],
        out_specs=pl.BlockSpec((1, SUPER, PAIR), lambda bi, sb, p: (bi, sb, p)),
        out_shape=jax.ShapeDtypeStruct((b, t, D_ATT), F32),
        scratch_shapes=[pltpu.VMEM((len(DILATIONS), SUPER, PAIR), F32),
                        pltpu.VMEM((len(DILATIONS), SUPER, PAIR), F32)],
        compiler_params=_cparams("parallel", "parallel", "parallel"),
        name="dil_attn",
    )(z3, z3, z3, z3, z3, jnp.asarray(slope_tbl))


def _cache_attn_kernel(q_ref, kn_ref, vn_ref, kc_ref, vc_ref, slope_ref, *rest, n_new):
    o_ref, ko_ref, vo_ref = rest[-3:]
    w = kc_ref.shape[-1]
    kn = kn_ref[0]
    vn = vn_ref[0]
    kt = kc_ref[0, 0]
    vt = vc_ref[0, 0]

    sel_row = lax.broadcasted_iota(jnp.int32, (SAMPLE_ROWS, LANE), 0)
    sel_lane = lax.broadcasted_iota(jnp.int32, (SAMPLE_ROWS, LANE), 1)
    sel = jnp.where((sel_lane == LANE - n_new + sel_row) & (sel_row < n_new), 1.0, 0.0).astype(BF16)
    tail_new = lax.broadcasted_iota(jnp.int32, (HALF_ATT, LANE), 1) >= LANE - n_new

    def shift_in(cache_t, new_rows, out_ref):
        rolled = pltpu.roll(cache_t, w - n_new, axis=1)
        out_ref[0, 0, :, 0:w - LANE] = rolled[:, 0:w - LANE]
        out_ref[0, 0, :, w - LANE:w] = jnp.where(tail_new, _dot_exact_tn(new_rows, sel), rolled[:, w - LANE:w])

    shift_in(kt, kn, ko_ref)
    shift_in(vt, vn, vo_ref)

    heads = N_ATT // 2
    rows = heads * SAMPLE_ROWS
    lane_head = lax.broadcasted_iota(jnp.int32, (rows, HALF_ATT), 1) >> _log2(ATT_DH)
    row_head = lax.broadcasted_iota(jnp.int32, (rows, HALF_ATT), 0) >> _log2(SAMPLE_ROWS)
    own = lane_head == row_head
    q8 = jnp.where(own, jnp.concatenate([q_ref[0]] * heads, axis=0), 0.0).astype(BF16)
    scale = ATT_DH ** -0.5
    slope = slope_ref[0][:, 0:1]

    def weights(dist):
        mult = jnp.zeros(dist.shape, F32)
        for dil in DILATIONS:
            hit = (dist >= 0) & (dist <= BAND * dil) & ((dist & (dil - 1)) == 0)
            mult = mult + jnp.where(hit, 1.0, 0.0)
        return mult

    def mask(s, dist):
        mult = weights(dist)
        return jnp.where(mult > 0.0, s * scale - slope * dist.astype(F32), NEG_BIG), mult

    t_c = lax.broadcasted_iota(jnp.int32, (rows, w), 0) & (SAMPLE_ROWS - 1)
    pos_c = lax.broadcasted_iota(jnp.int32, (rows, w), 1)
    s_c, mult_c = mask(_dot(q8, kt.astype(BF16)), w + t_c - pos_c)
    t_n = lax.broadcasted_iota(jnp.int32, (rows, SAMPLE_ROWS), 0) & (SAMPLE_ROWS - 1)
    pos_n = lax.broadcasted_iota(jnp.int32, (rows, SAMPLE_ROWS), 1)
    s_n, mult_n = mask(_dot_nt(q8, kn.astype(BF16)), t_n - pos_n)

    m = jnp.maximum(jnp.max(s_c, axis=1, keepdims=True), jnp.max(s_n, axis=1, keepdims=True))
    e_c = jnp.exp(s_c - m) * mult_c
    e_n = jnp.exp(s_n - m) * mult_n
    den = jnp.sum(e_c, axis=1, keepdims=True) + jnp.sum(e_n, axis=1, keepdims=True)
    o8 = (_dot_nt(e_c.astype(BF16), vt.astype(BF16)) + _dot(e_n.astype(BF16), vn.astype(BF16))) / den
    o8 = jnp.where(own, o8, 0.0)
    o = o8[0:SAMPLE_ROWS]
    for hd in range(1, heads):
        o = o + o8[hd * SAMPLE_ROWS:(hd + 1) * SAMPLE_ROWS]
    o_ref[0] = o


def _cache_attn(z3, cache_kt, cache_vt, layer, out_prev, n_new):
    depth, b, _, w = cache_kt.shape
    halves = D_ATT // HALF_ATT
    cols = D_ATT // HALF_ATT
    new = lambda colblk: pl.BlockSpec((1, SAMPLE_ROWS, HALF_ATT), lambda bi, hf: (bi, 0, colblk * cols + hf))
    cache = pl.BlockSpec((1, 1, HALF_ATT, w), lambda bi, hf: (layer, bi, hf, 0))
    heads = N_ATT // halves
    slope_tbl = np.broadcast_to(np.repeat(np.asarray(ALIBI_SLOPES, np.float32), SAMPLE_ROWS)
                                .reshape(halves, heads * SAMPLE_ROWS, 1), (halves, heads * SAMPLE_ROWS, LANE))
    in_specs = [new(COL_QC), new(COL_KC), new(COL_VC), cache, cache,
                pl.BlockSpec((1, heads * SAMPLE_ROWS, LANE), lambda bi, hf: (hf, 0, 0))]
    args = [z3, z3, z3, cache_kt, cache_vt, jnp.asarray(slope_tbl)]
    aliases = {}
    if out_prev is not None:
        in_specs += [pl.BlockSpec(memory_space=pl.ANY)] * 2
        aliases = {len(args): 1, len(args) + 1: 2}
        args += list(out_prev)
    stacked = jax.ShapeDtypeStruct((depth, b, D_ATT, w), F32)
    return pl.pallas_call(
        functools.partial(_cache_attn_kernel, n_new=n_new),
        grid=(b, halves),
        in_specs=in_specs,
        out_specs=(pl.BlockSpec((1, SAMPLE_ROWS, HALF_ATT), lambda bi, hf: (bi, 0, hf)), cache, cache),
        out_shape=(jax.ShapeDtypeStruct((b, SAMPLE_ROWS, D_ATT), F32), stacked, stacked),
        input_output_aliases=aliases,
        compiler_params=_cparams("parallel", "parallel"),
        name="cache_attn",
    )(*args)


def _wout_kernel(*refs, n_pat):
    x_ref, ya_ref, yb_ref = refs[0:3]
    n_lse = n_pat if n_pat > 1 else 0
    o_refs = refs[3:3 + n_pat]
    l_refs = refs[3 + n_pat:3 + n_pat + n_lse]
    gn_ref, w_ref, out_ref = refs[3 + n_pat + n_lse:]
    if n_pat == 1:
        yc = o_refs[0][...]
    else:
        lses = [l[...] for l in l_refs]
        m = functools.reduce(jnp.maximum, lses)
        ws = [jnp.exp(l - m) for l in lses]
        yc = sum(wgt * o[...] for wgt, o in zip(ws, o_refs)) / sum(ws)
    ycn = _rms(yc, gn_ref[...])
    acc = _dot(ya_ref[...].astype(BF16), w_ref[0:D_LRU, :])
    acc = acc + _dot(yb_ref[...].astype(BF16), w_ref[D_LRU:D_LRU + D_HGRN, :])
    acc = acc + _dot(ycn.astype(BF16), w_ref[D_LRU + D_HGRN:D_MODEL, :])
    out_ref[...] = x_ref[...] + acc


def _wout(x, ya, yb, outs, lses, gn_c, w_out, tm):
    m = x.shape[0]
    n_pat = len(outs)
    lses = list(lses) if n_pat > 1 else []
    spec = lambda n: pl.BlockSpec((tm, n), lambda i: (i, 0))
    return pl.pallas_call(
        functools.partial(_wout_kernel, n_pat=n_pat),
        grid=(m // tm,),
        in_specs=[spec(D_MODEL), spec(D_LRU), spec(D_HGRN)] + [spec(D_ATT)] * (n_pat + len(lses))
        + [pl.BlockSpec((1, D_ATT), lambda i: (0, 0)), pl.BlockSpec((D_MODEL, D_MODEL), lambda i: (0, 0))],
        out_specs=spec(D_MODEL),
        out_shape=jax.ShapeDtypeStruct((m, D_MODEL), F32),
        compiler_params=_cparams("parallel"),
        name="wout",
    )(x, ya, yb, *outs, *lses, gn_c.reshape(1, D_ATT), w_out)


def _xattn_kernel(x_ref, g_ref, wq_ref, mk_ref, mv_ref, wo_ref, o_ref):
    x = x_ref[0]
    q = _dot(_rms(x, g_ref[...]).astype(BF16), wq_ref[...])
    mk = mk_ref[0].astype(BF16)
    mv = mv_ref[0].astype(BF16)
    acc = x
    scale = XHEAD_DIM ** -0.5
    for hd in range(N_XHEADS):
        lanes = slice(hd * XHEAD_DIM, (hd + 1) * XHEAD_DIM)
        s = _dot_nt(q[:, lanes].astype(BF16), mk[:, lanes]) * scale
        e = jnp.exp(s - jnp.max(s, axis=1, keepdims=True))
        p = e / jnp.sum(e, axis=1, keepdims=True)
        oh = _dot(p.astype(BF16), mv[:, lanes])
        acc = acc + _dot(oh.astype(BF16), wo_ref[lanes, :])
    o_ref[0] = acc


def _xattn(x3, g, wq, mem, k_col, v_col, wo, tm):
    b, t, _ = x3.shape
    const = lambda bi, ti: (0, 0)
    mat = pl.BlockSpec((D_MODEL, D_MODEL), const)
    mem_k, mem_v = mem
    return pl.pallas_call(
        _xattn_kernel,
        grid=(b, t // tm),
        in_specs=[pl.BlockSpec((1, tm, D_MODEL), lambda bi, ti: (bi, ti, 0)),
                  pl.BlockSpec((1, D_MODEL), const), mat,
                  pl.BlockSpec((1, N_MEM, D_MODEL), lambda bi, ti: (bi, 0, k_col)),
                  pl.BlockSpec((1, N_MEM, D_MODEL), lambda bi, ti: (bi, 0, v_col)),
                  mat],
        out_specs=pl.BlockSpec((1, tm, D_MODEL), lambda bi, ti: (bi, ti, 0)),
        out_shape=jax.ShapeDtypeStruct((b, t, D_MODEL), F32),
        compiler_params=_cparams("parallel", "parallel"),
        name="xattn",
    )(x3, g.reshape(1, D_MODEL), wq, mem_k, mem_v, wo)


def _mem_attn_kernel(q_ref, mk_ref, mv_ref, o_ref):
    q = q_ref[0]
    mk = mk_ref[0].astype(BF16)
    mv = mv_ref[0].astype(BF16)
    scale = XHEAD_DIM ** -0.5
    for hd in range(N_XHEADS):
        lanes = slice(hd * XHEAD_DIM, (hd + 1) * XHEAD_DIM)
        s = _dot_nt(q[:, lanes].astype(BF16), mk[:, lanes]) * scale
        e = jnp.exp(s - jnp.max(s, axis=1, keepdims=True))
        p = e / jnp.sum(e, axis=1, keepdims=True)
        o_ref[0, :, lanes] = _dot(p.astype(BF16), mv[:, lanes])


def _mem_attn(q3, mem_k, mem_v):
    b, t, _ = q3.shape
    row = pl.BlockSpec((1, t, D_MODEL), lambda bi: (bi, 0, 0))
    mem = pl.BlockSpec((1, N_MEM, D_MODEL), lambda bi: (bi, 0, 0))
    return pl.pallas_call(
        _mem_attn_kernel,
        grid=(b,),
        in_specs=[row, mem, mem],
        out_specs=row,
        out_shape=jax.ShapeDtypeStruct((b, t, D_MODEL), F32),
        compiler_params=_cparams("parallel"),
        name="mem_attn",
    )(q3, mem_k, mem_v)


def _proj_res_kernel(x_ref, a_ref, w_ref, o_ref):
    o_ref[...] = x_ref[...] + _dot(a_ref[...].astype(BF16), w_ref[...])


def _proj_res(x, a, w, tm):
    m, k = a.shape
    n = w.shape[1]
    return pl.pallas_call(
        _proj_res_kernel,
        grid=(m // tm,),
        in_specs=[pl.BlockSpec((tm, n), lambda i: (i, 0)),
                  pl.BlockSpec((tm, k), lambda i: (i, 0)),
                  pl.BlockSpec((k, n), lambda i: (0, 0))],
        out_specs=pl.BlockSpec((tm, n), lambda i: (i, 0)),
        out_shape=jax.ShapeDtypeStruct((m, n), F32),
        compiler_params=_cparams("parallel"),
        name="proj_res",
    )(x, a, w)


def _block_diag(w):
    n, c, d = w.shape
    eye = jnp.eye(n, dtype=w.dtype)
    return jnp.einsum("ncd,nm->ncmd", w, eye).reshape(n * c, n * d)


def kernel(x_prompt, x_sample, state_lru_h, state_lru_conv, state_hgrn, cache_swa_k, cache_swa_v, cache_mem_k, cache_mem_v, mem_prompt, n_ffn1, ffn1_wg, ffn1_wu, ffn1_wd, n_mix, w_in, lru_conv_w, lru_conv_b, lru_wa, lru_ba, lru_wx, lru_bx, lru_lambda, hgrn_lb, hgrn_norm, gn_a, gn_c, w_out, n_cross, x_wq, x_wk, x_wv, x_wo, n_ffn2, ffn2_wg, ffn2_wu, ffn2_wd, n_final):
    bp, tp, _ = x_prompt.shape
    bs, ts_new, _ = x_sample.shape
    w_buf = cache_swa_k.shape[2]
    assert ts_new <= SAMPLE_ROWS and tp % (BAND * max(DILATIONS)) == 0 and tp >= MAX_WINDOW
    tm_p = 512
    tm_s = bs * SAMPLE_ROWS

    xp = x_prompt.reshape(bp * tp, D_MODEL)
    xs = jnp.pad(x_sample, ((0, 0), (0, SAMPLE_ROWS - ts_new), (0, 0))).reshape(tm_s, D_MODEL)
    mem2 = mem_prompt.reshape(bp * N_MEM, D_MODEL)
    ones = jnp.ones((D_MODEL,), F32)

    outs = {name: [] for name in ("p_h", "p_c", "p_s", "p_k", "p_v", "p_mk", "p_mv", "s_h", "s_c", "s_s")}
    to_feature_major = lambda c: jnp.transpose(c, (0, 1, 3, 4, 2)).reshape(DEPTH, bs, D_ATT, w_buf)
    cache_kt, cache_vt = to_feature_major(cache_swa_k), to_feature_major(cache_swa_v)
    swa_out = None
    yp = ys = None
    for l in range(DEPTH):
        bf = lambda w: w[l].astype(BF16)
        wg1, wu1, wd1 = bf(ffn1_wg), bf(ffn1_wu), bf(ffn1_wd)
        wg2, wu2, wd2 = bf(ffn2_wg), bf(ffn2_wu), bf(ffn2_wd)
        win, wo_mix, wq, wo_x = bf(w_in), bf(w_out), bf(x_wq), bf(x_wo)
        wkv = jnp.concatenate([x_wk[l], x_wv[l]], axis=1).astype(BF16)
        lp = dict(conv_w=lru_conv_w[l], conv_b=lru_conv_b[l].reshape(1, D_LRU),
                  wa=_block_diag(lru_wa[l]).astype(BF16), ba=lru_ba[l].reshape(1, D_LRU),
                  wx=_block_diag(lru_wx[l]).astype(BF16), bx=lru_bx[l].reshape(1, D_LRU),
                  lam=lru_lambda[l].reshape(1, D_LRU), gn_a=gn_a[l].reshape(1, D_LRU))
        last = l == DEPTH - 1

        mkv = _proj(mem2, ones, wkv, tm=bp * N_MEM, norm=False).reshape(bp, N_MEM, 2 * D_MODEL)
        xp = _ffn(xp, n_ffn1[l], wg1, wu1, wd1, tm_p)
        z3 = _proj(xp, n_mix[l], win, tm_p).reshape(bp, tp, D_IN)
        ya, hl = _lru(z3, jnp.zeros((bp, SUBLANE, D_LRU), F32), jnp.zeros((bp, 1, D_LRU), F32), lp, tt=256)
        yb, st = _hgrn(z3, jnp.zeros((bp, D_HGRN, D_HGRN), F32), hgrn_lb, hgrn_norm[l], l,
                       c=HGRN_CHUNK, chunks=4)
        yc = _dil_attn(z3)
        xp = _wout(xp, ya.reshape(bp * tp, D_LRU), yb.reshape(bp * tp, D_HGRN),
                   [yc.reshape(bp * tp, D_ATT)], [], gn_c[l], wo_mix, tm_p)
        xp = _xattn(xp.reshape(bp, tp, D_MODEL), n_cross[l], wq, (mkv, mkv), 0, 1, wo_x, tm_p)
        xp = xp.reshape(bp * tp, D_MODEL)
        if last:
            xp, yp = _ffn(xp, n_ffn2[l], wg2, wu2, wd2, tm_p, g_final=n_final)
        else:
            xp = _ffn(xp, n_ffn2[l], wg2, wu2, wd2, tm_p)
        outs["p_h"].append(hl[:, SUBLANE - 1])
        outs["p_c"].append(z3[:, tp - (CONV_W - 1):, 0:D_LRU])
        outs["p_s"].append(_blockdiag_t_to_state(st))
        keep = min(MAX_WINDOW, tp)
        kcol = COL_KC * D_ATT
        vcol = COL_VC * D_ATT
        outs["p_k"].append(z3[:, tp - keep:, kcol:kcol + D_ATT].reshape(bp, keep, N_ATT, ATT_DH))
        outs["p_v"].append(z3[:, tp - keep:, vcol:vcol + D_ATT].reshape(bp, keep, N_ATT, ATT_DH))
        outs["p_mk"].append(mkv[:, :, 0:D_MODEL].reshape(bp, N_MEM, N_XHEADS, XHEAD_DIM))
        outs["p_mv"].append(mkv[:, :, D_MODEL:].reshape(bp, N_MEM, N_XHEADS, XHEAD_DIM))

        xs = _ffn(xs, n_ffn1[l], wg1, wu1, wd1, tm_s)
        zs3 = _proj(xs, n_mix[l], win, tm_s).reshape(bs, SAMPLE_ROWS, D_IN)
        conv_buf = jnp.pad(state_lru_conv[l], ((0, 0), (SUBLANE - (CONV_W - 1), 0), (0, 0)))
        ya, hl = _lru(zs3, conv_buf, state_lru_h[l].reshape(bs, 1, D_LRU), lp, tt=SAMPLE_ROWS, valid=ts_new)
        yb, st = _hgrn(zs3, _state_to_blockdiag_t(state_hgrn[l]), hgrn_lb, hgrn_norm[l], l,
                       c=SAMPLE_ROWS, chunks=1, valid=ts_new)
        yc, ko, vo = _cache_attn(zs3, cache_kt, cache_vt, l, swa_out, ts_new)
        swa_out = (ko, vo)
        xs = _wout(xs, ya.reshape(tm_s, D_LRU), yb.reshape(tm_s, D_HGRN), [yc.reshape(tm_s, D_ATT)], [],
                   gn_c[l], wo_mix, tm_s)
        mem_k = cache_mem_k[l].reshape(bs, N_MEM, D_MODEL)
        mem_v = cache_mem_v[l].reshape(bs, N_MEM, D_MODEL)
        qs = _proj(xs, n_cross[l], wq, tm_s).reshape(bs, SAMPLE_ROWS, D_MODEL)
        xs = _proj_res(xs, _mem_attn(qs, mem_k, mem_v).reshape(tm_s, D_MODEL), wo_x, tm_s)
        if last:
            xs, ys = _ffn(xs, n_ffn2[l], wg2, wu2, wd2, tm_s, g_final=n_final)
        else:
            xs = _ffn(xs, n_ffn2[l], wg2, wu2, wd2, tm_s)
        outs["s_h"].append(hl[:, SUBLANE - 1])
        outs["s_c"].append(zs3[:, ts_new - (CONV_W - 1):ts_new, 0:D_LRU])
        outs["s_s"].append(_blockdiag_t_to_state(st))

    y_prompt = yp.reshape(bp, tp, D_MODEL)
    y_sample = ys.reshape(bs, SAMPLE_ROWS, D_MODEL)[:, :ts_new]
    stack = lambda name: jnp.stack(outs[name])
    from_feature_major = lambda c: jnp.transpose(c.reshape(DEPTH, bs, N_ATT, ATT_DH, w_buf), (0, 1, 4, 2, 3))
    return (y_prompt, y_sample,
            stack("p_h"), stack("p_c"), stack("p_s"), stack("p_k"), stack("p_v"), stack("p_mk"), stack("p_mv"),
            stack("s_h"), stack("s_c"), stack("s_s"), from_feature_major(swa_out[0]), from_feature_major(swa_out[1]))
```

```python
import functools

import numpy as np
import jax
import jax.numpy as jnp
from jax import lax
from jax.experimental import pallas as pl
from jax.experimental.pallas import tpu as pltpu

F32 = jnp.float32
BF16 = jnp.bfloat16

D_MODEL = 1024
DEPTH = 4
D_LRU = 256
N_LRU_BLOCKS = 4
CONV_W = 4
LRU_C = 8.0
HGRN_DK = 64
HGRN_DV = 64
D_HGRN = 256
N_HGRN = 4
HGRN_CHUNK = 64
ATT_DH = 64
D_ATT = 512
N_ATT = 8
DILATIONS = (1, 4, 16)
BAND = 128
SUPER = BAND * max(DILATIONS)
PAIR = 2 * ATT_DH
MAX_WINDOW = 2048
N_MEM = 256
N_XHEADS = 4
XHEAD_DIM = 256
D_FF = 2816
D_IN = 3072
EPS = 1e-6

COL_XA, COL_GA, COL_QB, COL_FB, COL_IB, COL_GB = 0, 1, 2, 3, 4, 5
COL_QC, COL_KC, COL_VC = 3, 4, 5
ATT_COLS_PER_ROW = D_IN // D_ATT

HALF_ATT = D_ATT // 2
SUBLANE = 8
LANE = 128
SAMPLE_ROWS = 8
VMEM_LIMIT = 56 * 1024 * 1024
NEG_BIG = -1e30

ALIBI_SLOPES = tuple(float(2.0 ** (-8.0 * (h + 1) / N_ATT)) for h in range(N_ATT))


def _log2(n):
    assert n & (n - 1) == 0
    return n.bit_length() - 1


def _cparams(*sem):
    return pltpu.CompilerParams(dimension_semantics=sem, vmem_limit_bytes=VMEM_LIMIT)


def _rms(x, g):
    ms = jnp.mean(x * x, axis=-1, keepdims=True)
    return x * lax.rsqrt(ms + EPS) * g


def _expm1(x):
    e = jnp.exp(x)
    near = (e - 1.0) * x / jnp.log(e)
    return jnp.where(e == 1.0, x, jnp.where(jnp.abs(x) > 1.0, e - 1.0, near))


def _dot(a, b):
    return jnp.dot(a, b, preferred_element_type=F32)


def _dot_nt(a, b):
    return lax.dot_general(a, b, (((1,), (1,)), ((), ())), preferred_element_type=F32)


def _dot_tn(a, b):
    return lax.dot_general(a, b, (((0,), (0,)), ((), ())), preferred_element_type=F32)


def _split3(x):
    hi = x.astype(BF16)
    r1 = x - hi.astype(F32)
    mid = r1.astype(BF16)
    lo = (r1 - mid.astype(F32)).astype(BF16)
    return hi, mid, lo


def _dot_exact_lhs(a_bf16, x):
    return sum(_dot(a_bf16, p) for p in _split3(x))


def _dot_exact_tn(x, a_bf16):
    return sum(_dot_tn(p, a_bf16) for p in _split3(x))


def _ffn_kernel(*refs, final_norm):
    if final_norm:
        x_ref, g_ref, wg_ref, wu_ref, wd_ref, gf_ref, o_ref, y_ref = refs
    else:
        x_ref, g_ref, wg_ref, wu_ref, wd_ref, o_ref = refs
    x = x_ref[...]
    h = _rms(x, g_ref[...]).astype(BF16)
    gate = _dot(h, wg_ref[...])
    up = _dot(h, wu_ref[...])
    act = (gate * jax.nn.sigmoid(gate) * up).astype(BF16)
    out = x + 0.5 * _dot(act, wd_ref[...])
    o_ref[...] = out
    if final_norm:
        y_ref[...] = _rms(out, gf_ref[...])


def _resident(shape):
    return pl.BlockSpec(shape, lambda *_: (0,) * len(shape), pipeline_mode=pl.Buffered(1))


def _ffn(x, g, wg, wu, wd, tm, g_final=None):
    m = x.shape[0]
    final_norm = g_final is not None
    row = pl.BlockSpec((tm, D_MODEL), lambda i: (i, 0))
    vec = _resident((1, D_MODEL))
    in_specs = [row, vec, _resident((D_MODEL, D_FF)), _resident((D_MODEL, D_FF)), _resident((D_FF, D_MODEL))]
    args = [x, g.reshape(1, D_MODEL), wg, wu, wd]
    out_shape = jax.ShapeDtypeStruct((m, D_MODEL), F32)
    out_specs = row
    if final_norm:
        in_specs.append(vec)
        args.append(g_final.reshape(1, D_MODEL))
        out_shape = (out_shape, out_shape)
        out_specs = (row, row)
    return pl.pallas_call(
        functools.partial(_ffn_kernel, final_norm=final_norm),
        grid=(m // tm,),
        in_specs=in_specs, out_specs=out_specs, out_shape=out_shape,
        compiler_params=_cparams("parallel"),
        name="ffn",
    )(*args)


def _proj_kernel(x_ref, g_ref, w_ref, o_ref, *, norm):
    x = x_ref[...]
    if norm:
        x = _rms(x, g_ref[...])
    o_ref[...] = _dot(x.astype(BF16), w_ref[...])


def _proj(x, g, w, tm, norm=True):
    m, k = x.shape
    n = w.shape[1]
    return pl.pallas_call(
        functools.partial(_proj_kernel, norm=norm),
        grid=(m // tm,),
        in_specs=[pl.BlockSpec((tm, k), lambda i: (i, 0)),
                  pl.BlockSpec((1, k), lambda i: (0, 0)),
                  _resident((k, n))],
        out_specs=pl.BlockSpec((tm, n), lambda i: (i, 0)),
        out_shape=jax.ShapeDtypeStruct((m, n), F32),
        compiler_params=_cparams("parallel"),
        name="proj",
    )(x, g.reshape(1, k), w)


def _lru_kernel(xa_ref, ga_ref, cb_ref, h0_ref, cw_ref, cbias_ref, wa_ref, ba_ref, wx_ref, bx_ref,
                lam_ref, gn_ref, ya_ref, hl_ref, xbuf, abuf, ubuf, hc, *, tt, pad, valid):
    ti = pl.program_id(1)

    @pl.when(ti == 0)
    def _():
        xbuf[0:SUBLANE, :] = cb_ref[0]
        hc[...] = jnp.broadcast_to(h0_ref[0], (SUBLANE, D_LRU))
        for s in range(2):
            abuf[s, 0:pad, :] = jnp.ones((pad, D_LRU), F32)
            ubuf[s, 0:pad, :] = jnp.zeros((pad, D_LRU), F32)

    x = xa_ref[0]
    xbuf[SUBLANE:SUBLANE + tt, :] = x
    y = cbias_ref[...] + cw_ref[CONV_W - 1:CONV_W, :] * x
    for tap in range(CONV_W - 1):
        back = CONV_W - 1 - tap
        y = y + cw_ref[tap:tap + 1, :] * xbuf[SUBLANE - back:SUBLANE - back + tt, :]
    xbuf[0:SUBLANE, :] = x[tt - SUBLANE:tt, :]

    yb = y.astype(BF16)
    r = jax.nn.sigmoid(_dot(yb, wa_ref[...]) + ba_ref[...])
    i = jax.nn.sigmoid(_dot(yb, wx_ref[...]) + bx_ref[...])
    lam = lam_ref[...]
    softplus_neg_lam = jnp.maximum(-lam, 0.0) + jnp.log1p(jnp.exp(-jnp.abs(lam)))
    log_a = -LRU_C * r * softplus_neg_lam
    a = jnp.exp(log_a)
    u = jnp.sqrt(-_expm1(2.0 * log_a)) * (i * y)
    if valid is not None:
        keep = lax.broadcasted_iota(jnp.int32, (tt, D_LRU), 0) < valid
        a = jnp.where(keep, a, 1.0)
        u = jnp.where(keep, u, 0.0)

    abuf[0, pad:pad + tt, :] = a
    ubuf[0, pad:pad + tt, :] = u
    cur, k = 0, 1
    while k < tt:
        a_c = abuf[cur, pad:pad + tt, :]
        u_c = ubuf[cur, pad:pad + tt, :]
        a_s = abuf[cur, pad - k:pad - k + tt, :]
        u_s = ubuf[cur, pad - k:pad - k + tt, :]
        abuf[1 - cur, pad:pad + tt, :] = a_c * a_s
        ubuf[1 - cur, pad:pad + tt, :] = a_c * u_s + u_c
        cur, k = 1 - cur, 2 * k
    h = abuf[cur, pad:pad + tt, :] * hc[SUBLANE - 1:SUBLANE, :] + ubuf[cur, pad:pad + tt, :]
    hc[...] = h[tt - SUBLANE:tt, :]
    hl_ref[0] = h[tt - SUBLANE:tt, :]
    ya_ref[0] = _rms(h * jax.nn.gelu(ga_ref[0]), gn_ref[...])


def _lru(z3, conv_buf, h0, lp, tt, valid=None):
    b, t, _ = z3.shape
    pad = max(tt // 2, SUBLANE)
    vec = pl.BlockSpec((1, D_LRU), lambda bi, ti: (0, 0))
    mat = pl.BlockSpec((D_LRU, D_LRU), lambda bi, ti: (0, 0))
    return pl.pallas_call(
        functools.partial(_lru_kernel, tt=tt, pad=pad, valid=valid),
        grid=(b, t // tt),
        in_specs=[pl.BlockSpec((1, tt, D_LRU), lambda bi, ti: (bi, ti, COL_XA)),
                  pl.BlockSpec((1, tt, D_LRU), lambda bi, ti: (bi, ti, COL_GA)),
                  pl.BlockSpec((1, SUBLANE, D_LRU), lambda bi, ti: (bi, 0, 0)),
                  pl.BlockSpec((1, 1, D_LRU), lambda bi, ti: (bi, 0, 0)),
                  pl.BlockSpec((CONV_W, D_LRU), lambda bi, ti: (0, 0)),
                  vec, mat, vec, mat, vec, vec, vec],
        out_specs=(pl.BlockSpec((1, tt, D_LRU), lambda bi, ti: (bi, ti, 0)),
                   pl.BlockSpec((1, SUBLANE, D_LRU), lambda bi, ti: (bi, 0, 0))),
        out_shape=(jax.ShapeDtypeStruct((b, t, D_LRU), F32),
                   jax.ShapeDtypeStruct((b, SUBLANE, D_LRU), F32)),
        scratch_shapes=[pltpu.VMEM((SUBLANE + tt, D_LRU), F32),
                        pltpu.VMEM((2, pad + tt, D_LRU), F32),
                        pltpu.VMEM((2, pad + tt, D_LRU), F32),
                        pltpu.VMEM((SUBLANE, D_LRU), F32)],
        compiler_params=_cparams("parallel", "arbitrary"),
        name="lru",
    )(z3, z3, conv_buf, h0, lp["conv_w"], lp["conv_b"], lp["wa"], lp["ba"], lp["wx"], lp["bx"],
      lp["lam"], lp["gn_a"])


@functools.lru_cache(maxsize=None)
def _hgrn_consts(c):
    t = np.arange(c)[:, None]
    j = np.arange(c)[None, :]
    blocks = [j <= t, j > t]
    masks = []
    w = c // 2
    while w >= 1:
        p = t % (2 * w)
        ref = t - p + w - 1
        blocks.append(np.where(p >= w, (j > ref) & (j <= t), (j > t) & (j <= ref)))
        masks.append((t // (2 * w) == j // (2 * w)) & (t % (2 * w) >= w) & (j % (2 * w) < w))
        w //= 2
    masks.append(t == j)
    dstack = np.concatenate(blocks, axis=0).astype(np.float32)
    masks = np.stack([np.tile(m, (1, N_HGRN)) for m in masks]).astype(np.float32)
    lane_head = np.arange(D_HGRN)[None, :] // HGRN_DK
    head_rows = np.repeat(np.arange(N_HGRN), c)[:, None]
    head_mask = (head_rows == lane_head).astype(np.float32)
    same_head = (np.arange(D_HGRN)[:, None] // HGRN_DV == lane_head).astype(np.float32)
    return dstack, masks, head_mask, same_head


def _hgrn_kernel(q_ref, f_ref, v_ref, g_ref, lbraw_ref, ng_ref, s0_ref, dst_ref, msk_ref, hm_ref, sh_ref,
                 y_ref, st_ref, st_scr, *, c, chunks, layer, valid):
    ti = pl.program_id(1)
    n_seq = q_ref.shape[0]

    @pl.when(ti == 0)
    def _():
        st_scr[...] = s0_ref[...]

    lbraw = lbraw_ref[...]
    e = jnp.exp(lbraw - jnp.max(lbraw, axis=0, keepdims=True))
    sm = e / jnp.sum(e, axis=0, keepdims=True)
    lrow = lax.broadcasted_iota(jnp.int32, sm.shape, 0)
    lb = jnp.sum(jnp.where((lrow >= 1) & (lrow <= layer), sm, 0.0), axis=0, keepdims=True)

    n_levels = msk_ref.shape[0] - 1
    hm = hm_ref[...]
    hm_b = hm.astype(BF16)
    same_head = sh_ref[...]
    shb = same_head.astype(BF16)

    def per_head_rows(x):
        if c % 16 == 0:
            return jnp.concatenate([x.astype(BF16)] * N_HGRN, axis=0) * hm_b
        return (jnp.concatenate([x] * N_HGRN, axis=0) * hm).astype(BF16)

    for ci in range(chunks):
        rows = slice(ci * c, (ci + 1) * c)
        for si in range(n_seq):
            q = q_ref[si, rows, :]
            v = v_ref[si, rows, :]
            g = g_ref[si, rows, :]
            f = lb + (1.0 - lb) * jax.nn.sigmoid(f_ref[si, rows, :])
            log_f = jnp.log(f)
            k = 1.0 - f
            if valid is not None:
                keep = lax.broadcasted_iota(jnp.int32, (c, D_HGRN), 0) < valid
                log_f = jnp.where(keep, log_f, 0.0)
                k = jnp.where(keep, k, 0.0)
            ex = _dot_exact_lhs(dst_ref[...], log_f)
            cum = ex[0:c]
            rem = ex[c:2 * c]

            sc = jnp.zeros((c, N_HGRN * c), F32)
            for li in range(n_levels + 1):
                if li < n_levels:
                    decay = jnp.exp(ex[(2 + li) * c:(3 + li) * c])
                    qt, kt = q * decay, k * decay
                else:
                    qt, kt = q, k
                sc = sc + _dot_nt(qt.astype(BF16), per_head_rows(kt)) * msk_ref[li]
            o = _dot(sc.astype(BF16), per_head_rows(v))

            st = st_scr[si]
            o = o + _dot_nt((q * jnp.exp(cum)).astype(BF16), st.astype(BF16))
            upd = _dot_tn(v.astype(BF16), (k * jnp.exp(rem)).astype(BF16))
            st_scr[si] = st * jnp.exp(cum[c - 1:c, :]) + upd * same_head

            o2 = o * o
            o2_hi = o2.astype(BF16)
            o2_lo = (o2 - o2_hi.astype(F32)).astype(BF16)
            ms = (_dot(o2_hi, shb) + _dot(o2_lo, shb)) * (1.0 / HGRN_DV)
            y_ref[si, rows, :] = o * lax.rsqrt(ms + EPS) * ng_ref[...] * (g * jax.nn.sigmoid(g))
    st_ref[...] = st_scr[...]


def _hgrn(z3, s0t, lb_raw, norm_g, layer, c, chunks, n_seq, valid=None):
    b, t, _ = z3.shape
    tt = c * chunks
    dstack, masks, head_mask, same_head = _hgrn_consts(c)
    const2 = lambda bi, ti: (0, 0)
    col = lambda cidx: pl.BlockSpec((n_seq, tt, D_HGRN), lambda bi, ti: (bi, ti, cidx))
    state = pl.BlockSpec((n_seq, D_HGRN, D_HGRN), lambda bi, ti: (bi, 0, 0))
    return pl.pallas_call(
        functools.partial(_hgrn_kernel, c=c, chunks=chunks, layer=layer, valid=valid),
        grid=(b // n_seq, t // tt),
        in_specs=[col(COL_QB), col(COL_FB), col(COL_IB), col(COL_GB),
                  pl.BlockSpec((DEPTH, D_HGRN), const2),
                  pl.BlockSpec((1, D_HGRN), const2),
                  state,
                  pl.BlockSpec(dstack.shape, const2),
                  pl.BlockSpec(masks.shape, lambda bi, ti: (0, 0, 0)),
                  pl.BlockSpec(head_mask.shape, const2),
                  pl.BlockSpec(same_head.shape, const2)],
        out_specs=(pl.BlockSpec((n_seq, tt, D_HGRN), lambda bi, ti: (bi, ti, 0)), state),
        out_shape=(jax.ShapeDtypeStruct((b, t, D_HGRN), F32),
                   jax.ShapeDtypeStruct((b, D_HGRN, D_HGRN), F32)),
        scratch_shapes=[pltpu.VMEM((n_seq, D_HGRN, D_HGRN), F32)],
        compiler_params=_cparams("parallel", "arbitrary"),
        name="hgrn",
    )(z3, z3, z3, z3, lb_raw, jnp.tile(norm_g, N_HGRN).reshape(1, D_HGRN), s0t,
      jnp.asarray(dstack, BF16), jnp.asarray(masks), jnp.asarray(head_mask), jnp.asarray(same_head))


def _state_to_blockdiag_t(s):
    b = s.shape[0]
    eye = jnp.eye(N_HGRN, dtype=s.dtype)
    return jnp.einsum("bhkv,hg->bhvgk", s, eye).reshape(b, D_HGRN, D_HGRN)


def _blockdiag_t_to_state(st):
    b = st.shape[0]
    x = st.reshape(b, N_HGRN, HGRN_DV, N_HGRN, HGRN_DK)
    diag = jnp.stack([x[:, h, :, h, :] for h in range(N_HGRN)], axis=1)
    return jnp.swapaxes(diag, 2, 3)


def _strided_rows(start, dil):
    return pl.ds(start, BAND, stride=dil) if dil > 1 else pl.ds(start, BAND)


def _dil_attn_kernel(q_ref, kp_ref, kc_ref, vp_ref, vc_ref, slope_ref, o_ref, o_scr, l_scr):
    first_block = pl.program_id(1) == 0
    n2 = 2 * BAND
    row = lax.broadcasted_iota(jnp.int32, (n2, BAND), 0)
    col = lax.broadcasted_iota(jnp.int32, (n2, BAND), 1)
    tq = row & (BAND - 1)
    upper = col > tq
    diag = col == tq
    steps = (tq - col + jnp.where(upper, BAND, 0)).astype(F32)
    slope = slope_ref[0]
    slope_col = slope[:, 0:1]
    lane_first = lax.broadcasted_iota(jnp.int32, (BAND, PAIR), 1) < ATT_DH
    no_prev = jnp.where(first_block, NEG_BIG, 0.0)
    scale = ATT_DH ** -0.5

    for g, dil in enumerate(DILATIONS):
        bias = slope * steps * float(dil)
        diag_bias = slope_col * float(BAND * dil)
        span = BAND * dil
        for j in range(SUPER // span):
            for r in range(dil):
                cur = _strided_rows(j * span + r, dil)
                if j == 0:
                    prev = _strided_rows(SUPER - span + r, dil)
                    kp, vp = kp_ref[0, prev, :], vp_ref[0, prev, :]
                else:
                    prev = _strided_rows((j - 1) * span + r, dil)
                    kp, vp = kc_ref[0, prev, :], vc_ref[0, prev, :]
                q = q_ref[0, cur, :] * scale
                kc, vc = kc_ref[0, cur, :], vc_ref[0, cur, :]
                q2 = jnp.concatenate([jnp.where(lane_first, q, 0.0), jnp.where(lane_first, 0.0, q)],
                                     axis=0).astype(BF16)
                sp = _dot_nt(q2, kp.astype(BF16))
                sc = _dot_nt(q2, kc.astype(BF16))
                s = jnp.where(upper, sp, sc) - bias
                sd = jnp.sum(jnp.where(diag, sp, 0.0), axis=1, keepdims=True) - diag_bias
                if j == 0:
                    s = s + jnp.where(upper, no_prev, 0.0)
                    sd = sd + no_prev
                m = jnp.maximum(jnp.max(s, axis=1, keepdims=True), sd)
                e = jnp.exp(s - m)
                ed = jnp.exp(sd - m)
                den = jnp.sum(e, axis=1, keepdims=True) + ed
                pv = (_dot(jnp.where(upper, e, 0.0).astype(BF16), vp.astype(BF16))
                      + _dot(jnp.where(upper, 0.0, e).astype(BF16), vc.astype(BF16))
                      + ed * jnp.concatenate([vp, vp], axis=0))
                o2 = pv / den
                lse = jnp.broadcast_to(m + jnp.log(den), (n2, PAIR))
                o_scr[g, cur, :] = jnp.where(lane_first, o2[0:BAND], o2[BAND:n2])
                l_scr[g, cur, :] = jnp.where(lane_first, lse[0:BAND], lse[BAND:n2])

    rows_per_iter = 2 * BAND

    def mix(i, carry):
        rows = pl.ds(pl.multiple_of(i * rows_per_iter, rows_per_iter), rows_per_iter)
        lses = [l_scr[g, rows, :] for g in range(len(DILATIONS))]
        m = functools.reduce(jnp.maximum, lses)
        ws = [jnp.exp(l - m) for l in lses]
        num = sum(wgt * o_scr[g, rows, :] for g, wgt in enumerate(ws))
        o_ref[0, rows, :] = num / sum(ws)
        return carry

    lax.fori_loop(0, SUPER // rows_per_iter, mix, 0)


def _dil_attn(z3):
    b, t, _ = z3.shape
    n_pairs = N_ATT // 2
    lanes_per_col = D_ATT // PAIR
    blk = lambda colblk, prev: pl.BlockSpec(
        (1, SUPER, PAIR),
        (lambda bi, sb, p: (bi, jnp.maximum(sb - 1, 0), colblk * lanes_per_col + p)) if prev
        else (lambda bi, sb, p: (bi, sb, colblk * lanes_per_col + p)))
    slopes = np.asarray(ALIBI_SLOPES, np.float32).reshape(n_pairs, 2, 1, 1)
    slope_tbl = np.broadcast_to(slopes, (n_pairs, 2, BAND, PAIR)).reshape(n_pairs, 2 * BAND, PAIR)
    return pl.pallas_call(
        _dil_attn_kernel,
        grid=(b, t // SUPER, n_pairs),
        in_specs=[blk(COL_QC, False), blk(COL_KC, True), blk(COL_KC, False),
                  blk(COL_VC, True), blk(COL_VC, False),
                  pl.BlockSpec((1, 2 * BAND, PAIR), lambda bi, sb, p: (p, 0, 0))],
        out_specs=pl.BlockSpec((1, SUPER, PAIR), lambda bi, sb, p: (bi, sb, p)),
        out_shape=jax.ShapeDtypeStruct((b, t, D_ATT), F32),
        scratch_shapes=[pltpu.VMEM((len(DILATIONS), SUPER, PAIR), F32),
                        pltpu.VMEM((len(DILATIONS), SUPER, PAIR), F32)],
        compiler_params=_cparams("parallel", "parallel", "parallel"),
        name="dil_attn",
    )(z3, z3, z3, z3, z3, jnp.asarray(slope_tbl))


def _cache_attn_kernel(q_ref, kn_ref, vn_ref, kc_ref, vc_ref, slope_ref, *rest, n_new):
    o_ref, ko_ref, vo_ref = rest[-3:]
    w = kc_ref.shape[-1]
    kn = kn_ref[0]
    vn = vn_ref[0]
    kt = kc_ref[0, 0]
    vt = vc_ref[0, 0]

    sel_row = lax.broadcasted_iota(jnp.int32, (SAMPLE_ROWS, LANE), 0)
    sel_lane = lax.broadcasted_iota(jnp.int32, (SAMPLE_ROWS, LANE), 1)
    sel = jnp.where((sel_lane == LANE - n_new + sel_row) & (sel_row < n_new), 1.0, 0.0).astype(BF16)
    tail_new = lax.broadcasted_iota(jnp.int32, (HALF_ATT, LANE), 1) >= LANE - n_new

    def shift_in(cache_t, new_rows, out_ref):
        rolled = pltpu.roll(cache_t, w - n_new, axis=1)
        out_ref[0, 0, :, 0:w - LANE] = rolled[:, 0:w - LANE]
        out_ref[0, 0, :, w - LANE:w] = jnp.where(tail_new, _dot_exact_tn(new_rows, sel), rolled[:, w - LANE:w])

    shift_in(kt, kn, ko_ref)
    shift_in(vt, vn, vo_ref)

    heads = N_ATT // 2
    rows = heads * SAMPLE_ROWS
    lane_head = lax.broadcasted_iota(jnp.int32, (rows, HALF_ATT), 1) >> _log2(ATT_DH)
    row_head = lax.broadcasted_iota(jnp.int32, (rows, HALF_ATT), 0) >> _log2(SAMPLE_ROWS)
    own = lane_head == row_head
    q8 = jnp.where(own, jnp.concatenate([q_ref[0]] * heads, axis=0), 0.0).astype(BF16)
    scale = ATT_DH ** -0.5
    slope = slope_ref[0][:, 0:1]

    def weights(dist):
        mult = jnp.zeros(dist.shape, F32)
        for dil in DILATIONS:
            hit = (dist >= 0) & (dist <= BAND * dil) & ((dist & (dil - 1)) == 0)
            mult = mult + jnp.where(hit, 1.0, 0.0)
        return mult

    def mask(s, dist):
        mult = weights(dist)
        return jnp.where(mult > 0.0, s * scale - slope * dist.astype(F32), NEG_BIG), mult

    t_c = lax.broadcasted_iota(jnp.int32, (rows, w), 0) & (SAMPLE_ROWS - 1)
    pos_c = lax.broadcasted_iota(jnp.int32, (rows, w), 1)
    s_c, mult_c = mask(_dot(q8, kt.astype(BF16)), w + t_c - pos_c)
    t_n = lax.broadcasted_iota(jnp.int32, (rows, SAMPLE_ROWS), 0) & (SAMPLE_ROWS - 1)
    pos_n = lax.broadcasted_iota(jnp.int32, (rows, SAMPLE_ROWS), 1)
    s_n, mult_n = mask(_dot_nt(q8, kn.astype(BF16)), t_n - pos_n)

    m = jnp.maximum(jnp.max(s_c, axis=1, keepdims=True), jnp.max(s_n, axis=1, keepdims=True))
    e_c = jnp.exp(s_c - m) * mult_c
    e_n = jnp.exp(s_n - m) * mult_n
    den = jnp.sum(e_c, axis=1, keepdims=True) + jnp.sum(e_n, axis=1, keepdims=True)
    o8 = (_dot_nt(e_c.astype(BF16), vt.astype(BF16)) + _dot(e_n.astype(BF16), vn.astype(BF16))) / den
    o8 = jnp.where(own, o8, 0.0)
    o = o8[0:SAMPLE_ROWS]
    for hd in range(1, heads):
        o = o + o8[hd * SAMPLE_ROWS:(hd + 1) * SAMPLE_ROWS]
    o_ref[0] = o


def _cache_attn(z3, cache_kt, cache_vt, layer, out_prev, n_new):
    depth, b, _, w = cache_kt.shape
    halves = D_ATT // HALF_ATT
    cols = D_ATT // HALF_ATT
    new = lambda colblk: pl.BlockSpec((1, SAMPLE_ROWS, HALF_ATT), lambda bi, hf: (bi, 0, colblk * cols + hf))
    cache = pl.BlockSpec((1, 1, HALF_ATT, w), lambda bi, hf: (layer, bi, hf, 0))
    heads = N_ATT // halves
    slope_tbl = np.broadcast_to(np.repeat(np.asarray(ALIBI_SLOPES, np.float32), SAMPLE_ROWS)
                                .reshape(halves, heads * SAMPLE_ROWS, 1), (halves, heads * SAMPLE_ROWS, LANE))
    in_specs = [new(COL_QC), new(COL_KC), new(COL_VC), cache, cache,
                pl.BlockSpec((1, heads * SAMPLE_ROWS, LANE), lambda bi, hf: (hf, 0, 0))]
    args = [z3, z3, z3, cache_kt, cache_vt, jnp.asarray(slope_tbl)]
    aliases = {}
    if out_prev is not None:
        in_specs += [pl.BlockSpec(memory_space=pl.ANY)] * 2
        aliases = {len(args): 1, len(args) + 1: 2}
        args += list(out_prev)
    stacked = jax.ShapeDtypeStruct((depth, b, D_ATT, w), F32)
    return pl.pallas_call(
        functools.partial(_cache_attn_kernel, n_new=n_new),
        grid=(b, halves),
        in_specs=in_specs,
        out_specs=(pl.BlockSpec((1, SAMPLE_ROWS, HALF_ATT), lambda bi, hf: (bi, 0, hf)), cache, cache),
        out_shape=(jax.ShapeDtypeStruct((b, SAMPLE_ROWS, D_ATT), F32), stacked, stacked),
        input_output_aliases=aliases,
        compiler_params=_cparams("parallel", "parallel"),
        name="cache_attn",
    )(*args)


def _wout_kernel(*refs, n_pat):
    x_ref, ya_ref, yb_ref = refs[0:3]
    n_lse = n_pat if n_pat > 1 else 0
    o_refs = refs[3:3 + n_pat]
    l_refs = refs[3 + n_pat:3 + n_pat + n_lse]
    gn_ref, w_ref, out_ref = refs[3 + n_pat + n_lse:]
    if n_pat == 1:
        yc = o_refs[0][...]
    else:
        lses = [l[...] for l in l_refs]
        m = functools.reduce(jnp.maximum, lses)
        ws = [jnp.exp(l - m) for l in lses]
        yc = sum(wgt * o[...] for wgt, o in zip(ws, o_refs)) / sum(ws)
    ycn = _rms(yc, gn_ref[...])
    acc = _dot(ya_ref[...].astype(BF16), w_ref[0:D_LRU, :])
    acc = acc + _dot(yb_ref[...].astype(BF16), w_ref[D_LRU:D_LRU + D_HGRN, :])
    acc = acc + _dot(ycn.astype(BF16), w_ref[D_LRU + D_HGRN:D_MODEL, :])
    out_ref[...] = x_ref[...] + acc


def _wout(x, ya, yb, outs, lses, gn_c, w_out, tm):
    m = x.shape[0]
    n_pat = len(outs)
    lses = list(lses) if n_pat > 1 else []
    spec = lambda n: pl.BlockSpec((tm, n), lambda i: (i, 0))
    return pl.pallas_call(
        functools.partial(_wout_kernel, n_pat=n_pat),
        grid=(m // tm,),
        in_specs=[spec(D_MODEL), spec(D_LRU), spec(D_HGRN)] + [spec(D_ATT)] * (n_pat + len(lses))
        + [pl.BlockSpec((1, D_ATT), lambda i: (0, 0)), _resident((D_MODEL, D_MODEL))],
        out_specs=spec(D_MODEL),
        out_shape=jax.ShapeDtypeStruct((m, D_MODEL), F32),
        compiler_params=_cparams("parallel"),
        name="wout",
    )(x, ya, yb, *outs, *lses, gn_c.reshape(1, D_ATT), w_out)


def _xattn_kernel(x_ref, g_ref, wq_ref, mk_ref, mv_ref, wo_ref, o_ref):
    x = x_ref[0]
    q = _dot(_rms(x, g_ref[...]).astype(BF16), wq_ref[...])
    mk = mk_ref[0].astype(BF16)
    mv = mv_ref[0].astype(BF16)
    acc = x
    scale = XHEAD_DIM ** -0.5
    for hd in range(N_XHEADS):
        lanes = slice(hd * XHEAD_DIM, (hd + 1) * XHEAD_DIM)
        s = _dot_nt(q[:, lanes].astype(BF16), mk[:, lanes]) * scale
        e = jnp.exp(s - jnp.max(s, axis=1, keepdims=True))
        p = e / jnp.sum(e, axis=1, keepdims=True)
        oh = _dot(p.astype(BF16), mv[:, lanes])
        acc = acc + _dot(oh.astype(BF16), wo_ref[lanes, :])
    o_ref[0] = acc


def _xattn(x3, g, wq, mem, k_col, v_col, wo, tm):
    b, t, _ = x3.shape
    const = lambda bi, ti: (0, 0)
    mat = _resident((D_MODEL, D_MODEL))
    mem_k, mem_v = mem
    return pl.pallas_call(
        _xattn_kernel,
        grid=(b, t // tm),
        in_specs=[pl.BlockSpec((1, tm, D_MODEL), lambda bi, ti: (bi, ti, 0)),
                  pl.BlockSpec((1, D_MODEL), const), mat,
                  pl.BlockSpec((1, N_MEM, D_MODEL), lambda bi, ti: (bi, 0, k_col)),
                  pl.BlockSpec((1, N_MEM, D_MODEL), lambda bi, ti: (bi, 0, v_col)),
                  mat],
        out_specs=pl.BlockSpec((1, tm, D_MODEL), lambda bi, ti: (bi, ti, 0)),
        out_shape=jax.ShapeDtypeStruct((b, t, D_MODEL), F32),
        compiler_params=_cparams("parallel", "parallel"),
        name="xattn",
    )(x3, g.reshape(1, D_MODEL), wq, mem_k, mem_v, wo)


def _mem_attn_kernel(q_ref, mk_ref, mv_ref, o_ref):
    q = q_ref[0]
    mk = mk_ref[0].astype(BF16)
    mv = mv_ref[0].astype(BF16)
    scale = XHEAD_DIM ** -0.5
    for hd in range(N_XHEADS):
        lanes = slice(hd * XHEAD_DIM, (hd + 1) * XHEAD_DIM)
        s = _dot_nt(q[:, lanes].astype(BF16), mk[:, lanes]) * scale
        e = jnp.exp(s - jnp.max(s, axis=1, keepdims=True))
        p = e / jnp.sum(e, axis=1, keepdims=True)
        o_ref[0, :, lanes] = _dot(p.astype(BF16), mv[:, lanes])


def _mem_attn(q3, mem_k, mem_v):
    b, t, _ = q3.shape
    row = pl.BlockSpec((1, t, D_MODEL), lambda bi: (bi, 0, 0))
    mem = pl.BlockSpec((1, N_MEM, D_MODEL), lambda bi: (bi, 0, 0))
    return pl.pallas_call(
        _mem_attn_kernel,
        grid=(b,),
        in_specs=[row, mem, mem],
        out_specs=row,
        out_shape=jax.ShapeDtypeStruct((b, t, D_MODEL), F32),
        compiler_params=_cparams("parallel"),
        name="mem_attn",
    )(q3, mem_k, mem_v)


def _proj_res_kernel(x_ref, a_ref, w_ref, o_ref):
    o_ref[...] = x_ref[...] + _dot(a_ref[...].astype(BF16), w_ref[...])


def _proj_res(x, a, w, tm):
    m, k = a.shape
    n = w.shape[1]
    return pl.pallas_call(
        _proj_res_kernel,
        grid=(m // tm,),
        in_specs=[pl.BlockSpec((tm, n), lambda i: (i, 0)),
                  pl.BlockSpec((tm, k), lambda i: (i, 0)),
                  _resident((k, n))],
        out_specs=pl.BlockSpec((tm, n), lambda i: (i, 0)),
        out_shape=jax.ShapeDtypeStruct((m, n), F32),
        compiler_params=_cparams("parallel"),
        name="proj_res",
    )(x, a, w)


def _block_diag(w):
    n, c, d = w.shape
    eye = jnp.eye(n, dtype=w.dtype)
    return jnp.einsum("ncd,nm->ncmd", w, eye).reshape(n * c, n * d)


def kernel(x_prompt, x_sample, state_lru_h, state_lru_conv, state_hgrn, cache_swa_k, cache_swa_v, cache_mem_k, cache_mem_v, mem_prompt, n_ffn1, ffn1_wg, ffn1_wu, ffn1_wd, n_mix, w_in, lru_conv_w, lru_conv_b, lru_wa, lru_ba, lru_wx, lru_bx, lru_lambda, hgrn_lb, hgrn_norm, gn_a, gn_c, w_out, n_cross, x_wq, x_wk, x_wv, x_wo, n_ffn2, ffn2_wg, ffn2_wu, ffn2_wd, n_final):
    bp, tp, _ = x_prompt.shape
    bs, ts_new, _ = x_sample.shape
    w_buf = cache_swa_k.shape[2]
    assert ts_new <= SAMPLE_ROWS and tp % (BAND * max(DILATIONS)) == 0 and tp >= MAX_WINDOW
    tm_p = 512
    tm_s = bs * SAMPLE_ROWS

    xp = x_prompt.reshape(bp * tp, D_MODEL)
    xs = jnp.pad(x_sample, ((0, 0), (0, SAMPLE_ROWS - ts_new), (0, 0))).reshape(tm_s, D_MODEL)
    mem2 = mem_prompt.reshape(bp * N_MEM, D_MODEL)
    ones = jnp.ones((D_MODEL,), F32)

    outs = {name: [] for name in ("p_h", "p_c", "p_s", "p_k", "p_v", "p_mk", "p_mv", "s_h", "s_c", "s_s")}
    to_feature_major = lambda c: jnp.transpose(c, (0, 1, 3, 4, 2)).reshape(DEPTH, bs, D_ATT, w_buf)
    cache_kt, cache_vt = to_feature_major(cache_swa_k), to_feature_major(cache_swa_v)
    swa_out = None
    yp = ys = None
    for l in range(DEPTH):
        bf = lambda w: w[l].astype(BF16)
        wg1, wu1, wd1 = bf(ffn1_wg), bf(ffn1_wu), bf(ffn1_wd)
        wg2, wu2, wd2 = bf(ffn2_wg), bf(ffn2_wu), bf(ffn2_wd)
        win, wo_mix, wq, wo_x = bf(w_in), bf(w_out), bf(x_wq), bf(x_wo)
        wkv = jnp.concatenate([x_wk[l], x_wv[l]], axis=1).astype(BF16)
        lp = dict(conv_w=lru_conv_w[l], conv_b=lru_conv_b[l].reshape(1, D_LRU),
                  wa=_block_diag(lru_wa[l]).astype(BF16), ba=lru_ba[l].reshape(1, D_LRU),
                  wx=_block_diag(lru_wx[l]).astype(BF16), bx=lru_bx[l].reshape(1, D_LRU),
                  lam=lru_lambda[l].reshape(1, D_LRU), gn_a=gn_a[l].reshape(1, D_LRU))
        last = l == DEPTH - 1

        mkv = _proj(mem2, ones, wkv, tm=bp * N_MEM, norm=False).reshape(bp, N_MEM, 2 * D_MODEL)
        xp = _ffn(xp, n_ffn1[l], wg1, wu1, wd1, tm_p)
        z3 = _proj(xp, n_mix[l], win, tm_p).reshape(bp, tp, D_IN)
        ya, hl = _lru(z3, jnp.zeros((bp, SUBLANE, D_LRU), F32), jnp.zeros((bp, 1, D_LRU), F32), lp, tt=256)
        yb, st = _hgrn(z3, jnp.zeros((bp, D_HGRN, D_HGRN), F32), hgrn_lb, hgrn_norm[l], l,
                       c=HGRN_CHUNK, chunks=2, n_seq=bp)
        yc = _dil_attn(z3)
        xp = _wout(xp, ya.reshape(bp * tp, D_LRU), yb.reshape(bp * tp, D_HGRN),
                   [yc.reshape(bp * tp, D_ATT)], [], gn_c[l], wo_mix, tm_p)
        xp = _xattn(xp.reshape(bp, tp, D_MODEL), n_cross[l], wq, (mkv, mkv), 0, 1, wo_x, tm_p)
        xp = xp.reshape(bp * tp, D_MODEL)
        if last:
            xp, yp = _ffn(xp, n_ffn2[l], wg2, wu2, wd2, tm_p, g_final=n_final)
        else:
            xp = _ffn(xp, n_ffn2[l], wg2, wu2, wd2, tm_p)
        outs["p_h"].append(hl[:, SUBLANE - 1])
        outs["p_c"].append(z3[:, tp - (CONV_W - 1):, 0:D_LRU])
        outs["p_s"].append(_blockdiag_t_to_state(st))
        keep = min(MAX_WINDOW, tp)
        kcol = COL_KC * D_ATT
        vcol = COL_VC * D_ATT
        outs["p_k"].append(z3[:, tp - keep:, kcol:kcol + D_ATT].reshape(bp, keep, N_ATT, ATT_DH))
        outs["p_v"].append(z3[:, tp - keep:, vcol:vcol + D_ATT].reshape(bp, keep, N_ATT, ATT_DH))
        outs["p_mk"].append(mkv[:, :, 0:D_MODEL].reshape(bp, N_MEM, N_XHEADS, XHEAD_DIM))
        outs["p_mv"].append(mkv[:, :, D_MODEL:].reshape(bp, N_MEM, N_XHEADS, XHEAD_DIM))

        xs = _ffn(xs, n_ffn1[l], wg1, wu1, wd1, tm_s)
        zs3 = _proj(xs, n_mix[l], win, tm_s).reshape(bs, SAMPLE_ROWS, D_IN)
        conv_buf = jnp.pad(state_lru_conv[l], ((0, 0), (SUBLANE - (CONV_W - 1), 0), (0, 0)))
        ya, hl = _lru(zs3, conv_buf, state_lru_h[l].reshape(bs, 1, D_LRU), lp, tt=SAMPLE_ROWS, valid=ts_new)
        yb, st = _hgrn(zs3, _state_to_blockdiag_t(state_hgrn[l]), hgrn_lb, hgrn_norm[l], l,
                       c=SAMPLE_ROWS, chunks=1, n_seq=4, valid=ts_new)
        yc, ko, vo = _cache_attn(zs3, cache_kt, cache_vt, l, swa_out, ts_new)
        swa_out = (ko, vo)
        xs = _wout(xs, ya.reshape(tm_s, D_LRU), yb.reshape(tm_s, D_HGRN), [yc.reshape(tm_s, D_ATT)], [],
                   gn_c[l], wo_mix, tm_s)
        mem_k = cache_mem_k[l].reshape(bs, N_MEM, D_MODEL)
        mem_v = cache_mem_v[l].reshape(bs, N_MEM, D_MODEL)
        qs = _proj(xs, n_cross[l], wq, tm_s).reshape(bs, SAMPLE_ROWS, D_MODEL)
        xs = _proj_res(xs, _mem_attn(qs, mem_k, mem_v).reshape(tm_s, D_MODEL), wo_x, tm_s)
        if last:
            xs, ys = _ffn(xs, n_ffn2[l], wg2, wu2, wd2, tm_s, g_final=n_final)
        else:
            xs = _ffn(xs, n_ffn2[l], wg2, wu2, wd2, tm_s)
        outs["s_h"].append(hl[:, SUBLANE - 1])
        outs["s_c"].append(zs3[:, ts_new - (CONV_W - 1):ts_new, 0:D_LRU])
        outs["s_s"].append(_blockdiag_t_to_state(st))

    y_prompt = yp.reshape(bp, tp, D_MODEL)
    y_sample = ys.reshape(bs, SAMPLE_ROWS, D_MODEL)[:, :ts_new]
    stack = lambda name: jnp.stack(outs[name])
    from_feature_major = lambda c: jnp.transpose(c.reshape(DEPTH, bs, N_ATT, ATT_DH, w_buf), (0, 1, 4, 2, 3))
    return (y_prompt, y_sample,
            stack("p_h"), stack("p_c"), stack("p_s"), stack("p_k"), stack("p_v"), stack("p_mk"), stack("p_mv"),
            stack("s_h"), stack("s_c"), stack("s_s"), from_feature_major(swa_out[0]), from_feature_major(swa_out[1]))
```

```python
import functools

import numpy as np
import jax
import jax.numpy as jnp
from jax import lax
from jax.experimental import pallas as pl
from jax.experimental.pallas import tpu as pltpu

F32 = jnp.float32
BF16 = jnp.bfloat16

D_MODEL = 1024
DEPTH = 4
D_LRU = 256
N_LRU_BLOCKS = 4
CONV_W = 4
LRU_C = 8.0
HGRN_DK = 64
HGRN_DV = 64
D_HGRN = 256
N_HGRN = 4
HGRN_CHUNK = 64
ATT_DH = 64
D_ATT = 512
N_ATT = 8
DILATIONS = (1, 4, 16)
BAND = 128
SUPER = BAND * max(DILATIONS)
PAIR = 2 * ATT_DH
UNITS_IN_FLIGHT = 8
MAX_WINDOW = 2048
N_MEM = 256
N_XHEADS = 4
XHEAD_DIM = 256
D_FF = 2816
D_IN = 3072
EPS = 1e-6

COL_XA, COL_GA, COL_QB, COL_FB, COL_IB, COL_GB = 0, 1, 2, 3, 4, 5
COL_QC, COL_KC, COL_VC = 3, 4, 5
ATT_COLS_PER_ROW = D_IN // D_ATT

HALF_ATT = D_ATT // 2
SUBLANE = 8
LANE = 128
SAMPLE_ROWS = 8
VMEM_LIMIT = 56 * 1024 * 1024
NEG_BIG = -1e30

ALIBI_SLOPES = tuple(float(2.0 ** (-8.0 * (h + 1) / N_ATT)) for h in range(N_ATT))


def _log2(n):
    assert n & (n - 1) == 0
    return n.bit_length() - 1


def _cparams(*sem):
    return pltpu.CompilerParams(dimension_semantics=sem, vmem_limit_bytes=VMEM_LIMIT)


def _rms(x, g):
    ms = jnp.mean(x * x, axis=-1, keepdims=True)
    return x * lax.rsqrt(ms + EPS) * g


def _expm1(x):
    e = jnp.exp(x)
    near = (e - 1.0) * x / jnp.log(e)
    return jnp.where(e == 1.0, x, jnp.where(jnp.abs(x) > 1.0, e - 1.0, near))


def _dot(a, b):
    return jnp.dot(a, b, preferred_element_type=F32)


def _dot_nt(a, b):
    return lax.dot_general(a, b, (((1,), (1,)), ((), ())), preferred_element_type=F32)


def _dot_tn(a, b):
    return lax.dot_general(a, b, (((0,), (0,)), ((), ())), preferred_element_type=F32)


def _split3(x):
    hi = x.astype(BF16)
    r1 = x - hi.astype(F32)
    mid = r1.astype(BF16)
    lo = (r1 - mid.astype(F32)).astype(BF16)
    return hi, mid, lo


def _dot_exact_lhs(a_bf16, x):
    return sum(_dot(a_bf16, p) for p in _split3(x))


def _dot_exact_tn(x, a_bf16):
    return sum(_dot_tn(p, a_bf16) for p in _split3(x))


def _ffn_kernel(*refs, final_norm):
    if final_norm:
        x_ref, g_ref, wg_ref, wu_ref, wd_ref, gf_ref, o_ref, y_ref = refs
    else:
        x_ref, g_ref, wg_ref, wu_ref, wd_ref, o_ref = refs
    x = x_ref[...]
    h = _rms(x, g_ref[...]).astype(BF16)
    gate = _dot(h, wg_ref[...])
    up = _dot(h, wu_ref[...])
    act = (gate * jax.nn.sigmoid(gate) * up).astype(BF16)
    out = x + 0.5 * _dot(act, wd_ref[...])
    o_ref[...] = out
    if final_norm:
        y_ref[...] = _rms(out, gf_ref[...])


def _resident(shape):
    return pl.BlockSpec(shape, lambda *_: (0,) * len(shape), pipeline_mode=pl.Buffered(1))


def _ffn(x, g, wg, wu, wd, tm, g_final=None):
    m = x.shape[0]
    final_norm = g_final is not None
    row = pl.BlockSpec((tm, D_MODEL), lambda i: (i, 0))
    vec = _resident((1, D_MODEL))
    in_specs = [row, vec, _resident((D_MODEL, D_FF)), _resident((D_MODEL, D_FF)), _resident((D_FF, D_MODEL))]
    args = [x, g.reshape(1, D_MODEL), wg, wu, wd]
    out_shape = jax.ShapeDtypeStruct((m, D_MODEL), F32)
    out_specs = row
    if final_norm:
        in_specs.append(vec)
        args.append(g_final.reshape(1, D_MODEL))
        out_shape = (out_shape, out_shape)
        out_specs = (row, row)
    return pl.pallas_call(
        functools.partial(_ffn_kernel, final_norm=final_norm),
        grid=(m // tm,),
        in_specs=in_specs, out_specs=out_specs, out_shape=out_shape,
        compiler_params=_cparams("parallel"),
        name="ffn",
    )(*args)


def _proj_kernel(x_ref, g_ref, w_ref, o_ref, *, norm):
    x = x_ref[...]
    if norm:
        x = _rms(x, g_ref[...])
    o_ref[...] = _dot(x.astype(BF16), w_ref[...])


def _proj(x, g, w, tm, norm=True):
    m, k = x.shape
    n = w.shape[1]
    return pl.pallas_call(
        functools.partial(_proj_kernel, norm=norm),
        grid=(m // tm,),
        in_specs=[pl.BlockSpec((tm, k), lambda i: (i, 0)),
                  pl.BlockSpec((1, k), lambda i: (0, 0)),
                  _resident((k, n))],
        out_specs=pl.BlockSpec((tm, n), lambda i: (i, 0)),
        out_shape=jax.ShapeDtypeStruct((m, n), F32),
        compiler_params=_cparams("parallel"),
        name="proj",
    )(x, g.reshape(1, k), w)


def _lru_kernel(xa_ref, ga_ref, cb_ref, h0_ref, cw_ref, cbias_ref, wa_ref, ba_ref, wx_ref, bx_ref,
                lam_ref, gn_ref, ya_ref, hl_ref, xbuf, abuf, ubuf, hc, *, tt, pad, valid):
    ti = pl.program_id(1)

    @pl.when(ti == 0)
    def _():
        xbuf[0:SUBLANE, :] = cb_ref[0]
        hc[...] = jnp.broadcast_to(h0_ref[0], (SUBLANE, D_LRU))
        for s in range(2):
            abuf[s, 0:pad, :] = jnp.ones((pad, D_LRU), F32)
            ubuf[s, 0:pad, :] = jnp.zeros((pad, D_LRU), F32)

    x = xa_ref[0]
    xbuf[SUBLANE:SUBLANE + tt, :] = x
    y = cbias_ref[...] + cw_ref[CONV_W - 1:CONV_W, :] * x
    for tap in range(CONV_W - 1):
        back = CONV_W - 1 - tap
        y = y + cw_ref[tap:tap + 1, :] * xbuf[SUBLANE - back:SUBLANE - back + tt, :]
    xbuf[0:SUBLANE, :] = x[tt - SUBLANE:tt, :]

    yb = y.astype(BF16)
    r = jax.nn.sigmoid(_dot(yb, wa_ref[...]) + ba_ref[...])
    i = jax.nn.sigmoid(_dot(yb, wx_ref[...]) + bx_ref[...])
    lam = lam_ref[...]
    softplus_neg_lam = jnp.maximum(-lam, 0.0) + jnp.log1p(jnp.exp(-jnp.abs(lam)))
    log_a = -LRU_C * r * softplus_neg_lam
    a = jnp.exp(log_a)
    u = jnp.sqrt(-_expm1(2.0 * log_a)) * (i * y)
    if valid is not None:
        keep = lax.broadcasted_iota(jnp.int32, (tt, D_LRU), 0) < valid
        a = jnp.where(keep, a, 1.0)
        u = jnp.where(keep, u, 0.0)

    abuf[0, pad:pad + tt, :] = a
    ubuf[0, pad:pad + tt, :] = u
    cur, k = 0, 1
    while k < tt:
        a_c = abuf[cur, pad:pad + tt, :]
        u_c = ubuf[cur, pad:pad + tt, :]
        a_s = abuf[cur, pad - k:pad - k + tt, :]
        u_s = ubuf[cur, pad - k:pad - k + tt, :]
        abuf[1 - cur, pad:pad + tt, :] = a_c * a_s
        ubuf[1 - cur, pad:pad + tt, :] = a_c * u_s + u_c
        cur, k = 1 - cur, 2 * k
    h = abuf[cur, pad:pad + tt, :] * hc[SUBLANE - 1:SUBLANE, :] + ubuf[cur, pad:pad + tt, :]
    hc[...] = h[tt - SUBLANE:tt, :]
    hl_ref[0] = h[tt - SUBLANE:tt, :]
    ya_ref[0] = _rms(h * jax.nn.gelu(ga_ref[0]), gn_ref[...])


def _lru(z3, conv_buf, h0, lp, tt, valid=None):
    b, t, _ = z3.shape
    pad = max(tt // 2, SUBLANE)
    vec = pl.BlockSpec((1, D_LRU), lambda bi, ti: (0, 0))
    mat = pl.BlockSpec((D_LRU, D_LRU), lambda bi, ti: (0, 0))
    return pl.pallas_call(
        functools.partial(_lru_kernel, tt=tt, pad=pad, valid=valid),
        grid=(b, t // tt),
        in_specs=[pl.BlockSpec((1, tt, D_LRU), lambda bi, ti: (bi, ti, COL_XA)),
                  pl.BlockSpec((1, tt, D_LRU), lambda bi, ti: (bi, ti, COL_GA)),
                  pl.BlockSpec((1, SUBLANE, D_LRU), lambda bi, ti: (bi, 0, 0)),
                  pl.BlockSpec((1, 1, D_LRU), lambda bi, ti: (bi, 0, 0)),
                  pl.BlockSpec((CONV_W, D_LRU), lambda bi, ti: (0, 0)),
                  vec, mat, vec, mat, vec, vec, vec],
        out_specs=(pl.BlockSpec((1, tt, D_LRU), lambda bi, ti: (bi, ti, 0)),
                   pl.BlockSpec((1, SUBLANE, D_LRU), lambda bi, ti: (bi, 0, 0))),
        out_shape=(jax.ShapeDtypeStruct((b, t, D_LRU), F32),
                   jax.ShapeDtypeStruct((b, SUBLANE, D_LRU), F32)),
        scratch_shapes=[pltpu.VMEM((SUBLANE + tt, D_LRU), F32),
                        pltpu.VMEM((2, pad + tt, D_LRU), F32),
                        pltpu.VMEM((2, pad + tt, D_LRU), F32),
                        pltpu.VMEM((SUBLANE, D_LRU), F32)],
        compiler_params=_cparams("parallel", "arbitrary"),
        name="lru",
    )(z3, z3, conv_buf, h0, lp["conv_w"], lp["conv_b"], lp["wa"], lp["ba"], lp["wx"], lp["bx"],
      lp["lam"], lp["gn_a"])


@functools.lru_cache(maxsize=None)
def _hgrn_consts(c):
    t = np.arange(c)[:, None]
    j = np.arange(c)[None, :]
    blocks = [j <= t, j > t]
    masks = []
    w = c // 2
    while w >= 1:
        p = t % (2 * w)
        ref = t - p + w - 1
        blocks.append(np.where(p >= w, (j > ref) & (j <= t), (j > t) & (j <= ref)))
        masks.append((t // (2 * w) == j // (2 * w)) & (t % (2 * w) >= w) & (j % (2 * w) < w))
        w //= 2
    masks.append(t == j)
    dstack = np.concatenate(blocks, axis=0).astype(np.float32)
    masks = np.stack([np.tile(m, (1, N_HGRN)) for m in masks]).astype(np.float32)
    lane_head = np.arange(D_HGRN)[None, :] // HGRN_DK
    head_rows = np.repeat(np.arange(N_HGRN), c)[:, None]
    head_mask = (head_rows == lane_head).astype(np.float32)
    same_head = (np.arange(D_HGRN)[:, None] // HGRN_DV == lane_head).astype(np.float32)
    return dstack, masks, head_mask, same_head


def _hgrn_kernel(q_ref, f_ref, v_ref, g_ref, lbraw_ref, ng_ref, s0_ref, dst_ref, msk_ref, hm_ref, sh_ref,
                 y_ref, st_ref, st_scr, *, c, chunks, layer, valid):
    ti = pl.program_id(1)
    n_seq = q_ref.shape[0]

    @pl.when(ti == 0)
    def _():
        st_scr[...] = s0_ref[...]

    lbraw = lbraw_ref[...]
    e = jnp.exp(lbraw - jnp.max(lbraw, axis=0, keepdims=True))
    sm = e / jnp.sum(e, axis=0, keepdims=True)
    lrow = lax.broadcasted_iota(jnp.int32, sm.shape, 0)
    lb = jnp.sum(jnp.where((lrow >= 1) & (lrow <= layer), sm, 0.0), axis=0, keepdims=True)

    n_levels = msk_ref.shape[0] - 1
    hm = hm_ref[...]
    hm_b = hm.astype(BF16)
    same_head = sh_ref[...]
    shb = same_head.astype(BF16)

    def per_head_rows(x):
        if c % 16 == 0:
            return jnp.concatenate([x.astype(BF16)] * N_HGRN, axis=0) * hm_b
        return (jnp.concatenate([x] * N_HGRN, axis=0) * hm).astype(BF16)

    for ci in range(chunks):
        rows = slice(ci * c, (ci + 1) * c)
        for si in range(n_seq):
            q = q_ref[si, rows, :]
            v = v_ref[si, rows, :]
            g = g_ref[si, rows, :]
            f = lb + (1.0 - lb) * jax.nn.sigmoid(f_ref[si, rows, :])
            log_f = jnp.log(f)
            k = 1.0 - f
            if valid is not None:
                keep = lax.broadcasted_iota(jnp.int32, (c, D_HGRN), 0) < valid
                log_f = jnp.where(keep, log_f, 0.0)
                k = jnp.where(keep, k, 0.0)
            ex = _dot_exact_lhs(dst_ref[...], log_f)
            cum = ex[0:c]
            rem = ex[c:2 * c]

            sc = jnp.zeros((c, N_HGRN * c), F32)
            for li in range(n_levels + 1):
                if li < n_levels:
                    decay = jnp.exp(ex[(2 + li) * c:(3 + li) * c])
                    qt, kt = q * decay, k * decay
                else:
                    qt, kt = q, k
                sc = sc + _dot_nt(qt.astype(BF16), per_head_rows(kt)) * msk_ref[li]
            o = _dot(sc.astype(BF16), per_head_rows(v))

            st = st_scr[si]
            o = o + _dot_nt((q * jnp.exp(cum)).astype(BF16), st.astype(BF16))
            upd = _dot_tn(v.astype(BF16), (k * jnp.exp(rem)).astype(BF16))
            st_scr[si] = st * jnp.exp(cum[c - 1:c, :]) + upd * same_head

            o2 = o * o
            o2_hi = o2.astype(BF16)
            o2_lo = (o2 - o2_hi.astype(F32)).astype(BF16)
            ms = (_dot(o2_hi, shb) + _dot(o2_lo, shb)) * (1.0 / HGRN_DV)
            y_ref[si, rows, :] = o * lax.rsqrt(ms + EPS) * ng_ref[...] * (g * jax.nn.sigmoid(g))
    st_ref[...] = st_scr[...]


def _hgrn(z3, s0t, lb_raw, norm_g, layer, c, chunks, n_seq, valid=None):
    b, t, _ = z3.shape
    tt = c * chunks
    dstack, masks, head_mask, same_head = _hgrn_consts(c)
    const2 = lambda bi, ti: (0, 0)
    col = lambda cidx: pl.BlockSpec((n_seq, tt, D_HGRN), lambda bi, ti: (bi, ti, cidx))
    state = pl.BlockSpec((n_seq, D_HGRN, D_HGRN), lambda bi, ti: (bi, 0, 0))
    return pl.pallas_call(
        functools.partial(_hgrn_kernel, c=c, chunks=chunks, layer=layer, valid=valid),
        grid=(b // n_seq, t // tt),
        in_specs=[col(COL_QB), col(COL_FB), col(COL_IB), col(COL_GB),
                  pl.BlockSpec((DEPTH, D_HGRN), const2),
                  pl.BlockSpec((1, D_HGRN), const2),
                  state,
                  pl.BlockSpec(dstack.shape, const2),
                  pl.BlockSpec(masks.shape, lambda bi, ti: (0, 0, 0)),
                  pl.BlockSpec(head_mask.shape, const2),
                  pl.BlockSpec(same_head.shape, const2)],
        out_specs=(pl.BlockSpec((n_seq, tt, D_HGRN), lambda bi, ti: (bi, ti, 0)), state),
        out_shape=(jax.ShapeDtypeStruct((b, t, D_HGRN), F32),
                   jax.ShapeDtypeStruct((b, D_HGRN, D_HGRN), F32)),
        scratch_shapes=[pltpu.VMEM((n_seq, D_HGRN, D_HGRN), F32)],
        compiler_params=_cparams("parallel", "arbitrary"),
        name="hgrn",
    )(z3, z3, z3, z3, lb_raw, jnp.tile(norm_g, N_HGRN).reshape(1, D_HGRN), s0t,
      jnp.asarray(dstack, BF16), jnp.asarray(masks), jnp.asarray(head_mask), jnp.asarray(same_head))


def _state_to_blockdiag_t(s):
    b = s.shape[0]
    eye = jnp.eye(N_HGRN, dtype=s.dtype)
    return jnp.einsum("bhkv,hg->bhvgk", s, eye).reshape(b, D_HGRN, D_HGRN)


def _blockdiag_t_to_state(st):
    b = st.shape[0]
    x = st.reshape(b, N_HGRN, HGRN_DV, N_HGRN, HGRN_DK)
    diag = jnp.stack([x[:, h, :, h, :] for h in range(N_HGRN)], axis=1)
    return jnp.swapaxes(diag, 2, 3)


def _strided_rows(start, dil):
    return pl.ds(start, BAND, stride=dil) if dil > 1 else pl.ds(start, BAND)


def _dil_attn_kernel(q_ref, kp_ref, kc_ref, vp_ref, vc_ref, slope_ref, o_ref, o_scr, d_scr, m_scr):
    first_block = pl.program_id(1) == 0
    n2 = 2 * BAND
    row = lax.broadcasted_iota(jnp.int32, (n2, BAND), 0)
    col = lax.broadcasted_iota(jnp.int32, (n2, BAND), 1)
    tq = row & (BAND - 1)
    upper = col > tq
    diag = col == tq
    steps = (tq - col + jnp.where(upper, BAND, 0)).astype(F32)
    slope = slope_ref[0]
    lane_first = lax.broadcasted_iota(jnp.int32, (BAND, PAIR), 1) < ATT_DH
    no_prev = jnp.where(first_block, NEG_BIG, 0.0)
    scale = ATT_DH ** -0.5

    ones = jnp.ones((BAND, PAIR), BF16)

    def scores(unit):
        g, cur, prev, prev_refs, b_s, b_x = unit
        kp_r, vp_r = prev_refs
        q = q_ref[0, cur, :] * scale
        q2 = jnp.concatenate([jnp.where(lane_first, q, 0.0), jnp.where(lane_first, 0.0, q)],
                             axis=0).astype(BF16)
        keys = jnp.concatenate([kp_r[0, prev, :], kc_ref[0, cur, :]], axis=0).astype(BF16)
        sps = _dot_nt(q2, keys)
        sp, sc = sps[:, 0:BAND], sps[:, BAND:n2]
        return jnp.where(upper, sp, sc) - b_s, sp - b_x

    def probs(sx):
        s, x = sx
        m = jnp.max(jnp.maximum(s, x), axis=1, keepdims=True)
        e = jnp.exp(s - m)
        ex = jnp.exp(x - m)
        return jnp.where(upper, e, ex).astype(BF16), jnp.where(upper, 0.0, e).astype(BF16), m

    def values(unit, pm):
        g, cur, prev, prev_refs, _, _ = unit
        kp_r, vp_r = prev_refs
        p_prev, p_cur, m = pm
        vp1 = jnp.concatenate([vp_r[0, prev, :].astype(BF16), ones], axis=1)
        vc1 = jnp.concatenate([vc_ref[0, cur, :].astype(BF16), ones], axis=1)
        pvd = _dot(p_prev, vp1) + _dot(p_cur, vc1)
        pv, den = pvd[:, 0:PAIR], pvd[:, PAIR:2 * PAIR]
        m_b = jnp.broadcast_to(m, (n2, PAIR))
        o_scr[g, cur, :] = jnp.where(lane_first, pv[0:BAND], pv[BAND:n2])
        d_scr[g, cur, :] = jnp.where(lane_first, den[0:BAND], den[BAND:n2])
        m_scr[g, cur, :] = jnp.where(lane_first, m_b[0:BAND], m_b[BAND:n2])

    units = []
    for g, dil in enumerate(DILATIONS):
        bias = slope * steps * float(dil)
        bias_x = jnp.where(diag, slope * float(BAND * dil), -NEG_BIG)
        bias_first = bias - jnp.where(upper, no_prev, 0.0)
        bias_x_first = bias_x - no_prev
        span = BAND * dil
        for j in range(SUPER // span):
            for r in range(dil):
                cur = _strided_rows(j * span + r, dil)
                if j == 0:
                    units.append((g, cur, _strided_rows(SUPER - span + r, dil), (kp_ref, vp_ref),
                                  bias_first, bias_x_first))
                else:
                    units.append((g, cur, _strided_rows((j - 1) * span + r, dil), (kc_ref, vc_ref),
                                  bias, bias_x))

    for i in range(0, len(units), UNITS_IN_FLIGHT):
        group = units[i:i + UNITS_IN_FLIGHT]
        sx = [scores(u) for u in group]
        pm = [probs(v) for v in sx]
        for u, v in zip(group, pm):
            values(u, v)

    rows_per_iter = 2 * BAND

    def mix(i, carry):
        rows = pl.ds(pl.multiple_of(i * rows_per_iter, rows_per_iter), rows_per_iter)
        ms = [m_scr[g, rows, :] for g in range(len(DILATIONS))]
        m = functools.reduce(jnp.maximum, ms)
        ws = [jnp.exp(mg - m) for mg in ms]
        num = sum(wgt * o_scr[g, rows, :] for g, wgt in enumerate(ws))
        den = sum(wgt * d_scr[g, rows, :] for g, wgt in enumerate(ws))
        o_ref[0, rows, :] = num / den
        return carry

    lax.fori_loop(0, SUPER // rows_per_iter, mix, 0)


def _dil_attn(z3):
    b, t, _ = z3.shape
    n_pairs = N_ATT // 2
    lanes_per_col = D_ATT // PAIR
    blk = lambda colblk, prev: pl.BlockSpec(
        (1, SUPER, PAIR),
        (lambda bi, sb, p: (bi, jnp.maximum(sb - 1, 0), colblk * lanes_per_col + p)) if prev
        else (lambda bi, sb, p: (bi, sb, colblk * lanes_per_col + p)))
    slopes = np.asarray(ALIBI_SLOPES, np.float32).reshape(n_pairs, 2, 1, 1)
    slope_tbl = np.broadcast_to(slopes, (n_pairs, 2, BAND, PAIR)).reshape(n_pairs, 2 * BAND, PAIR)
    return pl.pallas_call(
        _dil_attn_kernel,
        grid=(b, t // SUPER, n_pairs),
        in_specs=[blk(COL_QC, False), blk(COL_KC, True), blk(COL_KC, False),
                  blk(COL_VC, True), blk(COL_VC, False),
                  pl.BlockSpec((1, 2 * BAND, PAIR), lambda bi, sb, p: (p, 0, 0))],
        out_specs=pl.BlockSpec((1, SUPER, PAIR), lambda bi, sb, p: (bi, sb, p)),
        out_shape=jax.ShapeDtypeStruct((b, t, D_ATT), F32),
        scratch_shapes=[pltpu.VMEM((len(DILATIONS), SUPER, PAIR), F32)] * 3,
        compiler_params=_cparams("parallel", "parallel", "parallel"),
        name="dil_attn",
    )(z3, z3, z3, z3, z3, jnp.asarray(slope_tbl))


def _cache_attn_kernel(q_ref, kn_ref, vn_ref, kc_ref, vc_ref, slope_ref, *rest, n_new):
    o_ref, ko_ref, vo_ref = rest[-3:]
    w = kc_ref.shape[-1]
    kn = kn_ref[0]
    vn = vn_ref[0]
    kt = kc_ref[0, 0]
    vt = vc_ref[0, 0]

    sel_row = lax.broadcasted_iota(jnp.int32, (SAMPLE_ROWS, LANE), 0)
    sel_lane = lax.broadcasted_iota(jnp.int32, (SAMPLE_ROWS, LANE), 1)
    sel = jnp.where((sel_lane == LANE - n_new + sel_row) & (sel_row < n_new), 1.0, 0.0).astype(BF16)
    tail_new = lax.broadcasted_iota(jnp.int32, (HALF_ATT, LANE), 1) >= LANE - n_new

    def shift_in(cache_t, new_rows, out_ref):
        rolled = pltpu.roll(cache_t, w - n_new, axis=1)
        out_ref[0, 0, :, 0:w - LANE] = rolled[:, 0:w - LANE]
        out_ref[0, 0, :, w - LANE:w] = jnp.where(tail_new, _dot_exact_tn(new_rows, sel), rolled[:, w - LANE:w])

    shift_in(kt, kn, ko_ref)
    shift_in(vt, vn, vo_ref)

    heads = N_ATT // 2
    rows = heads * SAMPLE_ROWS
    lane_head = lax.broadcasted_iota(jnp.int32, (rows, HALF_ATT), 1) >> _log2(ATT_DH)
    row_head = lax.broadcasted_iota(jnp.int32, (rows, HALF_ATT), 0) >> _log2(SAMPLE_ROWS)
    own = lane_head == row_head
    q8 = jnp.where(own, jnp.concatenate([q_ref[0]] * heads, axis=0), 0.0).astype(BF16)
    scale = ATT_DH ** -0.5
    slope = slope_ref[0][:, 0:1]

    def weights(dist):
        mult = jnp.zeros(dist.shape, F32)
        for dil in DILATIONS:
            hit = (dist >= 0) & (dist <= BAND * dil) & ((dist & (dil - 1)) == 0)
            mult = mult + jnp.where(hit, 1.0, 0.0)
        return mult

    def mask(s, dist):
        mult = weights(dist)
        return jnp.where(mult > 0.0, s * scale - slope * dist.astype(F32), NEG_BIG), mult

    t_c = lax.broadcasted_iota(jnp.int32, (rows, w), 0) & (SAMPLE_ROWS - 1)
    pos_c = lax.broadcasted_iota(jnp.int32, (rows, w), 1)
    s_c, mult_c = mask(_dot(q8, kt.astype(BF16)), w + t_c - pos_c)
    t_n = lax.broadcasted_iota(jnp.int32, (rows, SAMPLE_ROWS), 0) & (SAMPLE_ROWS - 1)
    pos_n = lax.broadcasted_iota(jnp.int32, (rows, SAMPLE_ROWS), 1)
    s_n, mult_n = mask(_dot_nt(q8, kn.astype(BF16)), t_n - pos_n)

    m = jnp.maximum(jnp.max(s_c, axis=1, keepdims=True), jnp.max(s_n, axis=1, keepdims=True))
    e_c = jnp.exp(s_c - m) * mult_c
    e_n = jnp.exp(s_n - m) * mult_n
    den = jnp.sum(e_c, axis=1, keepdims=True) + jnp.sum(e_n, axis=1, keepdims=True)
    o8 = (_dot_nt(e_c.astype(BF16), vt.astype(BF16)) + _dot(e_n.astype(BF16), vn.astype(BF16))) / den
    o8 = jnp.where(own, o8, 0.0)
    o = o8[0:SAMPLE_ROWS]
    for hd in range(1, heads):
        o = o + o8[hd * SAMPLE_ROWS:(hd + 1) * SAMPLE_ROWS]
    o_ref[0] = o


def _cache_attn(z3, cache_kt, cache_vt, layer, out_prev, n_new):
    depth, b, _, w = cache_kt.shape
    halves = D_ATT // HALF_ATT
    cols = D_ATT // HALF_ATT
    new = lambda colblk: pl.BlockSpec((1, SAMPLE_ROWS, HALF_ATT), lambda bi, hf: (bi, 0, colblk * cols + hf))
    cache = pl.BlockSpec((1, 1, HALF_ATT, w), lambda bi, hf: (layer, bi, hf, 0))
    heads = N_ATT // halves
    slope_tbl = np.broadcast_to(np.repeat(np.asarray(ALIBI_SLOPES, np.float32), SAMPLE_ROWS)
                                .reshape(halves, heads * SAMPLE_ROWS, 1), (halves, heads * SAMPLE_ROWS, LANE))
    in_specs = [new(COL_QC), new(COL_KC), new(COL_VC), cache, cache,
                pl.BlockSpec((1, heads * SAMPLE_ROWS, LANE), lambda bi, hf: (hf, 0, 0))]
    args = [z3, z3, z3, cache_kt, cache_vt, jnp.asarray(slope_tbl)]
    aliases = {}
    if out_prev is not None:
        in_specs += [pl.BlockSpec(memory_space=pl.ANY)] * 2
        aliases = {len(args): 1, len(args) + 1: 2}
        args += list(out_prev)
    stacked = jax.ShapeDtypeStruct((depth, b, D_ATT, w), F32)
    return pl.pallas_call(
        functools.partial(_cache_attn_kernel, n_new=n_new),
        grid=(b, halves),
        in_specs=in_specs,
        out_specs=(pl.BlockSpec((1, SAMPLE_ROWS, HALF_ATT), lambda bi, hf: (bi, 0, hf)), cache, cache),
        out_shape=(jax.ShapeDtypeStruct((b, SAMPLE_ROWS, D_ATT), F32), stacked, stacked),
        input_output_aliases=aliases,
        compiler_params=_cparams("parallel", "parallel"),
        name="cache_attn",
    )(*args)


def _wout_kernel(*refs, n_pat):
    x_ref, ya_ref, yb_ref = refs[0:3]
    n_lse = n_pat if n_pat > 1 else 0
    o_refs = refs[3:3 + n_pat]
    l_refs = refs[3 + n_pat:3 + n_pat + n_lse]
    gn_ref, w_ref, out_ref = refs[3 + n_pat + n_lse:]
    if n_pat == 1:
        yc = o_refs[0][...]
    else:
        lses = [l[...] for l in l_refs]
        m = functools.reduce(jnp.maximum, lses)
        ws = [jnp.exp(l - m) for l in lses]
        yc = sum(wgt * o[...] for wgt, o in zip(ws, o_refs)) / sum(ws)
    ycn = _rms(yc, gn_ref[...])
    acc = _dot(ya_ref[...].astype(BF16), w_ref[0:D_LRU, :])
    acc = acc + _dot(yb_ref[...].astype(BF16), w_ref[D_LRU:D_LRU + D_HGRN, :])
    acc = acc + _dot(ycn.astype(BF16), w_ref[D_LRU + D_HGRN:D_MODEL, :])
    out_ref[...] = x_ref[...] + acc


def _wout(x, ya, yb, outs, lses, gn_c, w_out, tm):
    m = x.shape[0]
    n_pat = len(outs)
    lses = list(lses) if n_pat > 1 else []
    spec = lambda n: pl.BlockSpec((tm, n), lambda i: (i, 0))
    return pl.pallas_call(
        functools.partial(_wout_kernel, n_pat=n_pat),
        grid=(m // tm,),
        in_specs=[spec(D_MODEL), spec(D_LRU), spec(D_HGRN)] + [spec(D_ATT)] * (n_pat + len(lses))
        + [pl.BlockSpec((1, D_ATT), lambda i: (0, 0)), _resident((D_MODEL, D_MODEL))],
        out_specs=spec(D_MODEL),
        out_shape=jax.ShapeDtypeStruct((m, D_MODEL), F32),
        compiler_params=_cparams("parallel"),
        name="wout",
    )(x, ya, yb, *outs, *lses, gn_c.reshape(1, D_ATT), w_out)


def _xattn_kernel(x_ref, g_ref, wq_ref, mk_ref, mv_ref, wo_ref, o_ref):
    x = x_ref[0]
    q = _dot(_rms(x, g_ref[...]).astype(BF16), wq_ref[...])
    mk = mk_ref[0].astype(BF16)
    mv = mv_ref[0].astype(BF16)
    acc = x
    scale = XHEAD_DIM ** -0.5
    for hd in range(N_XHEADS):
        lanes = slice(hd * XHEAD_DIM, (hd + 1) * XHEAD_DIM)
        s = _dot_nt(q[:, lanes].astype(BF16), mk[:, lanes]) * scale
        e = jnp.exp(s - jnp.max(s, axis=1, keepdims=True))
        p = e / jnp.sum(e, axis=1, keepdims=True)
        oh = _dot(p.astype(BF16), mv[:, lanes])
        acc = acc + _dot(oh.astype(BF16), wo_ref[lanes, :])
    o_ref[0] = acc


def _xattn(x3, g, wq, mem, k_col, v_col, wo, tm):
    b, t, _ = x3.shape
    const = lambda bi, ti: (0, 0)
    mat = _resident((D_MODEL, D_MODEL))
    mem_k, mem_v = mem
    return pl.pallas_call(
        _xattn_kernel,
        grid=(b, t // tm),
        in_specs=[pl.BlockSpec((1, tm, D_MODEL), lambda bi, ti: (bi, ti, 0)),
                  pl.BlockSpec((1, D_MODEL), const), mat,
                  pl.BlockSpec((1, N_MEM, D_MODEL), lambda bi, ti: (bi, 0, k_col)),
                  pl.BlockSpec((1, N_MEM, D_MODEL), lambda bi, ti: (bi, 0, v_col)),
                  mat],
        out_specs=pl.BlockSpec((1, tm, D_MODEL), lambda bi, ti: (bi, ti, 0)),
        out_shape=jax.ShapeDtypeStruct((b, t, D_MODEL), F32),
        compiler_params=_cparams("parallel", "parallel"),
        name="xattn",
    )(x3, g.reshape(1, D_MODEL), wq, mem_k, mem_v, wo)


def _mem_attn_kernel(q_ref, mk_ref, mv_ref, o_ref):
    q = q_ref[0]
    scale = XHEAD_DIM ** -0.5
    for hd in range(N_XHEADS):
        lanes = slice(hd * XHEAD_DIM, (hd + 1) * XHEAD_DIM)
        kh = mk_ref[0, 0, :, hd, :].astype(BF16)
        vh = mv_ref[0, 0, :, hd, :].astype(BF16)
        s = _dot_nt(q[:, lanes].astype(BF16), kh) * scale
        e = jnp.exp(s - jnp.max(s, axis=1, keepdims=True))
        p = e / jnp.sum(e, axis=1, keepdims=True)
        o_ref[0, :, lanes] = _dot(p.astype(BF16), vh)


def _mem_attn(q3, mem_k, mem_v, layer):
    b, t, _ = q3.shape
    row = pl.BlockSpec((1, t, D_MODEL), lambda bi: (bi, 0, 0))
    mem = pl.BlockSpec((1, 1, N_MEM, N_XHEADS, XHEAD_DIM), lambda bi: (layer, bi, 0, 0, 0))
    return pl.pallas_call(
        _mem_attn_kernel,
        grid=(b,),
        in_specs=[row, mem, mem],
        out_specs=row,
        out_shape=jax.ShapeDtypeStruct((b, t, D_MODEL), F32),
        compiler_params=_cparams("parallel"),
        name="mem_attn",
    )(q3, mem_k, mem_v)


def _proj_res_kernel(x_ref, a_ref, w_ref, o_ref):
    o_ref[...] = x_ref[...] + _dot(a_ref[...].astype(BF16), w_ref[...])


def _proj_res(x, a, w, tm):
    m, k = a.shape
    n = w.shape[1]
    return pl.pallas_call(
        _proj_res_kernel,
        grid=(m // tm,),
        in_specs=[pl.BlockSpec((tm, n), lambda i: (i, 0)),
                  pl.BlockSpec((tm, k), lambda i: (i, 0)),
                  _resident((k, n))],
        out_specs=pl.BlockSpec((tm, n), lambda i: (i, 0)),
        out_shape=jax.ShapeDtypeStruct((m, n), F32),
        compiler_params=_cparams("parallel"),
        name="proj_res",
    )(x, a, w)


def _block_diag(w):
    n, c, d = w.shape
    eye = jnp.eye(n, dtype=w.dtype)
    return jnp.einsum("ncd,nm->ncmd", w, eye).reshape(n * c, n * d)


def kernel(x_prompt, x_sample, state_lru_h, state_lru_conv, state_hgrn, cache_swa_k, cache_swa_v, cache_mem_k, cache_mem_v, mem_prompt, n_ffn1, ffn1_wg, ffn1_wu, ffn1_wd, n_mix, w_in, lru_conv_w, lru_conv_b, lru_wa, lru_ba, lru_wx, lru_bx, lru_lambda, hgrn_lb, hgrn_norm, gn_a, gn_c, w_out, n_cross, x_wq, x_wk, x_wv, x_wo, n_ffn2, ffn2_wg, ffn2_wu, ffn2_wd, n_final):
    bp, tp, _ = x_prompt.shape
    bs, ts_new, _ = x_sample.shape
    w_buf = cache_swa_k.shape[2]
    assert ts_new <= SAMPLE_ROWS and tp % (BAND * max(DILATIONS)) == 0 and tp >= MAX_WINDOW
    tm_p = 512
    tm_s = bs * SAMPLE_ROWS

    xp = x_prompt.reshape(bp * tp, D_MODEL)
    xs = jnp.pad(x_sample, ((0, 0), (0, SAMPLE_ROWS - ts_new), (0, 0))).reshape(tm_s, D_MODEL)
    mem2 = mem_prompt.reshape(bp * N_MEM, D_MODEL)
    ones = jnp.ones((D_MODEL,), F32)

    outs = {name: [] for name in ("p_h", "p_c", "p_s", "p_k", "p_v", "p_mk", "p_mv", "s_h", "s_c", "s_s")}
    to_feature_major = lambda c: jnp.transpose(c, (0, 1, 3, 4, 2)).reshape(DEPTH, bs, D_ATT, w_buf)
    cache_kt, cache_vt = to_feature_major(cache_swa_k), to_feature_major(cache_swa_v)
    swa_out = None
    yp = ys = None
    for l in range(DEPTH):
        bf = lambda w: w[l].astype(BF16)
        wg1, wu1, wd1 = bf(ffn1_wg), bf(ffn1_wu), bf(ffn1_wd)
        wg2, wu2, wd2 = bf(ffn2_wg), bf(ffn2_wu), bf(ffn2_wd)
        win, wo_mix, wq, wo_x = bf(w_in), bf(w_out), bf(x_wq), bf(x_wo)
        wkv = jnp.concatenate([x_wk[l], x_wv[l]], axis=1).astype(BF16)
        lp = dict(conv_w=lru_conv_w[l], conv_b=lru_conv_b[l].reshape(1, D_LRU),
                  wa=_block_diag(lru_wa[l]).astype(BF16), ba=lru_ba[l].reshape(1, D_LRU),
                  wx=_block_diag(lru_wx[l]).astype(BF16), bx=lru_bx[l].reshape(1, D_LRU),
                  lam=lru_lambda[l].reshape(1, D_LRU), gn_a=gn_a[l].reshape(1, D_LRU))
        last = l == DEPTH - 1

        mkv = _proj(mem2, ones, wkv, tm=bp * N_MEM, norm=False).reshape(bp, N_MEM, 2 * D_MODEL)
        xp = _ffn(xp, n_ffn1[l], wg1, wu1, wd1, tm_p)
        z3 = _proj(xp, n_mix[l], win, tm_p).reshape(bp, tp, D_IN)
        ya, hl = _lru(z3, jnp.zeros((bp, SUBLANE, D_LRU), F32), jnp.zeros((bp, 1, D_LRU), F32), lp, tt=256)
        yb, st = _hgrn(z3, jnp.zeros((bp, D_HGRN, D_HGRN), F32), hgrn_lb, hgrn_norm[l], l,
                       c=HGRN_CHUNK, chunks=2, n_seq=bp)
        yc = _dil_attn(z3)
        xp = _wout(xp, ya.reshape(bp * tp, D_LRU), yb.reshape(bp * tp, D_HGRN),
                   [yc.reshape(bp * tp, D_ATT)], [], gn_c[l], wo_mix, tm_p)
        xp = _xattn(xp.reshape(bp, tp, D_MODEL), n_cross[l], wq, (mkv, mkv), 0, 1, wo_x, tm_p)
        xp = xp.reshape(bp * tp, D_MODEL)
        if last:
            xp, yp = _ffn(xp, n_ffn2[l], wg2, wu2, wd2, tm_p, g_final=n_final)
        else:
            xp = _ffn(xp, n_ffn2[l], wg2, wu2, wd2, tm_p)
        outs["p_h"].append(hl[:, SUBLANE - 1])
        outs["p_c"].append(z3[:, tp - (CONV_W - 1):, 0:D_LRU])
        outs["p_s"].append(_blockdiag_t_to_state(st))
        keep = min(MAX_WINDOW, tp)
        kcol = COL_KC * D_ATT
        vcol = COL_VC * D_ATT
        outs["p_k"].append(z3[:, tp - keep:, kcol:kcol + D_ATT].reshape(bp, keep, N_ATT, ATT_DH))
        outs["p_v"].append(z3[:, tp - keep:, vcol:vcol + D_ATT].reshape(bp, keep, N_ATT, ATT_DH))
        outs["p_mk"].append(mkv[:, :, 0:D_MODEL].reshape(bp, N_MEM, N_XHEADS, XHEAD_DIM))
        outs["p_mv"].append(mkv[:, :, D_MODEL:].reshape(bp, N_MEM, N_XHEADS, XHEAD_DIM))

        xs = _ffn(xs, n_ffn1[l], wg1, wu1, wd1, tm_s)
        zs3 = _proj(xs, n_mix[l], win, tm_s).reshape(bs, SAMPLE_ROWS, D_IN)
        conv_buf = jnp.pad(state_lru_conv[l], ((0, 0), (SUBLANE - (CONV_W - 1), 0), (0, 0)))
        ya, hl = _lru(zs3, conv_buf, state_lru_h[l].reshape(bs, 1, D_LRU), lp, tt=SAMPLE_ROWS, valid=ts_new)
        yb, st = _hgrn(zs3, _state_to_blockdiag_t(state_hgrn[l]), hgrn_lb, hgrn_norm[l], l,
                       c=SAMPLE_ROWS, chunks=1, n_seq=4, valid=ts_new)
        yc, ko, vo = _cache_attn(zs3, cache_kt, cache_vt, l, swa_out, ts_new)
        swa_out = (ko, vo)
        xs = _wout(xs, ya.reshape(tm_s, D_LRU), yb.reshape(tm_s, D_HGRN), [yc.reshape(tm_s, D_ATT)], [],
                   gn_c[l], wo_mix, tm_s)
        qs = _proj(xs, n_cross[l], wq, tm_s).reshape(bs, SAMPLE_ROWS, D_MODEL)
        xs = _proj_res(xs, _mem_attn(qs, cache_mem_k, cache_mem_v, l).reshape(tm_s, D_MODEL), wo_x, tm_s)
        if last:
            xs, ys = _ffn(xs, n_ffn2[l], wg2, wu2, wd2, tm_s, g_final=n_final)
        else:
            xs = _ffn(xs, n_ffn2[l], wg2, wu2, wd2, tm_s)
        outs["s_h"].append(hl[:, SUBLANE - 1])
        outs["s_c"].append(zs3[:, ts_new - (CONV_W - 1):ts_new, 0:D_LRU])
        outs["s_s"].append(_blockdiag_t_to_state(st))

    y_prompt = yp.reshape(bp, tp, D_MODEL)
    y_sample = ys.reshape(bs, SAMPLE_ROWS, D_MODEL)[:, :ts_new]
    stack = lambda name: jnp.stack(outs[name])
    from_feature_major = lambda c: jnp.transpose(c.reshape(DEPTH, bs, N_ATT, ATT_DH, w_buf), (0, 1, 4, 2, 3))
    return (y_prompt, y_sample,
            stack("p_h"), stack("p_c"), stack("p_s"), stack("p_k"), stack("p_v"), stack("p_mk"), stack("p_mv"),
            stack("s_h"), stack("s_c"), stack("s_s"), from_feature_major(swa_out[0]), from_feature_major(swa_out[1]))
```

```python
import functools

import numpy as np
import jax
import jax.numpy as jnp
from jax import lax
from jax.experimental import pallas as pl
from jax.experimental.pallas import tpu as pltpu

F32 = jnp.float32
BF16 = jnp.bfloat16

D_MODEL = 1024
DEPTH = 4
D_LRU = 256
N_LRU_BLOCKS = 4
CONV_W = 4
LRU_C = 8.0
HGRN_DK = 64
HGRN_DV = 64
D_HGRN = 256
N_HGRN = 4
HGRN_CHUNK = 64
ATT_DH = 64
D_ATT = 512
N_ATT = 8
DILATIONS = (1, 4, 16)
BAND = 128
SUPER = BAND * max(DILATIONS)
PAIR = 2 * ATT_DH
UNITS_IN_FLIGHT = 8
MAX_WINDOW = 2048
N_MEM = 256
N_XHEADS = 4
XHEAD_DIM = 256
D_FF = 2816
D_IN = 3072
EPS = 1e-6

COL_XA, COL_GA, COL_QB, COL_FB, COL_IB, COL_GB = 0, 1, 2, 3, 4, 5
COL_QC, COL_KC, COL_VC = 3, 4, 5
ATT_COLS_PER_ROW = D_IN // D_ATT

HALF_ATT = D_ATT // 2
SUBLANE = 8
LANE = 128
SAMPLE_ROWS = 8
VMEM_LIMIT = 56 * 1024 * 1024
NEG_BIG = -1e30

ALIBI_SLOPES = tuple(float(2.0 ** (-8.0 * (h + 1) / N_ATT)) for h in range(N_ATT))


def _log2(n):
    assert n & (n - 1) == 0
    return n.bit_length() - 1


def _cparams(*sem):
    return pltpu.CompilerParams(dimension_semantics=sem, vmem_limit_bytes=VMEM_LIMIT)


def _rms(x, g):
    ms = jnp.mean(x * x, axis=-1, keepdims=True)
    return x * lax.rsqrt(ms + EPS) * g


def _expm1(x):
    e = jnp.exp(x)
    near = (e - 1.0) * x / jnp.log(e)
    return jnp.where(e == 1.0, x, jnp.where(jnp.abs(x) > 1.0, e - 1.0, near))


def _dot(a, b):
    return jnp.dot(a, b, preferred_element_type=F32)


def _dot_nt(a, b):
    return lax.dot_general(a, b, (((1,), (1,)), ((), ())), preferred_element_type=F32)


def _dot_tn(a, b):
    return lax.dot_general(a, b, (((0,), (0,)), ((), ())), preferred_element_type=F32)


def _split3(x):
    hi = x.astype(BF16)
    r1 = x - hi.astype(F32)
    mid = r1.astype(BF16)
    lo = (r1 - mid.astype(F32)).astype(BF16)
    return hi, mid, lo


def _dot_exact_lhs(a_bf16, x):
    return sum(_dot(a_bf16, p) for p in _split3(x))


def _dot_exact_tn(x, a_bf16):
    return sum(_dot_tn(p, a_bf16) for p in _split3(x))


def _ffn_kernel(*refs, final_norm):
    if final_norm:
        x_ref, g_ref, wg_ref, wu_ref, wd_ref, gf_ref, o_ref, y_ref = refs
    else:
        x_ref, g_ref, wg_ref, wu_ref, wd_ref, o_ref = refs
    x = x_ref[...]
    h = _rms(x, g_ref[...]).astype(BF16)
    gate = _dot(h, wg_ref[...])
    up = _dot(h, wu_ref[...])
    act = (gate * jax.nn.sigmoid(gate) * up).astype(BF16)
    out = x + 0.5 * _dot(act, wd_ref[...])
    o_ref[...] = out
    if final_norm:
        y_ref[...] = _rms(out, gf_ref[...])


def _resident(shape):
    return pl.BlockSpec(shape, lambda *_: (0,) * len(shape), pipeline_mode=pl.Buffered(1))


def _ffn(x, g, wg, wu, wd, tm, g_final=None):
    m = x.shape[0]
    final_norm = g_final is not None
    row = pl.BlockSpec((tm, D_MODEL), lambda i: (i, 0))
    vec = _resident((1, D_MODEL))
    in_specs = [row, vec, _resident((D_MODEL, D_FF)), _resident((D_MODEL, D_FF)), _resident((D_FF, D_MODEL))]
    args = [x, g.reshape(1, D_MODEL), wg, wu, wd]
    out_shape = jax.ShapeDtypeStruct((m, D_MODEL), F32)
    out_specs = row
    if final_norm:
        in_specs.append(vec)
        args.append(g_final.reshape(1, D_MODEL))
        out_shape = (out_shape, out_shape)
        out_specs = (row, row)
    return pl.pallas_call(
        functools.partial(_ffn_kernel, final_norm=final_norm),
        grid=(m // tm,),
        in_specs=in_specs, out_specs=out_specs, out_shape=out_shape,
        compiler_params=_cparams("parallel"),
        name="ffn",
    )(*args)


def _proj_kernel(x_ref, g_ref, w_ref, o_ref, *, norm):
    x = x_ref[...]
    if norm:
        x = _rms(x, g_ref[...])
    o_ref[...] = _dot(x.astype(BF16), w_ref[...])


def _proj(x, g, w, tm, norm=True):
    m, k = x.shape
    n = w.shape[1]
    return pl.pallas_call(
        functools.partial(_proj_kernel, norm=norm),
        grid=(m // tm,),
        in_specs=[pl.BlockSpec((tm, k), lambda i: (i, 0)),
                  pl.BlockSpec((1, k), lambda i: (0, 0)),
                  _resident((k, n))],
        out_specs=pl.BlockSpec((tm, n), lambda i: (i, 0)),
        out_shape=jax.ShapeDtypeStruct((m, n), F32),
        compiler_params=_cparams("parallel"),
        name="proj",
    )(x, g.reshape(1, k), w)


def _lru_kernel(xa_ref, ga_ref, cb_ref, h0_ref, cw_ref, cbias_ref, wa_ref, ba_ref, wx_ref, bx_ref,
                lam_ref, gn_ref, ya_ref, hl_ref, xbuf, abuf, ubuf, hc, *, tt, pad, valid):
    ti = pl.program_id(1)

    @pl.when(ti == 0)
    def _():
        xbuf[0:SUBLANE, :] = cb_ref[0]
        hc[...] = jnp.broadcast_to(h0_ref[0], (SUBLANE, D_LRU))
        for s in range(2):
            abuf[s, 0:pad, :] = jnp.ones((pad, D_LRU), F32)
            ubuf[s, 0:pad, :] = jnp.zeros((pad, D_LRU), F32)

    x = xa_ref[0]
    xbuf[SUBLANE:SUBLANE + tt, :] = x
    y = cbias_ref[...] + cw_ref[CONV_W - 1:CONV_W, :] * x
    for tap in range(CONV_W - 1):
        back = CONV_W - 1 - tap
        y = y + cw_ref[tap:tap + 1, :] * xbuf[SUBLANE - back:SUBLANE - back + tt, :]
    xbuf[0:SUBLANE, :] = x[tt - SUBLANE:tt, :]

    yb = y.astype(BF16)
    r = jax.nn.sigmoid(_dot(yb, wa_ref[...]) + ba_ref[...])
    i = jax.nn.sigmoid(_dot(yb, wx_ref[...]) + bx_ref[...])
    lam = lam_ref[...]
    softplus_neg_lam = jnp.maximum(-lam, 0.0) + jnp.log1p(jnp.exp(-jnp.abs(lam)))
    log_a = -LRU_C * r * softplus_neg_lam
    a = jnp.exp(log_a)
    u = jnp.sqrt(-_expm1(2.0 * log_a)) * (i * y)
    if valid is not None:
        keep = lax.broadcasted_iota(jnp.int32, (tt, D_LRU), 0) < valid
        a = jnp.where(keep, a, 1.0)
        u = jnp.where(keep, u, 0.0)

    abuf[0, pad:pad + tt, :] = a
    ubuf[0, pad:pad + tt, :] = u
    cur, k = 0, 1
    while k < tt:
        a_c = abuf[cur, pad:pad + tt, :]
        u_c = ubuf[cur, pad:pad + tt, :]
        a_s = abuf[cur, pad - k:pad - k + tt, :]
        u_s = ubuf[cur, pad - k:pad - k + tt, :]
        abuf[1 - cur, pad:pad + tt, :] = a_c * a_s
        ubuf[1 - cur, pad:pad + tt, :] = a_c * u_s + u_c
        cur, k = 1 - cur, 2 * k
    h = abuf[cur, pad:pad + tt, :] * hc[SUBLANE - 1:SUBLANE, :] + ubuf[cur, pad:pad + tt, :]
    hc[...] = h[tt - SUBLANE:tt, :]
    hl_ref[0] = h[tt - SUBLANE:tt, :]
    ya_ref[0] = _rms(h * jax.nn.gelu(ga_ref[0]), gn_ref[...])


def _lru(z3, conv_buf, h0, lp, tt, valid=None):
    b, t, _ = z3.shape
    pad = max(tt // 2, SUBLANE)
    vec = pl.BlockSpec((1, D_LRU), lambda bi, ti: (0, 0))
    mat = pl.BlockSpec((D_LRU, D_LRU), lambda bi, ti: (0, 0))
    return pl.pallas_call(
        functools.partial(_lru_kernel, tt=tt, pad=pad, valid=valid),
        grid=(b, t // tt),
        in_specs=[pl.BlockSpec((1, tt, D_LRU), lambda bi, ti: (bi, ti, COL_XA)),
                  pl.BlockSpec((1, tt, D_LRU), lambda bi, ti: (bi, ti, COL_GA)),
                  pl.BlockSpec((1, SUBLANE, D_LRU), lambda bi, ti: (bi, 0, 0)),
                  pl.BlockSpec((1, 1, D_LRU), lambda bi, ti: (bi, 0, 0)),
                  pl.BlockSpec((CONV_W, D_LRU), lambda bi, ti: (0, 0)),
                  vec, mat, vec, mat, vec, vec, vec],
        out_specs=(pl.BlockSpec((1, tt, D_LRU), lambda bi, ti: (bi, ti, 0)),
                   pl.BlockSpec((1, SUBLANE, D_LRU), lambda bi, ti: (bi, 0, 0))),
        out_shape=(jax.ShapeDtypeStruct((b, t, D_LRU), F32),
                   jax.ShapeDtypeStruct((b, SUBLANE, D_LRU), F32)),
        scratch_shapes=[pltpu.VMEM((SUBLANE + tt, D_LRU), F32),
                        pltpu.VMEM((2, pad + tt, D_LRU), F32),
                        pltpu.VMEM((2, pad + tt, D_LRU), F32),
                        pltpu.VMEM((SUBLANE, D_LRU), F32)],
        compiler_params=_cparams("parallel", "arbitrary"),
        name="lru",
    )(z3, z3, conv_buf, h0, lp["conv_w"], lp["conv_b"], lp["wa"], lp["ba"], lp["wx"], lp["bx"],
      lp["lam"], lp["gn_a"])


@functools.lru_cache(maxsize=None)
def _hgrn_consts(c):
    t = np.arange(c)[:, None]
    j = np.arange(c)[None, :]
    blocks = [j <= t, j > t]
    masks = []
    w = c // 2
    while w >= 1:
        p = t % (2 * w)
        ref = t - p + w - 1
        blocks.append(np.where(p >= w, (j > ref) & (j <= t), (j > t) & (j <= ref)))
        masks.append((t // (2 * w) == j // (2 * w)) & (t % (2 * w) >= w) & (j % (2 * w) < w))
        w //= 2
    masks.append(t == j)
    dstack = np.concatenate(blocks, axis=0).astype(np.float32)
    masks = np.stack([np.tile(m, (1, N_HGRN)) for m in masks]).astype(np.float32)
    lane_head = np.arange(D_HGRN)[None, :] // HGRN_DK
    head_rows = np.repeat(np.arange(N_HGRN), c)[:, None]
    head_mask = (head_rows == lane_head).astype(np.float32)
    same_head = (np.arange(D_HGRN)[:, None] // HGRN_DV == lane_head).astype(np.float32)
    return dstack, masks, head_mask, same_head


def _hgrn_kernel(q_ref, f_ref, v_ref, g_ref, lbraw_ref, ng_ref, s0_ref, dst_ref, msk_ref, hm_ref, sh_ref,
                 y_ref, st_ref, st_scr, *, c, chunks, layer, valid):
    ti = pl.program_id(1)
    n_seq = q_ref.shape[0]

    @pl.when(ti == 0)
    def _():
        st_scr[...] = s0_ref[...]

    lbraw = lbraw_ref[...]
    e = jnp.exp(lbraw - jnp.max(lbraw, axis=0, keepdims=True))
    sm = e / jnp.sum(e, axis=0, keepdims=True)
    lrow = lax.broadcasted_iota(jnp.int32, sm.shape, 0)
    lb = jnp.sum(jnp.where((lrow >= 1) & (lrow <= layer), sm, 0.0), axis=0, keepdims=True)

    n_levels = msk_ref.shape[0] - 1
    hm = hm_ref[...]
    hm_b = hm.astype(BF16)
    same_head = sh_ref[...]
    shb = same_head.astype(BF16)

    def per_head_rows(x):
        if c % 16 == 0:
            return jnp.concatenate([x.astype(BF16)] * N_HGRN, axis=0) * hm_b
        return (jnp.concatenate([x] * N_HGRN, axis=0) * hm).astype(BF16)

    items = [(si, slice(ci * c, (ci + 1) * c)) for ci in range(chunks) for si in range(n_seq)]

    def prepare(item):
        si, rows = item
        f = lb + (1.0 - lb) * jax.nn.sigmoid(f_ref[si, rows, :])
        log_f = jnp.log(f)
        k = 1.0 - f
        if valid is not None:
            keep = lax.broadcasted_iota(jnp.int32, (c, D_HGRN), 0) < valid
            log_f = jnp.where(keep, log_f, 0.0)
            k = jnp.where(keep, k, 0.0)
        ex = _dot_exact_lhs(dst_ref[...], log_f)
        return q_ref[si, rows, :], k, v_ref[si, rows, :], ex

    prepared = [prepare(item) for item in items]

    scores = [jnp.zeros((c, N_HGRN * c), F32) for _ in items]
    for li in range(n_levels + 1):
        for n, (q, k, v, ex) in enumerate(prepared):
            if li < n_levels:
                decay = jnp.exp(ex[(2 + li) * c:(3 + li) * c])
                qt, kt = q * decay, k * decay
            else:
                qt, kt = q, k
            scores[n] = scores[n] + _dot_nt(qt.astype(BF16), per_head_rows(kt)) * msk_ref[li]
    intra = [_dot(sc.astype(BF16), per_head_rows(v)) for sc, (q, k, v, ex) in zip(scores, prepared)]

    outs = []
    for (si, rows), (q, k, v, ex), o in zip(items, prepared, intra):
        cum = ex[0:c]
        rem = ex[c:2 * c]
        st = st_scr[si]
        outs.append(o + _dot_nt((q * jnp.exp(cum)).astype(BF16), st.astype(BF16)))
        upd = _dot_tn(v.astype(BF16), (k * jnp.exp(rem)).astype(BF16))
        st_scr[si] = st * jnp.exp(cum[c - 1:c, :]) + upd * same_head

    for (si, rows), o in zip(items, outs):
        g = g_ref[si, rows, :]
        o2 = o * o
        o2_hi = o2.astype(BF16)
        o2_lo = (o2 - o2_hi.astype(F32)).astype(BF16)
        ms = (_dot(o2_hi, shb) + _dot(o2_lo, shb)) * (1.0 / HGRN_DV)
        y_ref[si, rows, :] = o * lax.rsqrt(ms + EPS) * ng_ref[...] * (g * jax.nn.sigmoid(g))
    st_ref[...] = st_scr[...]


def _hgrn(z3, s0t, lb_raw, norm_g, layer, c, chunks, n_seq, valid=None):
    b, t, _ = z3.shape
    tt = c * chunks
    dstack, masks, head_mask, same_head = _hgrn_consts(c)
    const2 = lambda bi, ti: (0, 0)
    col = lambda cidx: pl.BlockSpec((n_seq, tt, D_HGRN), lambda bi, ti: (bi, ti, cidx))
    state = pl.BlockSpec((n_seq, D_HGRN, D_HGRN), lambda bi, ti: (bi, 0, 0))
    return pl.pallas_call(
        functools.partial(_hgrn_kernel, c=c, chunks=chunks, layer=layer, valid=valid),
        grid=(b // n_seq, t // tt),
        in_specs=[col(COL_QB), col(COL_FB), col(COL_IB), col(COL_GB),
                  pl.BlockSpec((DEPTH, D_HGRN), const2),
                  pl.BlockSpec((1, D_HGRN), const2),
                  state,
                  pl.BlockSpec(dstack.shape, const2),
                  pl.BlockSpec(masks.shape, lambda bi, ti: (0, 0, 0)),
                  pl.BlockSpec(head_mask.shape, const2),
                  pl.BlockSpec(same_head.shape, const2)],
        out_specs=(pl.BlockSpec((n_seq, tt, D_HGRN), lambda bi, ti: (bi, ti, 0)), state),
        out_shape=(jax.ShapeDtypeStruct((b, t, D_HGRN), F32),
                   jax.ShapeDtypeStruct((b, D_HGRN, D_HGRN), F32)),
        scratch_shapes=[pltpu.VMEM((n_seq, D_HGRN, D_HGRN), F32)],
        compiler_params=_cparams("parallel", "arbitrary"),
        name="hgrn",
    )(z3, z3, z3, z3, lb_raw, jnp.tile(norm_g, N_HGRN).reshape(1, D_HGRN), s0t,
      jnp.asarray(dstack, BF16), jnp.asarray(masks), jnp.asarray(head_mask), jnp.asarray(same_head))


def _state_to_blockdiag_t(s):
    b = s.shape[0]
    eye = jnp.eye(N_HGRN, dtype=s.dtype)
    return jnp.einsum("bhkv,hg->bhvgk", s, eye).reshape(b, D_HGRN, D_HGRN)


def _blockdiag_t_to_state(st):
    b = st.shape[0]
    x = st.reshape(b, N_HGRN, HGRN_DV, N_HGRN, HGRN_DK)
    diag = jnp.stack([x[:, h, :, h, :] for h in range(N_HGRN)], axis=1)
    return jnp.swapaxes(diag, 2, 3)


def _strided_rows(start, dil):
    return pl.ds(start, BAND, stride=dil) if dil > 1 else pl.ds(start, BAND)


def _dil_attn_kernel(q_ref, kp_ref, kc_ref, vp_ref, vc_ref, slope_ref, o_ref, o_scr, d_scr, m_scr):
    first_block = pl.program_id(1) == 0
    n2 = 2 * BAND
    row = lax.broadcasted_iota(jnp.int32, (n2, BAND), 0)
    col = lax.broadcasted_iota(jnp.int32, (n2, BAND), 1)
    tq = row & (BAND - 1)
    upper = col > tq
    diag = col == tq
    steps = (tq - col + jnp.where(upper, BAND, 0)).astype(F32)
    slope = slope_ref[0]
    lane_first = lax.broadcasted_iota(jnp.int32, (BAND, PAIR), 1) < ATT_DH
    no_prev = jnp.where(first_block, NEG_BIG, 0.0)
    scale = ATT_DH ** -0.5

    ones = jnp.ones((BAND, PAIR), BF16)

    def scores(unit):
        g, cur, prev, prev_refs, b_s, b_x = unit
        kp_r, vp_r = prev_refs
        q = q_ref[0, cur, :] * scale
        q2 = jnp.concatenate([jnp.where(lane_first, q, 0.0), jnp.where(lane_first, 0.0, q)],
                             axis=0).astype(BF16)
        keys = jnp.concatenate([kp_r[0, prev, :], kc_ref[0, cur, :]], axis=0).astype(BF16)
        sps = _dot_nt(q2, keys)
        sp, sc = sps[:, 0:BAND], sps[:, BAND:n2]
        return jnp.where(upper, sp, sc) - b_s, sp - b_x

    def probs(sx):
        s, x = sx
        m = jnp.max(jnp.maximum(s, x), axis=1, keepdims=True)
        e = jnp.exp(s - m)
        ex = jnp.exp(x - m)
        return jnp.where(upper, e, ex).astype(BF16), jnp.where(upper, 0.0, e).astype(BF16), m

    def values(unit, pm):
        g, cur, prev, prev_refs, _, _ = unit
        kp_r, vp_r = prev_refs
        p_prev, p_cur, m = pm
        vp1 = jnp.concatenate([vp_r[0, prev, :].astype(BF16), ones], axis=1)
        vc1 = jnp.concatenate([vc_ref[0, cur, :].astype(BF16), ones], axis=1)
        pvd = _dot(p_prev, vp1) + _dot(p_cur, vc1)
        pv, den = pvd[:, 0:PAIR], pvd[:, PAIR:2 * PAIR]
        m_b = jnp.broadcast_to(m, (n2, PAIR))
        o_scr[g, cur, :] = jnp.where(lane_first, pv[0:BAND], pv[BAND:n2])
        d_scr[g, cur, :] = jnp.where(lane_first, den[0:BAND], den[BAND:n2])
        m_scr[g, cur, :] = jnp.where(lane_first, m_b[0:BAND], m_b[BAND:n2])

    units = []
    for g, dil in enumerate(DILATIONS):
        bias = slope * steps * float(dil)
        bias_x = jnp.where(diag, slope * float(BAND * dil), -NEG_BIG)
        bias_first = bias - jnp.where(upper, no_prev, 0.0)
        bias_x_first = bias_x - no_prev
        span = BAND * dil
        for j in range(SUPER // span):
            for r in range(dil):
                cur = _strided_rows(j * span + r, dil)
                if j == 0:
                    units.append((g, cur, _strided_rows(SUPER - span + r, dil), (kp_ref, vp_ref),
                                  bias_first, bias_x_first))
                else:
                    units.append((g, cur, _strided_rows((j - 1) * span + r, dil), (kc_ref, vc_ref),
                                  bias, bias_x))

    for i in range(0, len(units), UNITS_IN_FLIGHT):
        group = units[i:i + UNITS_IN_FLIGHT]
        sx = [scores(u) for u in group]
        pm = [probs(v) for v in sx]
        for u, v in zip(group, pm):
            values(u, v)

    rows_per_iter = 2 * BAND

    def mix(i, carry):
        rows = pl.ds(pl.multiple_of(i * rows_per_iter, rows_per_iter), rows_per_iter)
        ms = [m_scr[g, rows, :] for g in range(len(DILATIONS))]
        m = functools.reduce(jnp.maximum, ms)
        ws = [jnp.exp(mg - m) for mg in ms]
        num = sum(wgt * o_scr[g, rows, :] for g, wgt in enumerate(ws))
        den = sum(wgt * d_scr[g, rows, :] for g, wgt in enumerate(ws))
        o_ref[0, rows, :] = num / den
        return carry

    lax.fori_loop(0, SUPER // rows_per_iter, mix, 0)


def _dil_attn(z3):
    b, t, _ = z3.shape
    n_pairs = N_ATT // 2
    lanes_per_col = D_ATT // PAIR
    blk = lambda colblk, prev: pl.BlockSpec(
        (1, SUPER, PAIR),
        (lambda bi, sb, p: (bi, jnp.maximum(sb - 1, 0), colblk * lanes_per_col + p)) if prev
        else (lambda bi, sb, p: (bi, sb, colblk * lanes_per_col + p)))
    slopes = np.asarray(ALIBI_SLOPES, np.float32).reshape(n_pairs, 2, 1, 1)
    slope_tbl = np.broadcast_to(slopes, (n_pairs, 2, BAND, PAIR)).reshape(n_pairs, 2 * BAND, PAIR)
    return pl.pallas_call(
        _dil_attn_kernel,
        grid=(b, t // SUPER, n_pairs),
        in_specs=[blk(COL_QC, False), blk(COL_KC, True), blk(COL_KC, False),
                  blk(COL_VC, True), blk(COL_VC, False),
                  pl.BlockSpec((1, 2 * BAND, PAIR), lambda bi, sb, p: (p, 0, 0))],
        out_specs=pl.BlockSpec((1, SUPER, PAIR), lambda bi, sb, p: (bi, sb, p)),
        out_shape=jax.ShapeDtypeStruct((b, t, D_ATT), F32),
        scratch_shapes=[pltpu.VMEM((len(DILATIONS), SUPER, PAIR), F32)] * 3,
        compiler_params=_cparams("parallel", "parallel", "parallel"),
        name="dil_attn",
    )(z3, z3, z3, z3, z3, jnp.asarray(slope_tbl))


def _cache_attn_kernel(q_ref, kn_ref, vn_ref, kc_ref, vc_ref, slope_ref, *rest, n_new):
    o_ref, ko_ref, vo_ref = rest[-3:]
    w = kc_ref.shape[-1]
    kn = kn_ref[0]
    vn = vn_ref[0]
    kt = kc_ref[0, 0]
    vt = vc_ref[0, 0]

    sel_row = lax.broadcasted_iota(jnp.int32, (SAMPLE_ROWS, LANE), 0)
    sel_lane = lax.broadcasted_iota(jnp.int32, (SAMPLE_ROWS, LANE), 1)
    sel = jnp.where((sel_lane == LANE - n_new + sel_row) & (sel_row < n_new), 1.0, 0.0).astype(BF16)
    tail_new = lax.broadcasted_iota(jnp.int32, (HALF_ATT, LANE), 1) >= LANE - n_new

    def shift_in(cache_t, new_rows, out_ref):
        rolled = pltpu.roll(cache_t, w - n_new, axis=1)
        out_ref[0, 0, :, 0:w - LANE] = rolled[:, 0:w - LANE]
        out_ref[0, 0, :, w - LANE:w] = jnp.where(tail_new, _dot_exact_tn(new_rows, sel), rolled[:, w - LANE:w])

    shift_in(kt, kn, ko_ref)
    shift_in(vt, vn, vo_ref)

    heads = N_ATT // 2
    rows = heads * SAMPLE_ROWS
    lane_head = lax.broadcasted_iota(jnp.int32, (rows, HALF_ATT), 1) >> _log2(ATT_DH)
    row_head = lax.broadcasted_iota(jnp.int32, (rows, HALF_ATT), 0) >> _log2(SAMPLE_ROWS)
    own = lane_head == row_head
    q8 = jnp.where(own, jnp.concatenate([q_ref[0]] * heads, axis=0), 0.0).astype(BF16)
    scale = ATT_DH ** -0.5
    slope = slope_ref[0][:, 0:1]

    def weights(dist):
        mult = jnp.zeros(dist.shape, F32)
        for dil in DILATIONS:
            hit = (dist >= 0) & (dist <= BAND * dil) & ((dist & (dil - 1)) == 0)
            mult = mult + jnp.where(hit, 1.0, 0.0)
        return mult

    def mask(s, dist):
        mult = weights(dist)
        return jnp.where(mult > 0.0, s * scale - slope * dist.astype(F32), NEG_BIG), mult

    t_c = lax.broadcasted_iota(jnp.int32, (rows, w), 0) & (SAMPLE_ROWS - 1)
    pos_c = lax.broadcasted_iota(jnp.int32, (rows, w), 1)
    s_c, mult_c = mask(_dot(q8, kt.astype(BF16)), w + t_c - pos_c)
    t_n = lax.broadcasted_iota(jnp.int32, (rows, SAMPLE_ROWS), 0) & (SAMPLE_ROWS - 1)
    pos_n = lax.broadcasted_iota(jnp.int32, (rows, SAMPLE_ROWS), 1)
    s_n, mult_n = mask(_dot_nt(q8, kn.astype(BF16)), t_n - pos_n)

    m = jnp.maximum(jnp.max(s_c, axis=1, keepdims=True), jnp.max(s_n, axis=1, keepdims=True))
    e_c = jnp.exp(s_c - m) * mult_c
    e_n = jnp.exp(s_n - m) * mult_n
    den = jnp.sum(e_c, axis=1, keepdims=True) + jnp.sum(e_n, axis=1, keepdims=True)
    o8 = (_dot_nt(e_c.astype(BF16), vt.astype(BF16)) + _dot(e_n.astype(BF16), vn.astype(BF16))) / den
    o8 = jnp.where(own, o8, 0.0)
    o = o8[0:SAMPLE_ROWS]
    for hd in range(1, heads):
        o = o + o8[hd * SAMPLE_ROWS:(hd + 1) * SAMPLE_ROWS]
    o_ref[0] = o


def _cache_attn(z3, cache_kt, cache_vt, layer, out_prev, n_new):
    depth, b, _, w = cache_kt.shape
    halves = D_ATT // HALF_ATT
    cols = D_ATT // HALF_ATT
    new = lambda colblk: pl.BlockSpec((1, SAMPLE_ROWS, HALF_ATT), lambda bi, hf: (bi, 0, colblk * cols + hf))
    cache = pl.BlockSpec((1, 1, HALF_ATT, w), lambda bi, hf: (layer, bi, hf, 0))
    heads = N_ATT // halves
    slope_tbl = np.broadcast_to(np.repeat(np.asarray(ALIBI_SLOPES, np.float32), SAMPLE_ROWS)
                                .reshape(halves, heads * SAMPLE_ROWS, 1), (halves, heads * SAMPLE_ROWS, LANE))
    in_specs = [new(COL_QC), new(COL_KC), new(COL_VC), cache, cache,
                pl.BlockSpec((1, heads * SAMPLE_ROWS, LANE), lambda bi, hf: (hf, 0, 0))]
    args = [z3, z3, z3, cache_kt, cache_vt, jnp.asarray(slope_tbl)]
    aliases = {}
    if out_prev is not None:
        in_specs += [pl.BlockSpec(memory_space=pl.ANY)] * 2
        aliases = {len(args): 1, len(args) + 1: 2}
        args += list(out_prev)
    stacked = jax.ShapeDtypeStruct((depth, b, D_ATT, w), F32)
    return pl.pallas_call(
        functools.partial(_cache_attn_kernel, n_new=n_new),
        grid=(b, halves),
        in_specs=in_specs,
        out_specs=(pl.BlockSpec((1, SAMPLE_ROWS, HALF_ATT), lambda bi, hf: (bi, 0, hf)), cache, cache),
        out_shape=(jax.ShapeDtypeStruct((b, SAMPLE_ROWS, D_ATT), F32), stacked, stacked),
        input_output_aliases=aliases,
        compiler_params=_cparams("parallel", "parallel"),
        name="cache_attn",
    )(*args)


def _wout_kernel(*refs, n_pat):
    x_ref, ya_ref, yb_ref = refs[0:3]
    n_lse = n_pat if n_pat > 1 else 0
    o_refs = refs[3:3 + n_pat]
    l_refs = refs[3 + n_pat:3 + n_pat + n_lse]
    gn_ref, w_ref, out_ref = refs[3 + n_pat + n_lse:]
    if n_pat == 1:
        yc = o_refs[0][...]
    else:
        lses = [l[...] for l in l_refs]
        m = functools.reduce(jnp.maximum, lses)
        ws = [jnp.exp(l - m) for l in lses]
        yc = sum(wgt * o[...] for wgt, o in zip(ws, o_refs)) / sum(ws)
    ycn = _rms(yc, gn_ref[...])
    acc = _dot(ya_ref[...].astype(BF16), w_ref[0:D_LRU, :])
    acc = acc + _dot(yb_ref[...].astype(BF16), w_ref[D_LRU:D_LRU + D_HGRN, :])
    acc = acc + _dot(ycn.astype(BF16), w_ref[D_LRU + D_HGRN:D_MODEL, :])
    out_ref[...] = x_ref[...] + acc


def _wout(x, ya, yb, outs, lses, gn_c, w_out, tm):
    m = x.shape[0]
    n_pat = len(outs)
    lses = list(lses) if n_pat > 1 else []
    spec = lambda n: pl.BlockSpec((tm, n), lambda i: (i, 0))
    return pl.pallas_call(
        functools.partial(_wout_kernel, n_pat=n_pat),
        grid=(m // tm,),
        in_specs=[spec(D_MODEL), spec(D_LRU), spec(D_HGRN)] + [spec(D_ATT)] * (n_pat + len(lses))
        + [pl.BlockSpec((1, D_ATT), lambda i: (0, 0)), _resident((D_MODEL, D_MODEL))],
        out_specs=spec(D_MODEL),
        out_shape=jax.ShapeDtypeStruct((m, D_MODEL), F32),
        compiler_params=_cparams("parallel"),
        name="wout",
    )(x, ya, yb, *outs, *lses, gn_c.reshape(1, D_ATT), w_out)


def _xattn_kernel(x_ref, g_ref, wq_ref, mk_ref, mv_ref, wo_ref, o_ref):
    x = x_ref[0]
    q = _dot(_rms(x, g_ref[...]).astype(BF16), wq_ref[...])
    mk = mk_ref[0].astype(BF16)
    mv = mv_ref[0].astype(BF16)
    scale = XHEAD_DIM ** -0.5
    heads = [slice(hd * XHEAD_DIM, (hd + 1) * XHEAD_DIM) for hd in range(N_XHEADS)]
    ss = [_dot_nt(q[:, lanes].astype(BF16), mk[:, lanes]) * scale for lanes in heads]
    es = [jnp.exp(s - jnp.max(s, axis=1, keepdims=True)) for s in ss]
    ps = [(e / jnp.sum(e, axis=1, keepdims=True)).astype(BF16) for e in es]
    os_ = [_dot(p, mv[:, lanes]).astype(BF16) for p, lanes in zip(ps, heads)]
    acc = x
    for oh, lanes in zip(os_, heads):
        acc = acc + _dot(oh, wo_ref[lanes, :])
    o_ref[0] = acc


def _xattn(x3, g, wq, mem, k_col, v_col, wo, tm):
    b, t, _ = x3.shape
    const = lambda bi, ti: (0, 0)
    mat = _resident((D_MODEL, D_MODEL))
    mem_k, mem_v = mem
    return pl.pallas_call(
        _xattn_kernel,
        grid=(b, t // tm),
        in_specs=[pl.BlockSpec((1, tm, D_MODEL), lambda bi, ti: (bi, ti, 0)),
                  pl.BlockSpec((1, D_MODEL), const), mat,
                  pl.BlockSpec((1, N_MEM, D_MODEL), lambda bi, ti: (bi, 0, k_col)),
                  pl.BlockSpec((1, N_MEM, D_MODEL), lambda bi, ti: (bi, 0, v_col)),
                  mat],
        out_specs=pl.BlockSpec((1, tm, D_MODEL), lambda bi, ti: (bi, ti, 0)),
        out_shape=jax.ShapeDtypeStruct((b, t, D_MODEL), F32),
        compiler_params=_cparams("parallel", "parallel"),
        name="xattn",
    )(x3, g.reshape(1, D_MODEL), wq, mem_k, mem_v, wo)


def _mem_attn_kernel(q_ref, mk_ref, mv_ref, o_ref):
    t = q_ref.shape[1]
    rows = N_MEM * N_XHEADS
    k2 = mk_ref.at[0, 0].reshape(rows, XHEAD_DIM)[...].astype(BF16)
    v2 = mv_ref.at[0, 0].reshape(rows, XHEAD_DIM)[...].astype(BF16)
    q = q_ref[0]
    qs = jnp.concatenate([q[:, hd * XHEAD_DIM:(hd + 1) * XHEAD_DIM] for hd in range(N_XHEADS)], axis=0)
    row_head = lax.broadcasted_iota(jnp.int32, (N_XHEADS * t, rows), 0) >> _log2(t)
    col_head = lax.broadcasted_iota(jnp.int32, (N_XHEADS * t, rows), 1) & (N_XHEADS - 1)
    s = _dot_nt(qs.astype(BF16), k2) * (XHEAD_DIM ** -0.5)
    s = jnp.where(row_head == col_head, s, NEG_BIG)
    e = jnp.exp(s - jnp.max(s, axis=1, keepdims=True))
    p = e / jnp.sum(e, axis=1, keepdims=True)
    o = _dot(p.astype(BF16), v2)
    for hd in range(N_XHEADS):
        o_ref[0, :, hd * XHEAD_DIM:(hd + 1) * XHEAD_DIM] = o[hd * t:(hd + 1) * t]


def _mem_attn(q3, mem_k, mem_v, layer):
    b, t, _ = q3.shape
    row = pl.BlockSpec((1, t, D_MODEL), lambda bi: (bi, 0, 0))
    mem = pl.BlockSpec((1, 1, N_MEM, N_XHEADS, XHEAD_DIM), lambda bi: (layer, bi, 0, 0, 0))
    return pl.pallas_call(
        _mem_attn_kernel,
        grid=(b,),
        in_specs=[row, mem, mem],
        out_specs=row,
        out_shape=jax.ShapeDtypeStruct((b, t, D_MODEL), F32),
        compiler_params=_cparams("parallel"),
        name="mem_attn",
    )(q3, mem_k, mem_v)


def _proj_res_kernel(x_ref, a_ref, w_ref, o_ref):
    o_ref[...] = x_ref[...] + _dot(a_ref[...].astype(BF16), w_ref[...])


def _proj_res(x, a, w, tm):
    m, k = a.shape
    n = w.shape[1]
    return pl.pallas_call(
        _proj_res_kernel,
        grid=(m // tm,),
        in_specs=[pl.BlockSpec((tm, n), lambda i: (i, 0)),
                  pl.BlockSpec((tm, k), lambda i: (i, 0)),
                  _resident((k, n))],
        out_specs=pl.BlockSpec((tm, n), lambda i: (i, 0)),
        out_shape=jax.ShapeDtypeStruct((m, n), F32),
        compiler_params=_cparams("parallel"),
        name="proj_res",
    )(x, a, w)


def _block_diag(w):
    n, c, d = w.shape
    eye = jnp.eye(n, dtype=w.dtype)
    return jnp.einsum("ncd,nm->ncmd", w, eye).reshape(n * c, n * d)


def kernel(x_prompt, x_sample, state_lru_h, state_lru_conv, state_hgrn, cache_swa_k, cache_swa_v, cache_mem_k, cache_mem_v, mem_prompt, n_ffn1, ffn1_wg, ffn1_wu, ffn1_wd, n_mix, w_in, lru_conv_w, lru_conv_b, lru_wa, lru_ba, lru_wx, lru_bx, lru_lambda, hgrn_lb, hgrn_norm, gn_a, gn_c, w_out, n_cross, x_wq, x_wk, x_wv, x_wo, n_ffn2, ffn2_wg, ffn2_wu, ffn2_wd, n_final):
    bp, tp, _ = x_prompt.shape
    bs, ts_new, _ = x_sample.shape
    w_buf = cache_swa_k.shape[2]
    assert ts_new <= SAMPLE_ROWS and tp % (BAND * max(DILATIONS)) == 0 and tp >= MAX_WINDOW
    tm_p = 512
    tm_s = bs * SAMPLE_ROWS

    xp = x_prompt.reshape(bp * tp, D_MODEL)
    xs = jnp.pad(x_sample, ((0, 0), (0, SAMPLE_ROWS - ts_new), (0, 0))).reshape(tm_s, D_MODEL)
    mem2 = mem_prompt.reshape(bp * N_MEM, D_MODEL)
    ones = jnp.ones((D_MODEL,), F32)

    outs = {name: [] for name in ("p_h", "p_c", "p_s", "p_k", "p_v", "p_mk", "p_mv", "s_h", "s_c", "s_s")}
    to_feature_major = lambda c: jnp.transpose(c, (0, 1, 3, 4, 2)).reshape(DEPTH, bs, D_ATT, w_buf)
    cache_kt, cache_vt = to_feature_major(cache_swa_k), to_feature_major(cache_swa_v)
    swa_out = None
    yp = ys = None
    for l in range(DEPTH):
        bf = lambda w: w[l].astype(BF16)
        wg1, wu1, wd1 = bf(ffn1_wg), bf(ffn1_wu), bf(ffn1_wd)
        wg2, wu2, wd2 = bf(ffn2_wg), bf(ffn2_wu), bf(ffn2_wd)
        win, wo_mix, wq, wo_x = bf(w_in), bf(w_out), bf(x_wq), bf(x_wo)
        wkv = jnp.concatenate([x_wk[l], x_wv[l]], axis=1).astype(BF16)
        lp = dict(conv_w=lru_conv_w[l], conv_b=lru_conv_b[l].reshape(1, D_LRU),
                  wa=_block_diag(lru_wa[l]).astype(BF16), ba=lru_ba[l].reshape(1, D_LRU),
                  wx=_block_diag(lru_wx[l]).astype(BF16), bx=lru_bx[l].reshape(1, D_LRU),
                  lam=lru_lambda[l].reshape(1, D_LRU), gn_a=gn_a[l].reshape(1, D_LRU))
        last = l == DEPTH - 1

        mkv = _proj(mem2, ones, wkv, tm=bp * N_MEM, norm=False).reshape(bp, N_MEM, 2 * D_MODEL)
        xp = _ffn(xp, n_ffn1[l], wg1, wu1, wd1, tm_p)
        z3 = _proj(xp, n_mix[l], win, tm_p).reshape(bp, tp, D_IN)
        ya, hl = _lru(z3, jnp.zeros((bp, SUBLANE, D_LRU), F32), jnp.zeros((bp, 1, D_LRU), F32), lp, tt=256)
        yb, st = _hgrn(z3, jnp.zeros((bp, D_HGRN, D_HGRN), F32), hgrn_lb, hgrn_norm[l], l,
                       c=HGRN_CHUNK, chunks=2, n_seq=bp)
        yc = _dil_attn(z3)
        xp = _wout(xp, ya.reshape(bp * tp, D_LRU), yb.reshape(bp * tp, D_HGRN),
                   [yc.reshape(bp * tp, D_ATT)], [], gn_c[l], wo_mix, tm_p)
        xp = _xattn(xp.reshape(bp, tp, D_MODEL), n_cross[l], wq, (mkv, mkv), 0, 1, wo_x, tm_p)
        xp = xp.reshape(bp * tp, D_MODEL)
        if last:
            xp, yp = _ffn(xp, n_ffn2[l], wg2, wu2, wd2, tm_p, g_final=n_final)
        else:
            xp = _ffn(xp, n_ffn2[l], wg2, wu2, wd2, tm_p)
        outs["p_h"].append(hl[:, SUBLANE - 1])
        outs["p_c"].append(z3[:, tp - (CONV_W - 1):, 0:D_LRU])
        outs["p_s"].append(_blockdiag_t_to_state(st))
        keep = min(MAX_WINDOW, tp)
        kcol = COL_KC * D_ATT
        vcol = COL_VC * D_ATT
        outs["p_k"].append(z3[:, tp - keep:, kcol:kcol + D_ATT].reshape(bp, keep, N_ATT, ATT_DH))
        outs["p_v"].append(z3[:, tp - keep:, vcol:vcol + D_ATT].reshape(bp, keep, N_ATT, ATT_DH))
        outs["p_mk"].append(mkv[:, :, 0:D_MODEL].reshape(bp, N_MEM, N_XHEADS, XHEAD_DIM))
        outs["p_mv"].append(mkv[:, :, D_MODEL:].reshape(bp, N_MEM, N_XHEADS, XHEAD_DIM))

        xs = _ffn(xs, n_ffn1[l], wg1, wu1, wd1, tm_s)
        zs3 = _proj(xs, n_mix[l], win, tm_s).reshape(bs, SAMPLE_ROWS, D_IN)
        conv_buf = jnp.pad(state_lru_conv[l], ((0, 0), (SUBLANE - (CONV_W - 1), 0), (0, 0)))
        ya, hl = _lru(zs3, conv_buf, state_lru_h[l].reshape(bs, 1, D_LRU), lp, tt=SAMPLE_ROWS, valid=ts_new)
        yb, st = _hgrn(zs3, _state_to_blockdiag_t(state_hgrn[l]), hgrn_lb, hgrn_norm[l], l,
                       c=SAMPLE_ROWS, chunks=1, n_seq=4, valid=ts_new)
        yc, ko, vo = _cache_attn(zs3, cache_kt, cache_vt, l, swa_out, ts_new)
        swa_out = (ko, vo)
        xs = _wout(xs, ya.reshape(tm_s, D_LRU), yb.reshape(tm_s, D_HGRN), [yc.reshape(tm_s, D_ATT)], [],
                   gn_c[l], wo_mix, tm_s)
        qs = _proj(xs, n_cross[l], wq, tm_s).reshape(bs, SAMPLE_ROWS, D_MODEL)
        xs = _proj_res(xs, _mem_attn(qs, cache_mem_k, cache_mem_v, l).reshape(tm_s, D_MODEL), wo_x, tm_s)
        if last:
            xs, ys = _ffn(xs, n_ffn2[l], wg2, wu2, wd2, tm_s, g_final=n_final)
        else:
            xs = _ffn(xs, n_ffn2[l], wg2, wu2, wd2, tm_s)
        outs["s_h"].append(hl[:, SUBLANE - 1])
        outs["s_c"].append(zs3[:, ts_new - (CONV_W - 1):ts_new, 0:D_LRU])
        outs["s_s"].append(_blockdiag_t_to_state(st))

    y_prompt = yp.reshape(bp, tp, D_MODEL)
    y_sample = ys.reshape(bs, SAMPLE_ROWS, D_MODEL)[:, :ts_new]
    stack = lambda name: jnp.stack(outs[name])
    from_feature_major = lambda c: jnp.transpose(c.reshape(DEPTH, bs, N_ATT, ATT_DH, w_buf), (0, 1, 4, 2, 3))
    return (y_prompt, y_sample,
            stack("p_h"), stack("p_c"), stack("p_s"), stack("p_k"), stack("p_v"), stack("p_mk"), stack("p_mv"),
            stack("s_h"), stack("s_c"), stack("s_s"), from_feature_major(swa_out[0]), from_feature_major(swa_out[1]))
```

```python
import functools

import numpy as np
import jax
import jax.numpy as jnp
from jax import lax
from jax.experimental import pallas as pl
from jax.experimental.pallas import tpu as pltpu

F32 = jnp.float32
BF16 = jnp.bfloat16

D_MODEL = 1024
DEPTH = 4
D_LRU = 256
N_LRU_BLOCKS = 4
CONV_W = 4
LRU_C = 8.0
HGRN_DK = 64
HGRN_DV = 64
D_HGRN = 256
N_HGRN = 4
HGRN_CHUNK = 64
ATT_DH = 64
D_ATT = 512
N_ATT = 8
DILATIONS = (1, 4, 16)
BAND = 128
SUPER = BAND * max(DILATIONS)
PAIR = 2 * ATT_DH
UNITS_IN_FLIGHT = 4
MAX_WINDOW = 2048
N_MEM = 256
N_XHEADS = 4
XHEAD_DIM = 256
D_FF = 2816
D_IN = 3072
EPS = 1e-6

COL_XA, COL_GA, COL_QB, COL_FB, COL_IB, COL_GB = 0, 1, 2, 3, 4, 5
COL_QC, COL_KC, COL_VC = 3, 4, 5
ATT_COLS_PER_ROW = D_IN // D_ATT

HALF_ATT = D_ATT // 2
SUBLANE = 8
LANE = 128
SAMPLE_ROWS = 8
VMEM_LIMIT = 56 * 1024 * 1024
NEG_BIG = -1e30

ALIBI_SLOPES = tuple(float(2.0 ** (-8.0 * (h + 1) / N_ATT)) for h in range(N_ATT))


def _log2(n):
    assert n & (n - 1) == 0
    return n.bit_length() - 1


def _cparams(*sem):
    return pltpu.CompilerParams(dimension_semantics=sem, vmem_limit_bytes=VMEM_LIMIT)


def _rms(x, g):
    ms = jnp.mean(x * x, axis=-1, keepdims=True)
    return x * lax.rsqrt(ms + EPS) * g


def _expm1(x):
    e = jnp.exp(x)
    near = (e - 1.0) * x / jnp.log(e)
    return jnp.where(e == 1.0, x, jnp.where(jnp.abs(x) > 1.0, e - 1.0, near))


def _dot(a, b):
    return jnp.dot(a, b, preferred_element_type=F32)


def _dot_nt(a, b):
    return lax.dot_general(a, b, (((1,), (1,)), ((), ())), preferred_element_type=F32)


def _dot_tn(a, b):
    return lax.dot_general(a, b, (((0,), (0,)), ((), ())), preferred_element_type=F32)


def _split3(x):
    hi = x.astype(BF16)
    r1 = x - hi.astype(F32)
    mid = r1.astype(BF16)
    lo = (r1 - mid.astype(F32)).astype(BF16)
    return hi, mid, lo


def _dot_exact_lhs(a_bf16, x):
    return sum(_dot(a_bf16, p) for p in _split3(x))


def _dot_exact_tn(x, a_bf16):
    return sum(_dot_tn(p, a_bf16) for p in _split3(x))


def _ffn_kernel(*refs, final_norm):
    if final_norm:
        x_ref, g_ref, wg_ref, wu_ref, wd_ref, gf_ref, o_ref, y_ref = refs
    else:
        x_ref, g_ref, wg_ref, wu_ref, wd_ref, o_ref = refs
    x = x_ref[...]
    h = _rms(x, g_ref[...]).astype(BF16)
    gate = _dot(h, wg_ref[...])
    up = _dot(h, wu_ref[...])
    act = (gate * jax.nn.sigmoid(gate) * up).astype(BF16)
    out = x + 0.5 * _dot(act, wd_ref[...])
    o_ref[...] = out
    if final_norm:
        y_ref[...] = _rms(out, gf_ref[...])


def _resident(shape):
    return pl.BlockSpec(shape, lambda *_: (0,) * len(shape), pipeline_mode=pl.Buffered(1))


def _ffn(x, g, wg, wu, wd, tm, g_final=None):
    m = x.shape[0]
    final_norm = g_final is not None
    row = pl.BlockSpec((tm, D_MODEL), lambda i: (i, 0))
    vec = _resident((1, D_MODEL))
    in_specs = [row, vec, _resident((D_MODEL, D_FF)), _resident((D_MODEL, D_FF)), _resident((D_FF, D_MODEL))]
    args = [x, g.reshape(1, D_MODEL), wg, wu, wd]
    out_shape = jax.ShapeDtypeStruct((m, D_MODEL), F32)
    out_specs = row
    if final_norm:
        in_specs.append(vec)
        args.append(g_final.reshape(1, D_MODEL))
        out_shape = (out_shape, out_shape)
        out_specs = (row, row)
    return pl.pallas_call(
        functools.partial(_ffn_kernel, final_norm=final_norm),
        grid=(m // tm,),
        in_specs=in_specs, out_specs=out_specs, out_shape=out_shape,
        compiler_params=_cparams("parallel"),
        name="ffn",
    )(*args)


def _proj_kernel(x_ref, g_ref, w_ref, o_ref, *, norm):
    x = x_ref[...]
    if norm:
        x = _rms(x, g_ref[...])
    o_ref[...] = _dot(x.astype(BF16), w_ref[...])


def _proj(x, g, w, tm, norm=True):
    m, k = x.shape
    n = w.shape[1]
    return pl.pallas_call(
        functools.partial(_proj_kernel, norm=norm),
        grid=(m // tm,),
        in_specs=[pl.BlockSpec((tm, k), lambda i: (i, 0)),
                  pl.BlockSpec((1, k), lambda i: (0, 0)),
                  _resident((k, n))],
        out_specs=pl.BlockSpec((tm, n), lambda i: (i, 0)),
        out_shape=jax.ShapeDtypeStruct((m, n), F32),
        compiler_params=_cparams("parallel"),
        name="proj",
    )(x, g.reshape(1, k), w)


def _lru_kernel(xa_ref, ga_ref, cb_ref, h0_ref, cw_ref, cbias_ref, wa_ref, ba_ref, wx_ref, bx_ref,
                lam_ref, gn_ref, ya_ref, hl_ref, xbuf, abuf, ubuf, hc, *, tt, pad, valid):
    ti = pl.program_id(1)

    @pl.when(ti == 0)
    def _():
        xbuf[0:SUBLANE, :] = cb_ref[0]
        hc[...] = jnp.broadcast_to(h0_ref[0], (SUBLANE, D_LRU))
        for s in range(2):
            abuf[s, 0:pad, :] = jnp.ones((pad, D_LRU), F32)
            ubuf[s, 0:pad, :] = jnp.zeros((pad, D_LRU), F32)

    x = xa_ref[0]
    xbuf[SUBLANE:SUBLANE + tt, :] = x
    y = cbias_ref[...] + cw_ref[CONV_W - 1:CONV_W, :] * x
    for tap in range(CONV_W - 1):
        back = CONV_W - 1 - tap
        y = y + cw_ref[tap:tap + 1, :] * xbuf[SUBLANE - back:SUBLANE - back + tt, :]
    xbuf[0:SUBLANE, :] = x[tt - SUBLANE:tt, :]

    yb = y.astype(BF16)
    r = jax.nn.sigmoid(_dot(yb, wa_ref[...]) + ba_ref[...])
    i = jax.nn.sigmoid(_dot(yb, wx_ref[...]) + bx_ref[...])
    lam = lam_ref[...]
    softplus_neg_lam = jnp.maximum(-lam, 0.0) + jnp.log1p(jnp.exp(-jnp.abs(lam)))
    log_a = -LRU_C * r * softplus_neg_lam
    a = jnp.exp(log_a)
    u = jnp.sqrt(-_expm1(2.0 * log_a)) * (i * y)
    if valid is not None:
        keep = lax.broadcasted_iota(jnp.int32, (tt, D_LRU), 0) < valid
        a = jnp.where(keep, a, 1.0)
        u = jnp.where(keep, u, 0.0)

    abuf[0, pad:pad + tt, :] = a
    ubuf[0, pad:pad + tt, :] = u
    cur, k = 0, 1
    while k < tt:
        a_c = abuf[cur, pad:pad + tt, :]
        u_c = ubuf[cur, pad:pad + tt, :]
        a_s = abuf[cur, pad - k:pad - k + tt, :]
        u_s = ubuf[cur, pad - k:pad - k + tt, :]
        abuf[1 - cur, pad:pad + tt, :] = a_c * a_s
        ubuf[1 - cur, pad:pad + tt, :] = a_c * u_s + u_c
        cur, k = 1 - cur, 2 * k
    h = abuf[cur, pad:pad + tt, :] * hc[SUBLANE - 1:SUBLANE, :] + ubuf[cur, pad:pad + tt, :]
    hc[...] = h[tt - SUBLANE:tt, :]
    hl_ref[0] = h[tt - SUBLANE:tt, :]
    ya_ref[0] = _rms(h * jax.nn.gelu(ga_ref[0]), gn_ref[...])


def _lru(z3, conv_buf, h0, lp, tt, valid=None):
    b, t, _ = z3.shape
    pad = max(tt // 2, SUBLANE)
    vec = pl.BlockSpec((1, D_LRU), lambda bi, ti: (0, 0))
    mat = pl.BlockSpec((D_LRU, D_LRU), lambda bi, ti: (0, 0))
    return pl.pallas_call(
        functools.partial(_lru_kernel, tt=tt, pad=pad, valid=valid),
        grid=(b, t // tt),
        in_specs=[pl.BlockSpec((1, tt, D_LRU), lambda bi, ti: (bi, ti, COL_XA)),
                  pl.BlockSpec((1, tt, D_LRU), lambda bi, ti: (bi, ti, COL_GA)),
                  pl.BlockSpec((1, SUBLANE, D_LRU), lambda bi, ti: (bi, 0, 0)),
                  pl.BlockSpec((1, 1, D_LRU), lambda bi, ti: (bi, 0, 0)),
                  pl.BlockSpec((CONV_W, D_LRU), lambda bi, ti: (0, 0)),
                  vec, mat, vec, mat, vec, vec, vec],
        out_specs=(pl.BlockSpec((1, tt, D_LRU), lambda bi, ti: (bi, ti, 0)),
                   pl.BlockSpec((1, SUBLANE, D_LRU), lambda bi, ti: (bi, 0, 0))),
        out_shape=(jax.ShapeDtypeStruct((b, t, D_LRU), F32),
                   jax.ShapeDtypeStruct((b, SUBLANE, D_LRU), F32)),
        scratch_shapes=[pltpu.VMEM((SUBLANE + tt, D_LRU), F32),
                        pltpu.VMEM((2, pad + tt, D_LRU), F32),
                        pltpu.VMEM((2, pad + tt, D_LRU), F32),
                        pltpu.VMEM((SUBLANE, D_LRU), F32)],
        compiler_params=_cparams("parallel", "arbitrary"),
        name="lru",
    )(z3, z3, conv_buf, h0, lp["conv_w"], lp["conv_b"], lp["wa"], lp["ba"], lp["wx"], lp["bx"],
      lp["lam"], lp["gn_a"])


@functools.lru_cache(maxsize=None)
def _hgrn_consts(c):
    t = np.arange(c)[:, None]
    j = np.arange(c)[None, :]
    blocks = [j <= t, j > t]
    masks = []
    w = c // 2
    while w >= 1:
        p = t % (2 * w)
        ref = t - p + w - 1
        blocks.append(np.where(p >= w, (j > ref) & (j <= t), (j > t) & (j <= ref)))
        masks.append((t // (2 * w) == j // (2 * w)) & (t % (2 * w) >= w) & (j % (2 * w) < w))
        w //= 2
    masks.append(t == j)
    dstack = np.concatenate(blocks, axis=0).astype(np.float32)
    masks = np.stack([np.tile(m, (1, N_HGRN)) for m in masks]).astype(np.float32)
    lane_head = np.arange(D_HGRN)[None, :] // HGRN_DK
    head_rows = np.repeat(np.arange(N_HGRN), c)[:, None]
    head_mask = (head_rows == lane_head).astype(np.float32)
    same_head = (np.arange(D_HGRN)[:, None] // HGRN_DV == lane_head).astype(np.float32)
    return dstack, masks, head_mask, same_head


def _hgrn_kernel(q_ref, f_ref, v_ref, g_ref, lbraw_ref, ng_ref, s0_ref, dst_ref, msk_ref, hm_ref, sh_ref,
                 y_ref, st_ref, st_scr, *, c, chunks, layer, valid):
    ti = pl.program_id(1)
    n_seq = q_ref.shape[0]

    @pl.when(ti == 0)
    def _():
        st_scr[...] = s0_ref[...]

    lbraw = lbraw_ref[...]
    e = jnp.exp(lbraw - jnp.max(lbraw, axis=0, keepdims=True))
    sm = e / jnp.sum(e, axis=0, keepdims=True)
    lrow = lax.broadcasted_iota(jnp.int32, sm.shape, 0)
    lb = jnp.sum(jnp.where((lrow >= 1) & (lrow <= layer), sm, 0.0), axis=0, keepdims=True)

    n_levels = msk_ref.shape[0] - 1
    hm = hm_ref[...]
    hm_b = hm.astype(BF16)
    same_head = sh_ref[...]
    shb = same_head.astype(BF16)

    def per_head_rows(x):
        if c % 16 == 0:
            return jnp.concatenate([x.astype(BF16)] * N_HGRN, axis=0) * hm_b
        return (jnp.concatenate([x] * N_HGRN, axis=0) * hm).astype(BF16)

    items = [(si, slice(ci * c, (ci + 1) * c)) for ci in range(chunks) for si in range(n_seq)]

    def prepare(item):
        si, rows = item
        f = lb + (1.0 - lb) * jax.nn.sigmoid(f_ref[si, rows, :])
        log_f = jnp.log(f)
        k = 1.0 - f
        if valid is not None:
            keep = lax.broadcasted_iota(jnp.int32, (c, D_HGRN), 0) < valid
            log_f = jnp.where(keep, log_f, 0.0)
            k = jnp.where(keep, k, 0.0)
        ex = _dot_exact_lhs(dst_ref[...], log_f)
        return q_ref[si, rows, :], k, v_ref[si, rows, :], ex

    prepared = [prepare(item) for item in items]

    scores = [jnp.zeros((c, N_HGRN * c), F32) for _ in items]
    for li in range(n_levels + 1):
        for n, (q, k, v, ex) in enumerate(prepared):
            if li < n_levels:
                decay = jnp.exp(ex[(2 + li) * c:(3 + li) * c])
                qt, kt = q * decay, k * decay
            else:
                qt, kt = q, k
            scores[n] = scores[n] + _dot_nt(qt.astype(BF16), per_head_rows(kt)) * msk_ref[li]
    intra = [_dot(sc.astype(BF16), per_head_rows(v)) for sc, (q, k, v, ex) in zip(scores, prepared)]

    outs = []
    for (si, rows), (q, k, v, ex), o in zip(items, prepared, intra):
        cum = ex[0:c]
        rem = ex[c:2 * c]
        st = st_scr[si]
        outs.append(o + _dot_nt((q * jnp.exp(cum)).astype(BF16), st.astype(BF16)))
        upd = _dot_tn(v.astype(BF16), (k * jnp.exp(rem)).astype(BF16))
        st_scr[si] = st * jnp.exp(cum[c - 1:c, :]) + upd * same_head

    for (si, rows), o in zip(items, outs):
        g = g_ref[si, rows, :]
        o2 = o * o
        o2_hi = o2.astype(BF16)
        o2_lo = (o2 - o2_hi.astype(F32)).astype(BF16)
        ms = (_dot(o2_hi, shb) + _dot(o2_lo, shb)) * (1.0 / HGRN_DV)
        y_ref[si, rows, :] = o * lax.rsqrt(ms + EPS) * ng_ref[...] * (g * jax.nn.sigmoid(g))
    st_ref[...] = st_scr[...]


def _hgrn(z3, s0t, lb_raw, norm_g, layer, c, chunks, n_seq, valid=None):
    b, t, _ = z3.shape
    tt = c * chunks
    dstack, masks, head_mask, same_head = _hgrn_consts(c)
    const2 = lambda bi, ti: (0, 0)
    col = lambda cidx: pl.BlockSpec((n_seq, tt, D_HGRN), lambda bi, ti: (bi, ti, cidx))
    state = pl.BlockSpec((n_seq, D_HGRN, D_HGRN), lambda bi, ti: (bi, 0, 0))
    return pl.pallas_call(
        functools.partial(_hgrn_kernel, c=c, chunks=chunks, layer=layer, valid=valid),
        grid=(b // n_seq, t // tt),
        in_specs=[col(COL_QB), col(COL_FB), col(COL_IB), col(COL_GB),
                  pl.BlockSpec((DEPTH, D_HGRN), const2),
                  pl.BlockSpec((1, D_HGRN), const2),
                  state,
                  pl.BlockSpec(dstack.shape, const2),
                  pl.BlockSpec(masks.shape, lambda bi, ti: (0, 0, 0)),
                  pl.BlockSpec(head_mask.shape, const2),
                  pl.BlockSpec(same_head.shape, const2)],
        out_specs=(pl.BlockSpec((n_seq, tt, D_HGRN), lambda bi, ti: (bi, ti, 0)), state),
        out_shape=(jax.ShapeDtypeStruct((b, t, D_HGRN), F32),
                   jax.ShapeDtypeStruct((b, D_HGRN, D_HGRN), F32)),
        scratch_shapes=[pltpu.VMEM((n_seq, D_HGRN, D_HGRN), F32)],
        compiler_params=_cparams("parallel", "arbitrary"),
        name="hgrn",
    )(z3, z3, z3, z3, lb_raw, jnp.tile(norm_g, N_HGRN).reshape(1, D_HGRN), s0t,
      jnp.asarray(dstack, BF16), jnp.asarray(masks), jnp.asarray(head_mask), jnp.asarray(same_head))


def _state_to_blockdiag_t(s):
    b = s.shape[0]
    eye = jnp.eye(N_HGRN, dtype=s.dtype)
    return jnp.einsum("bhkv,hg->bhvgk", s, eye).reshape(b, D_HGRN, D_HGRN)


def _blockdiag_t_to_state(st):
    b = st.shape[0]
    x = st.reshape(b, N_HGRN, HGRN_DV, N_HGRN, HGRN_DK)
    diag = jnp.stack([x[:, h, :, h, :] for h in range(N_HGRN)], axis=1)
    return jnp.swapaxes(diag, 2, 3)


def _strided_rows(start, dil):
    return pl.ds(start, BAND, stride=dil) if dil > 1 else pl.ds(start, BAND)


def _dil_attn_kernel(q_ref, kp_ref, kc_ref, vp_ref, vc_ref, slope_ref, o_ref, o_scr, d_scr, m_scr):
    first_block = pl.program_id(1) == 0
    n2 = 2 * BAND
    row = lax.broadcasted_iota(jnp.int32, (n2, n2), 0)
    col = lax.broadcasted_iota(jnp.int32, (n2, n2), 1)
    back = (row & (BAND - 1)) - (col & (BAND - 1)) + jnp.where(col < BAND, BAND, 0)
    in_band = (back >= 0) & (back <= BAND)
    steps = back.astype(F32)
    slope = jnp.concatenate([slope_ref[0], slope_ref[0]], axis=1)
    lane_first = lax.broadcasted_iota(jnp.int32, (BAND, PAIR), 1) < ATT_DH
    scale = ATT_DH ** -0.5
    ones = jnp.ones((BAND, PAIR), BF16)

    def scores(unit):
        g, cur, prev, prev_refs, bias = unit
        kp_r, vp_r = prev_refs
        q = q_ref[0, cur, :] * scale
        q2 = jnp.concatenate([jnp.where(lane_first, q, 0.0), jnp.where(lane_first, 0.0, q)],
                             axis=0).astype(BF16)
        keys = jnp.concatenate([kp_r[0, prev, :], kc_ref[0, cur, :]], axis=0).astype(BF16)
        return _dot_nt(q2, keys) - bias

    def probs(s):
        m = jnp.max(s, axis=1, keepdims=True)
        return jnp.exp(s - m).astype(BF16), m

    def values(unit, pm):
        g, cur, prev, prev_refs, _ = unit
        kp_r, vp_r = prev_refs
        p, m = pm
        v1 = jnp.concatenate(
            [jnp.concatenate([vp_r[0, prev, :], vc_ref[0, cur, :]], axis=0).astype(BF16),
             jnp.concatenate([ones, ones], axis=0)], axis=1)
        pvd = _dot(p, v1)
        pv, den = pvd[:, 0:PAIR], pvd[:, PAIR:2 * PAIR]
        m_b = jnp.broadcast_to(m, (n2, PAIR))
        o_scr[g, cur, :] = jnp.where(lane_first, pv[0:BAND], pv[BAND:n2])
        d_scr[g, cur, :] = jnp.where(lane_first, den[0:BAND], den[BAND:n2])
        m_scr[g, cur, :] = jnp.where(lane_first, m_b[0:BAND], m_b[BAND:n2])

    units = []
    for g, dil in enumerate(DILATIONS):
        bias = jnp.where(in_band, slope * steps * float(dil), -NEG_BIG)
        bias_first = jnp.where((col < BAND) & first_block, -NEG_BIG, bias)
        span = BAND * dil
        for j in range(SUPER // span):
            for r in range(dil):
                cur = _strided_rows(j * span + r, dil)
                if j == 0:
                    units.append((g, cur, _strided_rows(SUPER - span + r, dil), (kp_ref, vp_ref), bias_first))
                else:
                    units.append((g, cur, _strided_rows((j - 1) * span + r, dil), (kc_ref, vc_ref), bias))

    for i in range(0, len(units), UNITS_IN_FLIGHT):
        group = units[i:i + UNITS_IN_FLIGHT]
        ss = [scores(u) for u in group]
        pm = [probs(s) for s in ss]
        for u, v in zip(group, pm):
            values(u, v)

    rows_per_iter = 2 * BAND

    def mix(i, carry):
        rows = pl.ds(pl.multiple_of(i * rows_per_iter, rows_per_iter), rows_per_iter)
        ms = [m_scr[g, rows, :] for g in range(len(DILATIONS))]
        m = functools.reduce(jnp.maximum, ms)
        ws = [jnp.exp(mg - m) for mg in ms]
        num = sum(wgt * o_scr[g, rows, :] for g, wgt in enumerate(ws))
        den = sum(wgt * d_scr[g, rows, :] for g, wgt in enumerate(ws))
        o_ref[0, rows, :] = num / den
        return carry

    lax.fori_loop(0, SUPER // rows_per_iter, mix, 0)


def _dil_attn(z3):
    b, t, _ = z3.shape
    n_pairs = N_ATT // 2
    lanes_per_col = D_ATT // PAIR
    blk = lambda colblk, prev: pl.BlockSpec(
        (1, SUPER, PAIR),
        (lambda bi, sb, p: (bi, jnp.maximum(sb - 1, 0), colblk * lanes_per_col + p)) if prev
        else (lambda bi, sb, p: (bi, sb, colblk * lanes_per_col + p)))
    slopes = np.asarray(ALIBI_SLOPES, np.float32).reshape(n_pairs, 2, 1, 1)
    slope_tbl = np.broadcast_to(slopes, (n_pairs, 2, BAND, PAIR)).reshape(n_pairs, 2 * BAND, PAIR)
    return pl.pallas_call(
        _dil_attn_kernel,
        grid=(b, t // SUPER, n_pairs),
        in_specs=[blk(COL_QC, False), blk(COL_KC, True), blk(COL_KC, False),
                  blk(COL_VC, True), blk(COL_VC, False),
                  pl.BlockSpec((1, 2 * BAND, PAIR), lambda bi, sb, p: (p, 0, 0))],
        out_specs=pl.BlockSpec((1, SUPER, PAIR), lambda bi, sb, p: (bi, sb, p)),
        out_shape=jax.ShapeDtypeStruct((b, t, D_ATT), F32),
        scratch_shapes=[pltpu.VMEM((len(DILATIONS), SUPER, PAIR), F32)] * 3,
        compiler_params=_cparams("parallel", "parallel", "parallel"),
        name="dil_attn",
    )(z3, z3, z3, z3, z3, jnp.asarray(slope_tbl))


def _cache_attn_kernel(q_ref, kn_ref, vn_ref, kc_ref, vc_ref, slope_ref, *rest, n_new):
    o_ref, ko_ref, vo_ref = rest[-3:]
    w = kc_ref.shape[-1]
    kn = kn_ref[0]
    vn = vn_ref[0]
    kt = kc_ref[0, 0]
    vt = vc_ref[0, 0]

    sel_row = lax.broadcasted_iota(jnp.int32, (SAMPLE_ROWS, LANE), 0)
    sel_lane = lax.broadcasted_iota(jnp.int32, (SAMPLE_ROWS, LANE), 1)
    sel = jnp.where((sel_lane == LANE - n_new + sel_row) & (sel_row < n_new), 1.0, 0.0).astype(BF16)
    tail_new = lax.broadcasted_iota(jnp.int32, (HALF_ATT, LANE), 1) >= LANE - n_new

    def shift_in(cache_t, new_rows, out_ref):
        rolled = pltpu.roll(cache_t, w - n_new, axis=1)
        out_ref[0, 0, :, 0:w - LANE] = rolled[:, 0:w - LANE]
        out_ref[0, 0, :, w - LANE:w] = jnp.where(tail_new, _dot_exact_tn(new_rows, sel), rolled[:, w - LANE:w])

    shift_in(kt, kn, ko_ref)
    shift_in(vt, vn, vo_ref)

    heads = N_ATT // 2
    rows = heads * SAMPLE_ROWS
    lane_head = lax.broadcasted_iota(jnp.int32, (rows, HALF_ATT), 1) >> _log2(ATT_DH)
    row_head = lax.broadcasted_iota(jnp.int32, (rows, HALF_ATT), 0) >> _log2(SAMPLE_ROWS)
    own = lane_head == row_head
    q8 = jnp.where(own, jnp.concatenate([q_ref[0]] * heads, axis=0), 0.0).astype(BF16)
    scale = ATT_DH ** -0.5
    slope = slope_ref[0][:, 0:1]

    def weights(dist):
        mult = jnp.zeros(dist.shape, F32)
        for dil in DILATIONS:
            hit = (dist >= 0) & (dist <= BAND * dil) & ((dist & (dil - 1)) == 0)
            mult = mult + jnp.where(hit, 1.0, 0.0)
        return mult

    def mask(s, dist):
        mult = weights(dist)
        return jnp.where(mult > 0.0, s * scale - slope * dist.astype(F32), NEG_BIG), mult

    t_c = lax.broadcasted_iota(jnp.int32, (rows, w), 0) & (SAMPLE_ROWS - 1)
    pos_c = lax.broadcasted_iota(jnp.int32, (rows, w), 1)
    s_c, mult_c = mask(_dot(q8, kt.astype(BF16)), w + t_c - pos_c)
    t_n = lax.broadcasted_iota(jnp.int32, (rows, SAMPLE_ROWS), 0) & (SAMPLE_ROWS - 1)
    pos_n = lax.broadcasted_iota(jnp.int32, (rows, SAMPLE_ROWS), 1)
    s_n, mult_n = mask(_dot_nt(q8, kn.astype(BF16)), t_n - pos_n)

    m = jnp.maximum(jnp.max(s_c, axis=1, keepdims=True), jnp.max(s_n, axis=1, keepdims=True))
    e_c = jnp.exp(s_c - m) * mult_c
    e_n = jnp.exp(s_n - m) * mult_n
    den = jnp.sum(e_c, axis=1, keepdims=True) + jnp.sum(e_n, axis=1, keepdims=True)
    o8 = (_dot_nt(e_c.astype(BF16), vt.astype(BF16)) + _dot(e_n.astype(BF16), vn.astype(BF16))) / den
    o8 = jnp.where(own, o8, 0.0)
    o = o8[0:SAMPLE_ROWS]
    for hd in range(1, heads):
        o = o + o8[hd * SAMPLE_ROWS:(hd + 1) * SAMPLE_ROWS]
    o_ref[0] = o


def _cache_attn(z3, cache_kt, cache_vt, layer, out_prev, n_new):
    depth, b, _, w = cache_kt.shape
    halves = D_ATT // HALF_ATT
    cols = D_ATT // HALF_ATT
    new = lambda colblk: pl.BlockSpec((1, SAMPLE_ROWS, HALF_ATT), lambda bi, hf: (bi, 0, colblk * cols + hf))
    cache = pl.BlockSpec((1, 1, HALF_ATT, w), lambda bi, hf: (layer, bi, hf, 0))
    heads = N_ATT // halves
    slope_tbl = np.broadcast_to(np.repeat(np.asarray(ALIBI_SLOPES, np.float32), SAMPLE_ROWS)
                                .reshape(halves, heads * SAMPLE_ROWS, 1), (halves, heads * SAMPLE_ROWS, LANE))
    in_specs = [new(COL_QC), new(COL_KC), new(COL_VC), cache, cache,
                pl.BlockSpec((1, heads * SAMPLE_ROWS, LANE), lambda bi, hf: (hf, 0, 0))]
    args = [z3, z3, z3, cache_kt, cache_vt, jnp.asarray(slope_tbl)]
    aliases = {}
    if out_prev is not None:
        in_specs += [pl.BlockSpec(memory_space=pl.ANY)] * 2
        aliases = {len(args): 1, len(args) + 1: 2}
        args += list(out_prev)
    stacked = jax.ShapeDtypeStruct((depth, b, D_ATT, w), F32)
    return pl.pallas_call(
        functools.partial(_cache_attn_kernel, n_new=n_new),
        grid=(b, halves),
        in_specs=in_specs,
        out_specs=(pl.BlockSpec((1, SAMPLE_ROWS, HALF_ATT), lambda bi, hf: (bi, 0, hf)), cache, cache),
        out_shape=(jax.ShapeDtypeStruct((b, SAMPLE_ROWS, D_ATT), F32), stacked, stacked),
        input_output_aliases=aliases,
        compiler_params=_cparams("parallel", "parallel"),
        name="cache_attn",
    )(*args)


def _wout_kernel(*refs, n_pat):
    x_ref, ya_ref, yb_ref = refs[0:3]
    n_lse = n_pat if n_pat > 1 else 0
    o_refs = refs[3:3 + n_pat]
    l_refs = refs[3 + n_pat:3 + n_pat + n_lse]
    gn_ref, w_ref, out_ref = refs[3 + n_pat + n_lse:]
    if n_pat == 1:
        yc = o_refs[0][...]
    else:
        lses = [l[...] for l in l_refs]
        m = functools.reduce(jnp.maximum, lses)
        ws = [jnp.exp(l - m) for l in lses]
        yc = sum(wgt * o[...] for wgt, o in zip(ws, o_refs)) / sum(ws)
    ycn = _rms(yc, gn_ref[...])
    acc = _dot(ya_ref[...].astype(BF16), w_ref[0:D_LRU, :])
    acc = acc + _dot(yb_ref[...].astype(BF16), w_ref[D_LRU:D_LRU + D_HGRN, :])
    acc = acc + _dot(ycn.astype(BF16), w_ref[D_LRU + D_HGRN:D_MODEL, :])
    out_ref[...] = x_ref[...] + acc


def _wout(x, ya, yb, outs, lses, gn_c, w_out, tm):
    m = x.shape[0]
    n_pat = len(outs)
    lses = list(lses) if n_pat > 1 else []
    spec = lambda n: pl.BlockSpec((tm, n), lambda i: (i, 0))
    return pl.pallas_call(
        functools.partial(_wout_kernel, n_pat=n_pat),
        grid=(m // tm,),
        in_specs=[spec(D_MODEL), spec(D_LRU), spec(D_HGRN)] + [spec(D_ATT)] * (n_pat + len(lses))
        + [pl.BlockSpec((1, D_ATT), lambda i: (0, 0)), _resident((D_MODEL, D_MODEL))],
        out_specs=spec(D_MODEL),
        out_shape=jax.ShapeDtypeStruct((m, D_MODEL), F32),
        compiler_params=_cparams("parallel"),
        name="wout",
    )(x, ya, yb, *outs, *lses, gn_c.reshape(1, D_ATT), w_out)


def _xattn_kernel(x_ref, g_ref, wq_ref, mk_ref, mv_ref, wo_ref, o_ref):
    x = x_ref[0]
    q = _dot(_rms(x, g_ref[...]).astype(BF16), wq_ref[...])
    mk = mk_ref[0].astype(BF16)
    mv = mv_ref[0].astype(BF16)
    scale = XHEAD_DIM ** -0.5
    heads = [slice(hd * XHEAD_DIM, (hd + 1) * XHEAD_DIM) for hd in range(N_XHEADS)]
    ss = [_dot_nt(q[:, lanes].astype(BF16), mk[:, lanes]) * scale for lanes in heads]
    es = [jnp.exp(s - jnp.max(s, axis=1, keepdims=True)) for s in ss]
    ps = [(e / jnp.sum(e, axis=1, keepdims=True)).astype(BF16) for e in es]
    os_ = [_dot(p, mv[:, lanes]).astype(BF16) for p, lanes in zip(ps, heads)]
    acc = x
    for oh, lanes in zip(os_, heads):
        acc = acc + _dot(oh, wo_ref[lanes, :])
    o_ref[0] = acc


def _xattn(x3, g, wq, mem, k_col, v_col, wo, tm):
    b, t, _ = x3.shape
    const = lambda bi, ti: (0, 0)
    mat = _resident((D_MODEL, D_MODEL))
    mem_k, mem_v = mem
    return pl.pallas_call(
        _xattn_kernel,
        grid=(b, t // tm),
        in_specs=[pl.BlockSpec((1, tm, D_MODEL), lambda bi, ti: (bi, ti, 0)),
                  pl.BlockSpec((1, D_MODEL), const), mat,
                  pl.BlockSpec((1, N_MEM, D_MODEL), lambda bi, ti: (bi, 0, k_col)),
                  pl.BlockSpec((1, N_MEM, D_MODEL), lambda bi, ti: (bi, 0, v_col)),
                  mat],
        out_specs=pl.BlockSpec((1, tm, D_MODEL), lambda bi, ti: (bi, ti, 0)),
        out_shape=jax.ShapeDtypeStruct((b, t, D_MODEL), F32),
        compiler_params=_cparams("parallel", "parallel"),
        name="xattn",
    )(x3, g.reshape(1, D_MODEL), wq, mem_k, mem_v, wo)


def _mem_attn_kernel(q_ref, mk_ref, mv_ref, o_ref):
    t = q_ref.shape[1]
    rows = N_MEM * N_XHEADS
    k2 = mk_ref.at[0, 0].reshape(rows, XHEAD_DIM)[...].astype(BF16)
    v2 = mv_ref.at[0, 0].reshape(rows, XHEAD_DIM)[...].astype(BF16)
    q = q_ref[0]
    qs = jnp.concatenate([q[:, hd * XHEAD_DIM:(hd + 1) * XHEAD_DIM] for hd in range(N_XHEADS)], axis=0)
    row_head = lax.broadcasted_iota(jnp.int32, (N_XHEADS * t, rows), 0) >> _log2(t)
    col_head = lax.broadcasted_iota(jnp.int32, (N_XHEADS * t, rows), 1) & (N_XHEADS - 1)
    s = _dot_nt(qs.astype(BF16), k2) * (XHEAD_DIM ** -0.5)
    s = jnp.where(row_head == col_head, s, NEG_BIG)
    e = jnp.exp(s - jnp.max(s, axis=1, keepdims=True))
    p = e / jnp.sum(e, axis=1, keepdims=True)
    o = _dot(p.astype(BF16), v2)
    for hd in range(N_XHEADS):
        o_ref[0, :, hd * XHEAD_DIM:(hd + 1) * XHEAD_DIM] = o[hd * t:(hd + 1) * t]


def _mem_attn(q3, mem_k, mem_v, layer):
    b, t, _ = q3.shape
    row = pl.BlockSpec((1, t, D_MODEL), lambda bi: (bi, 0, 0))
    mem = pl.BlockSpec((1, 1, N_MEM, N_XHEADS, XHEAD_DIM), lambda bi: (layer, bi, 0, 0, 0))
    return pl.pallas_call(
        _mem_attn_kernel,
        grid=(b,),
        in_specs=[row, mem, mem],
        out_specs=row,
        out_shape=jax.ShapeDtypeStruct((b, t, D_MODEL), F32),
        compiler_params=_cparams("parallel"),
        name="mem_attn",
    )(q3, mem_k, mem_v)


def _proj_res_kernel(x_ref, a_ref, w_ref, o_ref):
    o_ref[...] = x_ref[...] + _dot(a_ref[...].astype(BF16), w_ref[...])


def _proj_res(x, a, w, tm):
    m, k = a.shape
    n = w.shape[1]
    return pl.pallas_call(
        _proj_res_kernel,
        grid=(m // tm,),
        in_specs=[pl.BlockSpec((tm, n), lambda i: (i, 0)),
                  pl.BlockSpec((tm, k), lambda i: (i, 0)),
                  _resident((k, n))],
        out_specs=pl.BlockSpec((tm, n), lambda i: (i, 0)),
        out_shape=jax.ShapeDtypeStruct((m, n), F32),
        compiler_params=_cparams("parallel"),
        name="proj_res",
    )(x, a, w)


def _block_diag(w):
    n, c, d = w.shape
    eye = jnp.eye(n, dtype=w.dtype)
    return jnp.einsum("ncd,nm->ncmd", w, eye).reshape(n * c, n * d)


def kernel(x_prompt, x_sample, state_lru_h, state_lru_conv, state_hgrn, cache_swa_k, cache_swa_v, cache_mem_k, cache_mem_v, mem_prompt, n_ffn1, ffn1_wg, ffn1_wu, ffn1_wd, n_mix, w_in, lru_conv_w, lru_conv_b, lru_wa, lru_ba, lru_wx, lru_bx, lru_lambda, hgrn_lb, hgrn_norm, gn_a, gn_c, w_out, n_cross, x_wq, x_wk, x_wv, x_wo, n_ffn2, ffn2_wg, ffn2_wu, ffn2_wd, n_final):
    bp, tp, _ = x_prompt.shape
    bs, ts_new, _ = x_sample.shape
    w_buf = cache_swa_k.shape[2]
    assert ts_new <= SAMPLE_ROWS and tp % (BAND * max(DILATIONS)) == 0 and tp >= MAX_WINDOW
    tm_p = 512
    tm_s = bs * SAMPLE_ROWS

    xp = x_prompt.reshape(bp * tp, D_MODEL)
    xs = jnp.pad(x_sample, ((0, 0), (0, SAMPLE_ROWS - ts_new), (0, 0))).reshape(tm_s, D_MODEL)
    mem2 = mem_prompt.reshape(bp * N_MEM, D_MODEL)
    ones = jnp.ones((D_MODEL,), F32)

    outs = {name: [] for name in ("p_h", "p_c", "p_s", "p_k", "p_v", "p_mk", "p_mv", "s_h", "s_c", "s_s")}
    to_feature_major = lambda c: jnp.transpose(c, (0, 1, 3, 4, 2)).reshape(DEPTH, bs, D_ATT, w_buf)
    cache_kt, cache_vt = to_feature_major(cache_swa_k), to_feature_major(cache_swa_v)
    swa_out = None
    yp = ys = None
    for l in range(DEPTH):
        bf = lambda w: w[l].astype(BF16)
        wg1, wu1, wd1 = bf(ffn1_wg), bf(ffn1_wu), bf(ffn1_wd)
        wg2, wu2, wd2 = bf(ffn2_wg), bf(ffn2_wu), bf(ffn2_wd)
        win, wo_mix, wq, wo_x = bf(w_in), bf(w_out), bf(x_wq), bf(x_wo)
        wkv = jnp.concatenate([x_wk[l], x_wv[l]], axis=1).astype(BF16)
        lp = dict(conv_w=lru_conv_w[l], conv_b=lru_conv_b[l].reshape(1, D_LRU),
                  wa=_block_diag(lru_wa[l]).astype(BF16), ba=lru_ba[l].reshape(1, D_LRU),
                  wx=_block_diag(lru_wx[l]).astype(BF16), bx=lru_bx[l].reshape(1, D_LRU),
                  lam=lru_lambda[l].reshape(1, D_LRU), gn_a=gn_a[l].reshape(1, D_LRU))
        last = l == DEPTH - 1

        mkv = _proj(mem2, ones, wkv, tm=bp * N_MEM, norm=False).reshape(bp, N_MEM, 2 * D_MODEL)
        xp = _ffn(xp, n_ffn1[l], wg1, wu1, wd1, tm_p)
        z3 = _proj(xp, n_mix[l], win, tm_p).reshape(bp, tp, D_IN)
        ya, hl = _lru(z3, jnp.zeros((bp, SUBLANE, D_LRU), F32), jnp.zeros((bp, 1, D_LRU), F32), lp, tt=256)
        yb, st = _hgrn(z3, jnp.zeros((bp, D_HGRN, D_HGRN), F32), hgrn_lb, hgrn_norm[l], l,
                       c=HGRN_CHUNK, chunks=8, n_seq=bp)
        yc = _dil_attn(z3)
        xp = _wout(xp, ya.reshape(bp * tp, D_LRU), yb.reshape(bp * tp, D_HGRN),
                   [yc.reshape(bp * tp, D_ATT)], [], gn_c[l], wo_mix, tm_p)
        xp = _xattn(xp.reshape(bp, tp, D_MODEL), n_cross[l], wq, (mkv, mkv), 0, 1, wo_x, tm_p)
        xp = xp.reshape(bp * tp, D_MODEL)
        if last:
            xp, yp = _ffn(xp, n_ffn2[l], wg2, wu2, wd2, tm_p, g_final=n_final)
        else:
            xp = _ffn(xp, n_ffn2[l], wg2, wu2, wd2, tm_p)
        outs["p_h"].append(hl[:, SUBLANE - 1])
        outs["p_c"].append(z3[:, tp - (CONV_W - 1):, 0:D_LRU])
        outs["p_s"].append(_blockdiag_t_to_state(st))
        keep = min(MAX_WINDOW, tp)
        kcol = COL_KC * D_ATT
        vcol = COL_VC * D_ATT
        outs["p_k"].append(z3[:, tp - keep:, kcol:kcol + D_ATT].reshape(bp, keep, N_ATT, ATT_DH))
        outs["p_v"].append(z3[:, tp - keep:, vcol:vcol + D_ATT].reshape(bp, keep, N_ATT, ATT_DH))
        outs["p_mk"].append(mkv[:, :, 0:D_MODEL].reshape(bp, N_MEM, N_XHEADS, XHEAD_DIM))
        outs["p_mv"].append(mkv[:, :, D_MODEL:].reshape(bp, N_MEM, N_XHEADS, XHEAD_DIM))

        xs = _ffn(xs, n_ffn1[l], wg1, wu1, wd1, tm_s)
        zs3 = _proj(xs, n_mix[l], win, tm_s).reshape(bs, SAMPLE_ROWS, D_IN)
        conv_buf = jnp.pad(state_lru_conv[l], ((0, 0), (SUBLANE - (CONV_W - 1), 0), (0, 0)))
        ya, hl = _lru(zs3, conv_buf, state_lru_h[l].reshape(bs, 1, D_LRU), lp, tt=SAMPLE_ROWS, valid=ts_new)
        yb, st = _hgrn(zs3, _state_to_blockdiag_t(state_hgrn[l]), hgrn_lb, hgrn_norm[l], l,
                       c=SAMPLE_ROWS, chunks=1, n_seq=4, valid=ts_new)
        yc, ko, vo = _cache_attn(zs3, cache_kt, cache_vt, l, swa_out, ts_new)
        swa_out = (ko, vo)
        xs = _wout(xs, ya.reshape(tm_s, D_LRU), yb.reshape(tm_s, D_HGRN), [yc.reshape(tm_s, D_ATT)], [],
                   gn_c[l], wo_mix, tm_s)
        qs = _proj(xs, n_cross[l], wq, tm_s).reshape(bs, SAMPLE_ROWS, D_MODEL)
        xs = _proj_res(xs, _mem_attn(qs, cache_mem_k, cache_mem_v, l).reshape(tm_s, D_MODEL), wo_x, tm_s)
        if last:
            xs, ys = _ffn(xs, n_ffn2[l], wg2, wu2, wd2, tm_s, g_final=n_final)
        else:
            xs = _ffn(xs, n_ffn2[l], wg2, wu2, wd2, tm_s)
        outs["s_h"].append(hl[:, SUBLANE - 1])
        outs["s_c"].append(zs3[:, ts_new - (CONV_W - 1):ts_new, 0:D_LRU])
        outs["s_s"].append(_blockdiag_t_to_state(st))

    y_prompt = yp.reshape(bp, tp, D_MODEL)
    y_sample = ys.reshape(bs, SAMPLE_ROWS, D_MODEL)[:, :ts_new]
    stack = lambda name: jnp.stack(outs[name])
    from_feature_major = lambda c: jnp.transpose(c.reshape(DEPTH, bs, N_ATT, ATT_DH, w_buf), (0, 1, 4, 2, 3))
    return (y_prompt, y_sample,
            stack("p_h"), stack("p_c"), stack("p_s"), stack("p_k"), stack("p_v"), stack("p_mk"), stack("p_mv"),
            stack("s_h"), stack("s_c"), stack("s_s"), from_feature_major(swa_out[0]), from_feature_major(swa_out[1]))
```

```python
import functools

import numpy as np
import jax
import jax.numpy as jnp
from jax import lax
from jax.experimental import pallas as pl
from jax.experimental.pallas import tpu as pltpu

F32 = jnp.float32
BF16 = jnp.bfloat16

D_MODEL = 1024
DEPTH = 4
D_LRU = 256
N_LRU_BLOCKS = 4
CONV_W = 4
LRU_C = 8.0
HGRN_DK = 64
HGRN_DV = 64
D_HGRN = 256
N_HGRN = 4
HGRN_CHUNK = 64
SCAN_GROUP = 16
ATT_DH = 64
D_ATT = 512
N_ATT = 8
DILATIONS = (1, 4, 16)
BAND = 128
SUPER = BAND * max(DILATIONS)
PAIR = 2 * ATT_DH
UNITS_IN_FLIGHT = 2
MAX_WINDOW = 2048
N_MEM = 256
N_XHEADS = 4
XHEAD_DIM = 256
D_FF = 2816
D_IN = 3072
EPS = 1e-6

COL_XA, COL_GA, COL_QB, COL_FB, COL_IB, COL_GB = 0, 1, 2, 3, 4, 5
COL_QC, COL_KC, COL_VC = 3, 4, 5
ATT_COLS_PER_ROW = D_IN // D_ATT

HALF_ATT = D_ATT // 2
SUBLANE = 8
LANE = 128
SAMPLE_ROWS = 8
VMEM_LIMIT = 56 * 1024 * 1024
NEG_BIG = -1e30

ALIBI_SLOPES = tuple(float(2.0 ** (-8.0 * (h + 1) / N_ATT)) for h in range(N_ATT))


def _log2(n):
    assert n & (n - 1) == 0
    return n.bit_length() - 1


def _cparams(*sem):
    return pltpu.CompilerParams(dimension_semantics=sem, vmem_limit_bytes=VMEM_LIMIT)


def _rms(x, g):
    ms = jnp.mean(x * x, axis=-1, keepdims=True)
    return x * lax.rsqrt(ms + EPS) * g


def _expm1(x):
    e = jnp.exp(x)
    near = (e - 1.0) * x / jnp.log(e)
    return jnp.where(e == 1.0, x, jnp.where(jnp.abs(x) > 1.0, e - 1.0, near))


def _dot(a, b):
    return jnp.dot(a, b, preferred_element_type=F32)


def _dot_nt(a, b):
    return lax.dot_general(a, b, (((1,), (1,)), ((), ())), preferred_element_type=F32)


def _dot_tn(a, b):
    return lax.dot_general(a, b, (((0,), (0,)), ((), ())), preferred_element_type=F32)


def _split3(x):
    hi = x.astype(BF16)
    r1 = x - hi.astype(F32)
    mid = r1.astype(BF16)
    lo = (r1 - mid.astype(F32)).astype(BF16)
    return hi, mid, lo


def _dot_exact_lhs(a_bf16, x):
    return sum(_dot(a_bf16, p) for p in _split3(x))


def _dot_exact_tn(x, a_bf16):
    return sum(_dot_tn(p, a_bf16) for p in _split3(x))


def _ffn_kernel(*refs, final_norm):
    if final_norm:
        x_ref, g_ref, wg_ref, wu_ref, wd_ref, gf_ref, o_ref, y_ref = refs
    else:
        x_ref, g_ref, wg_ref, wu_ref, wd_ref, o_ref = refs
    x = x_ref[...]
    h = _rms(x, g_ref[...]).astype(BF16)
    gate = _dot(h, wg_ref[...])
    up = _dot(h, wu_ref[...])
    act = (gate * jax.nn.sigmoid(gate) * up).astype(BF16)
    out = x + 0.5 * _dot(act, wd_ref[...])
    o_ref[...] = out
    if final_norm:
        y_ref[...] = _rms(out, gf_ref[...])


def _resident(shape):
    return pl.BlockSpec(shape, lambda *_: (0,) * len(shape), pipeline_mode=pl.Buffered(1))


def _ffn(x, g, wg, wu, wd, tm, g_final=None):
    m = x.shape[0]
    final_norm = g_final is not None
    row = pl.BlockSpec((tm, D_MODEL), lambda i: (i, 0))
    vec = _resident((1, D_MODEL))
    in_specs = [row, vec, _resident((D_MODEL, D_FF)), _resident((D_MODEL, D_FF)), _resident((D_FF, D_MODEL))]
    args = [x, g.reshape(1, D_MODEL), wg, wu, wd]
    out_shape = jax.ShapeDtypeStruct((m, D_MODEL), F32)
    out_specs = row
    if final_norm:
        in_specs.append(vec)
        args.append(g_final.reshape(1, D_MODEL))
        out_shape = (out_shape, out_shape)
        out_specs = (row, row)
    return pl.pallas_call(
        functools.partial(_ffn_kernel, final_norm=final_norm),
        grid=(m // tm,),
        in_specs=in_specs, out_specs=out_specs, out_shape=out_shape,
        compiler_params=_cparams("parallel"),
        name="ffn",
    )(*args)


def _proj_kernel(x_ref, g_ref, w_ref, o_ref, *, norm):
    x = x_ref[...]
    if norm:
        x = _rms(x, g_ref[...])
    o_ref[...] = _dot(x.astype(BF16), w_ref[...])


def _proj(x, g, w, tm, norm=True):
    m, k = x.shape
    n = w.shape[1]
    return pl.pallas_call(
        functools.partial(_proj_kernel, norm=norm),
        grid=(m // tm,),
        in_specs=[pl.BlockSpec((tm, k), lambda i: (i, 0)),
                  pl.BlockSpec((1, k), lambda i: (0, 0)),
                  _resident((k, n))],
        out_specs=pl.BlockSpec((tm, n), lambda i: (i, 0)),
        out_shape=jax.ShapeDtypeStruct((m, n), F32),
        compiler_params=_cparams("parallel"),
        name="proj",
    )(x, g.reshape(1, k), w)


def _lru_kernel(xa_ref, ga_ref, cb_ref, h0_ref, cw_ref, cbias_ref, wa_ref, ba_ref, wx_ref, bx_ref,
                lam_ref, gn_ref, ya_ref, hl_ref, xbuf, abuf, ubuf, hc, *, tt, pad, valid):
    ti = pl.program_id(1)

    @pl.when(ti == 0)
    def _():
        xbuf[0:SUBLANE, :] = cb_ref[0]
        hc[...] = jnp.broadcast_to(h0_ref[0], (SUBLANE, D_LRU))
        for s in range(2):
            abuf[s, 0:pad, :] = jnp.ones((pad, D_LRU), F32)
            ubuf[s, 0:pad, :] = jnp.zeros((pad, D_LRU), F32)

    x = xa_ref[0]
    xbuf[SUBLANE:SUBLANE + tt, :] = x
    y = cbias_ref[...] + cw_ref[CONV_W - 1:CONV_W, :] * x
    for tap in range(CONV_W - 1):
        back = CONV_W - 1 - tap
        y = y + cw_ref[tap:tap + 1, :] * xbuf[SUBLANE - back:SUBLANE - back + tt, :]
    xbuf[0:SUBLANE, :] = x[tt - SUBLANE:tt, :]

    yb = y.astype(BF16)
    r = jax.nn.sigmoid(_dot(yb, wa_ref[...]) + ba_ref[...])
    i = jax.nn.sigmoid(_dot(yb, wx_ref[...]) + bx_ref[...])
    lam = lam_ref[...]
    softplus_neg_lam = jnp.maximum(-lam, 0.0) + jnp.log1p(jnp.exp(-jnp.abs(lam)))
    log_a = -LRU_C * r * softplus_neg_lam
    a = jnp.exp(log_a)
    u = jnp.sqrt(-_expm1(2.0 * log_a)) * (i * y)
    if valid is not None:
        keep = lax.broadcasted_iota(jnp.int32, (tt, D_LRU), 0) < valid
        a = jnp.where(keep, a, 1.0)
        u = jnp.where(keep, u, 0.0)

    abuf[0, pad:pad + tt, :] = a
    ubuf[0, pad:pad + tt, :] = u
    cur, k = 0, 1
    while k < min(tt, SCAN_GROUP):
        a_c = abuf[cur, pad:pad + tt, :]
        u_c = ubuf[cur, pad:pad + tt, :]
        a_s = abuf[cur, pad - k:pad - k + tt, :]
        u_s = ubuf[cur, pad - k:pad - k + tt, :]
        abuf[1 - cur, pad:pad + tt, :] = a_c * a_s
        ubuf[1 - cur, pad:pad + tt, :] = a_c * u_s + u_c
        cur, k = 1 - cur, 2 * k
    a_w = abuf[cur, pad:pad + tt, :]
    u_w = ubuf[cur, pad:pad + tt, :]
    h_before = hc[SUBLANE - 1:SUBLANE, :]
    groups = []
    for lo in range(0, tt, SCAN_GROUP):
        h_before = a_w[lo:lo + SCAN_GROUP] * h_before + u_w[lo:lo + SCAN_GROUP]
        groups.append(h_before)
    h = groups[0] if len(groups) == 1 else jnp.concatenate(groups, axis=0)
    hc[...] = h[tt - SUBLANE:tt, :]
    hl_ref[0] = h[tt - SUBLANE:tt, :]
    ya_ref[0] = _rms(h * jax.nn.gelu(ga_ref[0]), gn_ref[...])


def _lru(z3, conv_buf, h0, lp, tt, valid=None):
    b, t, _ = z3.shape
    pad = max(min(tt, SCAN_GROUP) // 2, SUBLANE)
    vec = pl.BlockSpec((1, D_LRU), lambda bi, ti: (0, 0))
    mat = pl.BlockSpec((D_LRU, D_LRU), lambda bi, ti: (0, 0))
    return pl.pallas_call(
        functools.partial(_lru_kernel, tt=tt, pad=pad, valid=valid),
        grid=(b, t // tt),
        in_specs=[pl.BlockSpec((1, tt, D_LRU), lambda bi, ti: (bi, ti, COL_XA)),
                  pl.BlockSpec((1, tt, D_LRU), lambda bi, ti: (bi, ti, COL_GA)),
                  pl.BlockSpec((1, SUBLANE, D_LRU), lambda bi, ti: (bi, 0, 0)),
                  pl.BlockSpec((1, 1, D_LRU), lambda bi, ti: (bi, 0, 0)),
                  pl.BlockSpec((CONV_W, D_LRU), lambda bi, ti: (0, 0)),
                  vec, mat, vec, mat, vec, vec, vec],
        out_specs=(pl.BlockSpec((1, tt, D_LRU), lambda bi, ti: (bi, ti, 0)),
                   pl.BlockSpec((1, SUBLANE, D_LRU), lambda bi, ti: (bi, 0, 0))),
        out_shape=(jax.ShapeDtypeStruct((b, t, D_LRU), F32),
                   jax.ShapeDtypeStruct((b, SUBLANE, D_LRU), F32)),
        scratch_shapes=[pltpu.VMEM((SUBLANE + tt, D_LRU), F32),
                        pltpu.VMEM((2, pad + tt, D_LRU), F32),
                        pltpu.VMEM((2, pad + tt, D_LRU), F32),
                        pltpu.VMEM((SUBLANE, D_LRU), F32)],
        compiler_params=_cparams("parallel", "arbitrary"),
        name="lru",
    )(z3, z3, conv_buf, h0, lp["conv_w"], lp["conv_b"], lp["wa"], lp["ba"], lp["wx"], lp["bx"],
      lp["lam"], lp["gn_a"])


@functools.lru_cache(maxsize=None)
def _hgrn_consts(c):
    t = np.arange(c)[:, None]
    j = np.arange(c)[None, :]
    blocks = [j <= t, j > t]
    masks = []
    w = c // 2
    while w >= 1:
        p = t % (2 * w)
        ref = t - p + w - 1
        blocks.append(np.where(p >= w, (j > ref) & (j <= t), (j > t) & (j <= ref)))
        masks.append((t // (2 * w) == j // (2 * w)) & (t % (2 * w) >= w) & (j % (2 * w) < w))
        w //= 2
    masks.append(t == j)
    dstack = np.concatenate(blocks, axis=0).astype(np.float32)
    masks = np.stack([np.tile(m, (1, N_HGRN)) for m in masks]).astype(np.float32)
    lane_head = np.arange(D_HGRN)[None, :] // HGRN_DK
    head_rows = np.repeat(np.arange(N_HGRN), c)[:, None]
    head_mask = (head_rows == lane_head).astype(np.float32)
    same_head = (np.arange(D_HGRN)[:, None] // HGRN_DV == lane_head).astype(np.float32)
    return dstack, masks, head_mask, same_head


def _hgrn_kernel(q_ref, f_ref, v_ref, g_ref, lbraw_ref, ng_ref, s0_ref, dst_ref, msk_ref, hm_ref, sh_ref,
                 y_ref, st_ref, st_scr, *, c, chunks, layer, valid):
    ti = pl.program_id(1)
    n_seq = q_ref.shape[0]

    @pl.when(ti == 0)
    def _():
        st_scr[...] = s0_ref[...]

    lbraw = lbraw_ref[...]
    e = jnp.exp(lbraw - jnp.max(lbraw, axis=0, keepdims=True))
    sm = e / jnp.sum(e, axis=0, keepdims=True)
    lrow = lax.broadcasted_iota(jnp.int32, sm.shape, 0)
    lb = jnp.sum(jnp.where((lrow >= 1) & (lrow <= layer), sm, 0.0), axis=0, keepdims=True)

    n_levels = msk_ref.shape[0] - 1
    hm = hm_ref[...]
    hm_b = hm.astype(BF16)
    same_head = sh_ref[...]
    shb = same_head.astype(BF16)

    def per_head_rows(x):
        if c % 16 == 0:
            return jnp.concatenate([x.astype(BF16)] * N_HGRN, axis=0) * hm_b
        return (jnp.concatenate([x] * N_HGRN, axis=0) * hm).astype(BF16)

    items = [(si, slice(ci * c, (ci + 1) * c)) for ci in range(chunks) for si in range(n_seq)]

    def prepare(item):
        si, rows = item
        f = lb + (1.0 - lb) * jax.nn.sigmoid(f_ref[si, rows, :])
        log_f = jnp.log(f)
        k = 1.0 - f
        if valid is not None:
            keep = lax.broadcasted_iota(jnp.int32, (c, D_HGRN), 0) < valid
            log_f = jnp.where(keep, log_f, 0.0)
            k = jnp.where(keep, k, 0.0)
        ex = _dot_exact_lhs(dst_ref[...], log_f)
        return q_ref[si, rows, :], k, v_ref[si, rows, :], ex

    prepared = [prepare(item) for item in items]

    scores = [jnp.zeros((c, N_HGRN * c), F32) for _ in items]
    for li in range(n_levels + 1):
        for n, (q, k, v, ex) in enumerate(prepared):
            if li < n_levels:
                decay = jnp.exp(ex[(2 + li) * c:(3 + li) * c])
                qt, kt = q * decay, k * decay
            else:
                qt, kt = q, k
            scores[n] = scores[n] + _dot_nt(qt.astype(BF16), per_head_rows(kt)) * msk_ref[li]
    intra = [_dot(sc.astype(BF16), per_head_rows(v)) for sc, (q, k, v, ex) in zip(scores, prepared)]

    outs = []
    for (si, rows), (q, k, v, ex), o in zip(items, prepared, intra):
        cum = ex[0:c]
        rem = ex[c:2 * c]
        st = st_scr[si]
        outs.append(o + _dot_nt((q * jnp.exp(cum)).astype(BF16), st.astype(BF16)))
        upd = _dot_tn(v.astype(BF16), (k * jnp.exp(rem)).astype(BF16))
        st_scr[si] = st * jnp.exp(cum[c - 1:c, :]) + upd * same_head

    for (si, rows), o in zip(items, outs):
        g = g_ref[si, rows, :]
        o2 = o * o
        o2_hi = o2.astype(BF16)
        o2_lo = (o2 - o2_hi.astype(F32)).astype(BF16)
        ms = (_dot(o2_hi, shb) + _dot(o2_lo, shb)) * (1.0 / HGRN_DV)
        y_ref[si, rows, :] = o * lax.rsqrt(ms + EPS) * ng_ref[...] * (g * jax.nn.sigmoid(g))
    st_ref[...] = st_scr[...]


def _hgrn(z3, s0t, lb_raw, norm_g, layer, c, chunks, n_seq, valid=None):
    b, t, _ = z3.shape
    tt = c * chunks
    dstack, masks, head_mask, same_head = _hgrn_consts(c)
    const2 = lambda bi, ti: (0, 0)
    col = lambda cidx: pl.BlockSpec((n_seq, tt, D_HGRN), lambda bi, ti: (bi, ti, cidx))
    state = pl.BlockSpec((n_seq, D_HGRN, D_HGRN), lambda bi, ti: (bi, 0, 0))
    return pl.pallas_call(
        functools.partial(_hgrn_kernel, c=c, chunks=chunks, layer=layer, valid=valid),
        grid=(b // n_seq, t // tt),
        in_specs=[col(COL_QB), col(COL_FB), col(COL_IB), col(COL_GB),
                  pl.BlockSpec((DEPTH, D_HGRN), const2),
                  pl.BlockSpec((1, D_HGRN), const2),
                  state,
                  pl.BlockSpec(dstack.shape, const2),
                  pl.BlockSpec(masks.shape, lambda bi, ti: (0, 0, 0)),
                  pl.BlockSpec(head_mask.shape, const2),
                  pl.BlockSpec(same_head.shape, const2)],
        out_specs=(pl.BlockSpec((n_seq, tt, D_HGRN), lambda bi, ti: (bi, ti, 0)), state),
        out_shape=(jax.ShapeDtypeStruct((b, t, D_HGRN), F32),
                   jax.ShapeDtypeStruct((b, D_HGRN, D_HGRN), F32)),
        scratch_shapes=[pltpu.VMEM((n_seq, D_HGRN, D_HGRN), F32)],
        compiler_params=_cparams("parallel", "arbitrary"),
        name="hgrn",
    )(z3, z3, z3, z3, lb_raw, jnp.tile(norm_g, N_HGRN).reshape(1, D_HGRN), s0t,
      jnp.asarray(dstack, BF16), jnp.asarray(masks), jnp.asarray(head_mask), jnp.asarray(same_head))


def _state_to_blockdiag_t(s):
    b = s.shape[0]
    eye = jnp.eye(N_HGRN, dtype=s.dtype)
    return jnp.einsum("bhkv,hg->bhvgk", s, eye).reshape(b, D_HGRN, D_HGRN)


def _blockdiag_t_to_state(st):
    b = st.shape[0]
    x = st.reshape(b, N_HGRN, HGRN_DV, N_HGRN, HGRN_DK)
    diag = jnp.stack([x[:, h, :, h, :] for h in range(N_HGRN)], axis=1)
    return jnp.swapaxes(diag, 2, 3)


def _strided_rows(start, dil):
    return pl.ds(start, BAND, stride=dil) if dil > 1 else pl.ds(start, BAND)


def _dil_attn_kernel(q_ref, kp_ref, kc_ref, vp_ref, vc_ref, slope_ref, o_ref, o_scr, d_scr, m_scr):
    first_block = pl.program_id(1) == 0
    n2 = 2 * BAND
    row = lax.broadcasted_iota(jnp.int32, (n2, n2), 0)
    col = lax.broadcasted_iota(jnp.int32, (n2, n2), 1)
    back = (row & (BAND - 1)) - (col & (BAND - 1)) + jnp.where(col < BAND, BAND, 0)
    in_band = (back >= 0) & (back <= BAND)
    steps = back.astype(F32)
    slope = jnp.concatenate([slope_ref[0], slope_ref[0]], axis=1)
    lane_first = lax.broadcasted_iota(jnp.int32, (BAND, PAIR), 1) < ATT_DH
    scale = ATT_DH ** -0.5
    ones = jnp.ones((BAND, PAIR), BF16)

    def scores(unit):
        g, cur, prev, prev_refs, bias = unit
        kp_r, vp_r = prev_refs
        q = q_ref[0, cur, :] * scale
        q2 = jnp.concatenate([jnp.where(lane_first, q, 0.0), jnp.where(lane_first, 0.0, q)],
                             axis=0).astype(BF16)
        keys = jnp.concatenate([kp_r[0, prev, :], kc_ref[0, cur, :]], axis=0).astype(BF16)
        return _dot_nt(q2, keys) - bias

    def probs(s):
        m = jnp.max(s, axis=1, keepdims=True)
        return jnp.exp(s - m).astype(BF16), m

    def values(unit, pm):
        g, cur, prev, prev_refs, _ = unit
        kp_r, vp_r = prev_refs
        p, m = pm
        v1 = jnp.concatenate(
            [jnp.concatenate([vp_r[0, prev, :], vc_ref[0, cur, :]], axis=0).astype(BF16),
             jnp.concatenate([ones, ones], axis=0)], axis=1)
        pvd = _dot(p, v1)
        pv, den = pvd[:, 0:PAIR], pvd[:, PAIR:2 * PAIR]
        m_b = jnp.broadcast_to(m, (n2, PAIR))
        o_scr[g, cur, :] = jnp.where(lane_first, pv[0:BAND], pv[BAND:n2])
        d_scr[g, cur, :] = jnp.where(lane_first, den[0:BAND], den[BAND:n2])
        m_scr[g, cur, :] = jnp.where(lane_first, m_b[0:BAND], m_b[BAND:n2])

    units = []
    for g, dil in enumerate(DILATIONS):
        bias = jnp.where(in_band, slope * steps * float(dil), -NEG_BIG)
        bias_first = jnp.where((col < BAND) & first_block, -NEG_BIG, bias)
        span = BAND * dil
        for j in range(SUPER // span):
            for r in range(dil):
                cur = _strided_rows(j * span + r, dil)
                if j == 0:
                    units.append((g, cur, _strided_rows(SUPER - span + r, dil), (kp_ref, vp_ref), bias_first))
                else:
                    units.append((g, cur, _strided_rows((j - 1) * span + r, dil), (kc_ref, vc_ref), bias))

    groups = [units[i:i + UNITS_IN_FLIGHT] for i in range(0, len(units), UNITS_IN_FLIGHT)]
    ss_next = [scores(u) for u in groups[0]]
    for gi, group in enumerate(groups):
        ss = ss_next
        if gi + 1 < len(groups):
            ss_next = [scores(u) for u in groups[gi + 1]]
        pm = [probs(s) for s in ss]
        for u, v in zip(group, pm):
            values(u, v)

    rows_per_iter = 2 * BAND

    def mix(i, carry):
        rows = pl.ds(pl.multiple_of(i * rows_per_iter, rows_per_iter), rows_per_iter)
        ms = [m_scr[g, rows, :] for g in range(len(DILATIONS))]
        m = functools.reduce(jnp.maximum, ms)
        ws = [jnp.exp(mg - m) for mg in ms]
        num = sum(wgt * o_scr[g, rows, :] for g, wgt in enumerate(ws))
        den = sum(wgt * d_scr[g, rows, :] for g, wgt in enumerate(ws))
        o_ref[0, rows, :] = num / den
        return carry

    lax.fori_loop(0, SUPER // rows_per_iter, mix, 0)


def _dil_attn(z3):
    b, t, _ = z3.shape
    n_pairs = N_ATT // 2
    lanes_per_col = D_ATT // PAIR
    blk = lambda colblk, prev: pl.BlockSpec(
        (1, SUPER, PAIR),
        (lambda bi, sb, p: (bi, jnp.maximum(sb - 1, 0), colblk * lanes_per_col + p)) if prev
        else (lambda bi, sb, p: (bi, sb, colblk * lanes_per_col + p)))
    slopes = np.asarray(ALIBI_SLOPES, np.float32).reshape(n_pairs, 2, 1, 1)
    slope_tbl = np.broadcast_to(slopes, (n_pairs, 2, BAND, PAIR)).reshape(n_pairs, 2 * BAND, PAIR)
    return pl.pallas_call(
        _dil_attn_kernel,
        grid=(b, t // SUPER, n_pairs),
        in_specs=[blk(COL_QC, False), blk(COL_KC, True), blk(COL_KC, False),
                  blk(COL_VC, True), blk(COL_VC, False),
                  pl.BlockSpec((1, 2 * BAND, PAIR), lambda bi, sb, p: (p, 0, 0))],
        out_specs=pl.BlockSpec((1, SUPER, PAIR), lambda bi, sb, p: (bi, sb, p)),
        out_shape=jax.ShapeDtypeStruct((b, t, D_ATT), F32),
        scratch_shapes=[pltpu.VMEM((len(DILATIONS), SUPER, PAIR), F32)] * 3,
        compiler_params=_cparams("parallel", "parallel", "parallel"),
        name="dil_attn",
    )(z3, z3, z3, z3, z3, jnp.asarray(slope_tbl))


def _cache_attn_kernel(q_ref, kn_ref, vn_ref, kc_ref, vc_ref, slope_ref, *rest, n_new):
    o_ref, ko_ref, vo_ref = rest[-3:]
    w = kc_ref.shape[-1]
    kn = kn_ref[0]
    vn = vn_ref[0]
    kt = kc_ref[0, 0]
    vt = vc_ref[0, 0]

    sel_row = lax.broadcasted_iota(jnp.int32, (SAMPLE_ROWS, LANE), 0)
    sel_lane = lax.broadcasted_iota(jnp.int32, (SAMPLE_ROWS, LANE), 1)
    sel = jnp.where((sel_lane == LANE - n_new + sel_row) & (sel_row < n_new), 1.0, 0.0).astype(BF16)
    tail_new = lax.broadcasted_iota(jnp.int32, (HALF_ATT, LANE), 1) >= LANE - n_new

    def shift_in(cache_t, new_rows, out_ref):
        rolled = pltpu.roll(cache_t, w - n_new, axis=1)
        out_ref[0, 0, :, 0:w - LANE] = rolled[:, 0:w - LANE]
        out_ref[0, 0, :, w - LANE:w] = jnp.where(tail_new, _dot_exact_tn(new_rows, sel), rolled[:, w - LANE:w])

    shift_in(kt, kn, ko_ref)
    shift_in(vt, vn, vo_ref)

    heads = N_ATT // 2
    rows = heads * SAMPLE_ROWS
    lane_head = lax.broadcasted_iota(jnp.int32, (rows, HALF_ATT), 1) >> _log2(ATT_DH)
    row_head = lax.broadcasted_iota(jnp.int32, (rows, HALF_ATT), 0) >> _log2(SAMPLE_ROWS)
    own = lane_head == row_head
    q8 = jnp.where(own, jnp.concatenate([q_ref[0]] * heads, axis=0), 0.0).astype(BF16)
    scale = ATT_DH ** -0.5
    slope = slope_ref[0][:, 0:1]

    def weights(dist):
        mult = jnp.zeros(dist.shape, F32)
        for dil in DILATIONS:
            hit = (dist >= 0) & (dist <= BAND * dil) & ((dist & (dil - 1)) == 0)
            mult = mult + jnp.where(hit, 1.0, 0.0)
        return mult

    def mask(s, dist):
        mult = weights(dist)
        return jnp.where(mult > 0.0, s * scale - slope * dist.astype(F32), NEG_BIG), mult

    t_c = lax.broadcasted_iota(jnp.int32, (rows, w), 0) & (SAMPLE_ROWS - 1)
    pos_c = lax.broadcasted_iota(jnp.int32, (rows, w), 1)
    s_c, mult_c = mask(_dot(q8, kt.astype(BF16)), w + t_c - pos_c)
    t_n = lax.broadcasted_iota(jnp.int32, (rows, SAMPLE_ROWS), 0) & (SAMPLE_ROWS - 1)
    pos_n = lax.broadcasted_iota(jnp.int32, (rows, SAMPLE_ROWS), 1)
    s_n, mult_n = mask(_dot_nt(q8, kn.astype(BF16)), t_n - pos_n)

    m = jnp.maximum(jnp.max(s_c, axis=1, keepdims=True), jnp.max(s_n, axis=1, keepdims=True))
    e_c = jnp.exp(s_c - m) * mult_c
    e_n = jnp.exp(s_n - m) * mult_n
    den = jnp.sum(e_c, axis=1, keepdims=True) + jnp.sum(e_n, axis=1, keepdims=True)
    o8 = (_dot_nt(e_c.astype(BF16), vt.astype(BF16)) + _dot(e_n.astype(BF16), vn.astype(BF16))) / den
    o8 = jnp.where(own, o8, 0.0)
    o = o8[0:SAMPLE_ROWS]
    for hd in range(1, heads):
        o = o + o8[hd * SAMPLE_ROWS:(hd + 1) * SAMPLE_ROWS]
    o_ref[0] = o


def _cache_attn(z3, cache_kt, cache_vt, layer, out_prev, n_new):
    depth, b, _, w = cache_kt.shape
    halves = D_ATT // HALF_ATT
    cols = D_ATT // HALF_ATT
    new = lambda colblk: pl.BlockSpec((1, SAMPLE_ROWS, HALF_ATT), lambda bi, hf: (bi, 0, colblk * cols + hf))
    cache = pl.BlockSpec((1, 1, HALF_ATT, w), lambda bi, hf: (layer, bi, hf, 0))
    heads = N_ATT // halves
    slope_tbl = np.broadcast_to(np.repeat(np.asarray(ALIBI_SLOPES, np.float32), SAMPLE_ROWS)
                                .reshape(halves, heads * SAMPLE_ROWS, 1), (halves, heads * SAMPLE_ROWS, LANE))
    in_specs = [new(COL_QC), new(COL_KC), new(COL_VC), cache, cache,
                pl.BlockSpec((1, heads * SAMPLE_ROWS, LANE), lambda bi, hf: (hf, 0, 0))]
    args = [z3, z3, z3, cache_kt, cache_vt, jnp.asarray(slope_tbl)]
    aliases = {}
    if out_prev is not None:
        in_specs += [pl.BlockSpec(memory_space=pl.ANY)] * 2
        aliases = {len(args): 1, len(args) + 1: 2}
        args += list(out_prev)
    stacked = jax.ShapeDtypeStruct((depth, b, D_ATT, w), F32)
    return pl.pallas_call(
        functools.partial(_cache_attn_kernel, n_new=n_new),
        grid=(b, halves),
        in_specs=in_specs,
        out_specs=(pl.BlockSpec((1, SAMPLE_ROWS, HALF_ATT), lambda bi, hf: (bi, 0, hf)), cache, cache),
        out_shape=(jax.ShapeDtypeStruct((b, SAMPLE_ROWS, D_ATT), F32), stacked, stacked),
        input_output_aliases=aliases,
        compiler_params=_cparams("parallel", "parallel"),
        name="cache_attn",
    )(*args)


def _mix_out(x, ya, yb, yc, gn, w_ref):
    acc = _dot(ya.astype(BF16), w_ref[0:D_LRU, :])
    acc = acc + _dot(yb.astype(BF16), w_ref[D_LRU:D_LRU + D_HGRN, :])
    acc = acc + _dot(_rms(yc, gn).astype(BF16), w_ref[D_LRU + D_HGRN:D_MODEL, :])
    return x + acc


def _wout_kernel(x_ref, ya_ref, yb_ref, yc_ref, gn_ref, w_ref, out_ref):
    out_ref[...] = _mix_out(x_ref[...], ya_ref[...], yb_ref[...], yc_ref[...], gn_ref[...], w_ref)


def _wout(x, ya, yb, yc, gn_c, w_out, tm):
    m = x.shape[0]
    spec = lambda n: pl.BlockSpec((tm, n), lambda i: (i, 0))
    return pl.pallas_call(
        _wout_kernel,
        grid=(m // tm,),
        in_specs=[spec(D_MODEL), spec(D_LRU), spec(D_HGRN), spec(D_ATT),
                  pl.BlockSpec((1, D_ATT), lambda i: (0, 0)), _resident((D_MODEL, D_MODEL))],
        out_specs=spec(D_MODEL),
        out_shape=jax.ShapeDtypeStruct((m, D_MODEL), F32),
        compiler_params=_cparams("parallel"),
        name="wout",
    )(x, ya, yb, yc, gn_c.reshape(1, D_ATT), w_out)


def _xattn_kernel(x_ref, ya_ref, yb_ref, yc_ref, gnc_ref, wmix_ref, g_ref, wq_ref, mk_ref, mv_ref, wo_ref, o_ref):
    x = _mix_out(x_ref[0], ya_ref[0], yb_ref[0], yc_ref[0], gnc_ref[...], wmix_ref)
    q = _dot(_rms(x, g_ref[...]).astype(BF16), wq_ref[...])
    mk = mk_ref[0].astype(BF16)
    mv = mv_ref[0].astype(BF16)
    scale = XHEAD_DIM ** -0.5
    heads = [slice(hd * XHEAD_DIM, (hd + 1) * XHEAD_DIM) for hd in range(N_XHEADS)]
    ss = [_dot_nt(q[:, lanes].astype(BF16), mk[:, lanes]) * scale for lanes in heads]
    es = [jnp.exp(s - jnp.max(s, axis=1, keepdims=True)) for s in ss]
    ps = [(e / jnp.sum(e, axis=1, keepdims=True)).astype(BF16) for e in es]
    os_ = [_dot(p, mv[:, lanes]).astype(BF16) for p, lanes in zip(ps, heads)]
    acc = x
    for oh, lanes in zip(os_, heads):
        acc = acc + _dot(oh, wo_ref[lanes, :])
    o_ref[0] = acc


def _xattn(x3, ya, yb, yc, gn_c, w_mix, g, wq, mem_kv, wo, tm):
    b, t, _ = x3.shape
    const = lambda bi, ti: (0, 0)
    mat = _resident((D_MODEL, D_MODEL))
    rows = lambda n: pl.BlockSpec((1, tm, n), lambda bi, ti: (bi, ti, 0))
    return pl.pallas_call(
        _xattn_kernel,
        grid=(b, t // tm),
        in_specs=[rows(D_MODEL), rows(D_LRU), rows(D_HGRN), rows(D_ATT),
                  pl.BlockSpec((1, D_ATT), const), mat,
                  pl.BlockSpec((1, D_MODEL), const), mat,
                  pl.BlockSpec((1, N_MEM, D_MODEL), lambda bi, ti: (bi, 0, 0)),
                  pl.BlockSpec((1, N_MEM, D_MODEL), lambda bi, ti: (bi, 0, 1)),
                  mat],
        out_specs=rows(D_MODEL),
        out_shape=jax.ShapeDtypeStruct((b, t, D_MODEL), F32),
        compiler_params=_cparams("parallel", "parallel"),
        name="xattn",
    )(x3, ya, yb, yc, gn_c.reshape(1, D_ATT), w_mix, g.reshape(1, D_MODEL), wq, mem_kv, mem_kv, wo)


def _mem_attn_kernel(q_ref, mk_ref, mv_ref, o_ref):
    t = q_ref.shape[1]
    rows = N_MEM * N_XHEADS
    k2 = mk_ref.at[0, 0].reshape(rows, XHEAD_DIM)[...].astype(BF16)
    v2 = mv_ref.at[0, 0].reshape(rows, XHEAD_DIM)[...].astype(BF16)
    q = q_ref[0]
    qs = jnp.concatenate([q[:, hd * XHEAD_DIM:(hd + 1) * XHEAD_DIM] for hd in range(N_XHEADS)], axis=0)
    row_head = lax.broadcasted_iota(jnp.int32, (N_XHEADS * t, rows), 0) >> _log2(t)
    col_head = lax.broadcasted_iota(jnp.int32, (N_XHEADS * t, rows), 1) & (N_XHEADS - 1)
    s = _dot_nt(qs.astype(BF16), k2) * (XHEAD_DIM ** -0.5)
    s = jnp.where(row_head == col_head, s, NEG_BIG)
    e = jnp.exp(s - jnp.max(s, axis=1, keepdims=True))
    p = e / jnp.sum(e, axis=1, keepdims=True)
    o = _dot(p.astype(BF16), v2)
    for hd in range(N_XHEADS):
        o_ref[0, :, hd * XHEAD_DIM:(hd + 1) * XHEAD_DIM] = o[hd * t:(hd + 1) * t]


def _mem_attn(q3, mem_k, mem_v, layer):
    b, t, _ = q3.shape
    row = pl.BlockSpec((1, t, D_MODEL), lambda bi: (bi, 0, 0))
    mem = pl.BlockSpec((1, 1, N_MEM, N_XHEADS, XHEAD_DIM), lambda bi: (layer, bi, 0, 0, 0))
    return pl.pallas_call(
        _mem_attn_kernel,
        grid=(b,),
        in_specs=[row, mem, mem],
        out_specs=row,
        out_shape=jax.ShapeDtypeStruct((b, t, D_MODEL), F32),
        compiler_params=_cparams("parallel"),
        name="mem_attn",
    )(q3, mem_k, mem_v)


def _proj_res_kernel(x_ref, a_ref, w_ref, o_ref):
    o_ref[...] = x_ref[...] + _dot(a_ref[...].astype(BF16), w_ref[...])


def _proj_res(x, a, w, tm):
    m, k = a.shape
    n = w.shape[1]
    return pl.pallas_call(
        _proj_res_kernel,
        grid=(m // tm,),
        in_specs=[pl.BlockSpec((tm, n), lambda i: (i, 0)),
                  pl.BlockSpec((tm, k), lambda i: (i, 0)),
                  _resident((k, n))],
        out_specs=pl.BlockSpec((tm, n), lambda i: (i, 0)),
        out_shape=jax.ShapeDtypeStruct((m, n), F32),
        compiler_params=_cparams("parallel"),
        name="proj_res",
    )(x, a, w)


def _block_diag(w):
    n, c, d = w.shape
    eye = jnp.eye(n, dtype=w.dtype)
    return jnp.einsum("ncd,nm->ncmd", w, eye).reshape(n * c, n * d)


def kernel(x_prompt, x_sample, state_lru_h, state_lru_conv, state_hgrn, cache_swa_k, cache_swa_v, cache_mem_k, cache_mem_v, mem_prompt, n_ffn1, ffn1_wg, ffn1_wu, ffn1_wd, n_mix, w_in, lru_conv_w, lru_conv_b, lru_wa, lru_ba, lru_wx, lru_bx, lru_lambda, hgrn_lb, hgrn_norm, gn_a, gn_c, w_out, n_cross, x_wq, x_wk, x_wv, x_wo, n_ffn2, ffn2_wg, ffn2_wu, ffn2_wd, n_final):
    bp, tp, _ = x_prompt.shape
    bs, ts_new, _ = x_sample.shape
    w_buf = cache_swa_k.shape[2]
    assert ts_new <= SAMPLE_ROWS and tp % (BAND * max(DILATIONS)) == 0 and tp >= MAX_WINDOW
    tm_p = 512
    tm_s = bs * SAMPLE_ROWS

    xp = x_prompt.reshape(bp * tp, D_MODEL)
    xs = jnp.pad(x_sample, ((0, 0), (0, SAMPLE_ROWS - ts_new), (0, 0))).reshape(tm_s, D_MODEL)
    mem2 = mem_prompt.reshape(bp * N_MEM, D_MODEL)
    ones = jnp.ones((D_MODEL,), F32)

    outs = {name: [] for name in ("p_h", "p_c", "p_s", "p_k", "p_v", "p_mk", "p_mv", "s_h", "s_c", "s_s")}
    to_feature_major = lambda c: jnp.transpose(c, (0, 1, 3, 4, 2)).reshape(DEPTH, bs, D_ATT, w_buf)
    cache_kt, cache_vt = to_feature_major(cache_swa_k), to_feature_major(cache_swa_v)
    swa_out = None
    yp = ys = None
    for l in range(DEPTH):
        bf = lambda w: w[l].astype(BF16)
        wg1, wu1, wd1 = bf(ffn1_wg), bf(ffn1_wu), bf(ffn1_wd)
        wg2, wu2, wd2 = bf(ffn2_wg), bf(ffn2_wu), bf(ffn2_wd)
        win, wo_mix, wq, wo_x = bf(w_in), bf(w_out), bf(x_wq), bf(x_wo)
        wkv = jnp.concatenate([x_wk[l], x_wv[l]], axis=1).astype(BF16)
        lp = dict(conv_w=lru_conv_w[l], conv_b=lru_conv_b[l].reshape(1, D_LRU),
                  wa=_block_diag(lru_wa[l]).astype(BF16), ba=lru_ba[l].reshape(1, D_LRU),
                  wx=_block_diag(lru_wx[l]).astype(BF16), bx=lru_bx[l].reshape(1, D_LRU),
                  lam=lru_lambda[l].reshape(1, D_LRU), gn_a=gn_a[l].reshape(1, D_LRU))
        last = l == DEPTH - 1

        mkv = _proj(mem2, ones, wkv, tm=bp * N_MEM, norm=False).reshape(bp, N_MEM, 2 * D_MODEL)
        xp = _ffn(xp, n_ffn1[l], wg1, wu1, wd1, tm_p)
        z3 = _proj(xp, n_mix[l], win, tm_p).reshape(bp, tp, D_IN)
        ya, hl = _lru(z3, jnp.zeros((bp, SUBLANE, D_LRU), F32), jnp.zeros((bp, 1, D_LRU), F32), lp, tt=256)
        yb, st = _hgrn(z3, jnp.zeros((bp, D_HGRN, D_HGRN), F32), hgrn_lb, hgrn_norm[l], l,
                       c=HGRN_CHUNK, chunks=8, n_seq=bp)
        yc = _dil_attn(z3)
        xp = _xattn(xp.reshape(bp, tp, D_MODEL), ya, yb, yc, gn_c[l], wo_mix, n_cross[l], wq, mkv, wo_x, tm_p)
        xp = xp.reshape(bp * tp, D_MODEL)
        if last:
            xp, yp = _ffn(xp, n_ffn2[l], wg2, wu2, wd2, tm_p, g_final=n_final)
        else:
            xp = _ffn(xp, n_ffn2[l], wg2, wu2, wd2, tm_p)
        outs["p_h"].append(hl[:, SUBLANE - 1])
        outs["p_c"].append(z3[:, tp - (CONV_W - 1):, 0:D_LRU])
        outs["p_s"].append(_blockdiag_t_to_state(st))
        keep = min(MAX_WINDOW, tp)
        kcol = COL_KC * D_ATT
        vcol = COL_VC * D_ATT
        outs["p_k"].append(z3[:, tp - keep:, kcol:kcol + D_ATT].reshape(bp, keep, N_ATT, ATT_DH))
        outs["p_v"].append(z3[:, tp - keep:, vcol:vcol + D_ATT].reshape(bp, keep, N_ATT, ATT_DH))
        outs["p_mk"].append(mkv[:, :, 0:D_MODEL].reshape(bp, N_MEM, N_XHEADS, XHEAD_DIM))
        outs["p_mv"].append(mkv[:, :, D_MODEL:].reshape(bp, N_MEM, N_XHEADS, XHEAD_DIM))

        xs = _ffn(xs, n_ffn1[l], wg1, wu1, wd1, tm_s)
        zs3 = _proj(xs, n_mix[l], win, tm_s).reshape(bs, SAMPLE_ROWS, D_IN)
        conv_buf = jnp.pad(state_lru_conv[l], ((0, 0), (SUBLANE - (CONV_W - 1), 0), (0, 0)))
        ya, hl = _lru(zs3, conv_buf, state_lru_h[l].reshape(bs, 1, D_LRU), lp, tt=SAMPLE_ROWS, valid=ts_new)
        yb, st = _hgrn(zs3, _state_to_blockdiag_t(state_hgrn[l]), hgrn_lb, hgrn_norm[l], l,
                       c=SAMPLE_ROWS, chunks=1, n_seq=4, valid=ts_new)
        yc, ko, vo = _cache_attn(zs3, cache_kt, cache_vt, l, swa_out, ts_new)
        swa_out = (ko, vo)
        xs = _wout(xs, ya.reshape(tm_s, D_LRU), yb.reshape(tm_s, D_HGRN), yc.reshape(tm_s, D_ATT),
                   gn_c[l], wo_mix, tm_s)
        qs = _proj(xs, n_cross[l], wq, tm_s).reshape(bs, SAMPLE_ROWS, D_MODEL)
        xs = _proj_res(xs, _mem_attn(qs, cache_mem_k, cache_mem_v, l).reshape(tm_s, D_MODEL), wo_x, tm_s)
        if last:
            xs, ys = _ffn(xs, n_ffn2[l], wg2, wu2, wd2, tm_s, g_final=n_final)
        else:
            xs = _ffn(xs, n_ffn2[l], wg2, wu2, wd2, tm_s)
        outs["s_h"].append(hl[:, SUBLANE - 1])
        outs["s_c"].append(zs3[:, ts_new - (CONV_W - 1):ts_new, 0:D_LRU])
        outs["s_s"].append(_blockdiag_t_to_state(st))

    y_prompt = yp.reshape(bp, tp, D_MODEL)
    y_sample = ys.reshape(bs, SAMPLE_ROWS, D_MODEL)[:, :ts_new]
    stack = lambda name: jnp.stack(outs[name])
    from_feature_major = lambda c: jnp.transpose(c.reshape(DEPTH, bs, N_ATT, ATT_DH, w_buf), (0, 1, 4, 2, 3))
    return (y_prompt, y_sample,
            stack("p_h"), stack("p_c"), stack("p_s"), stack("p_k"), stack("p_v"), stack("p_mk"), stack("p_mv"),
            stack("s_h"), stack("s_c"), stack("s_s"), from_feature_major(swa_out[0]), from_feature_major(swa_out[1]))
```

```python
import functools

import numpy as np
import jax
import jax.numpy as jnp
from jax import lax
from jax.experimental import pallas as pl
from jax.experimental.pallas import tpu as pltpu

F32 = jnp.float32
BF16 = jnp.bfloat16

D_MODEL = 1024
DEPTH = 4
D_LRU = 256
N_LRU_BLOCKS = 4
CONV_W = 4
LRU_C = 8.0
HGRN_DK = 64
HGRN_DV = 64
D_HGRN = 256
N_HGRN = 4
HGRN_CHUNK = 64
SCAN_GROUP = 16
ATT_DH = 64
D_ATT = 512
N_ATT = 8
DILATIONS = (1, 4, 16)
BAND = 128
SUPER = BAND * max(DILATIONS)
PAIR = 2 * ATT_DH
UNITS_IN_FLIGHT = 2
MAX_WINDOW = 2048
N_MEM = 256
N_XHEADS = 4
XHEAD_DIM = 256
D_FF = 2816
D_IN = 3072
EPS = 1e-6

COL_XA, COL_GA, COL_QB, COL_FB, COL_IB, COL_GB = 0, 1, 2, 3, 4, 5
COL_QC, COL_KC, COL_VC = 3, 4, 5
ATT_COLS_PER_ROW = D_IN // D_ATT

HALF_ATT = D_ATT // 2
SUBLANE = 8
LANE = 128
SAMPLE_ROWS = 8
VMEM_LIMIT = 56 * 1024 * 1024
NEG_BIG = -1e30

ALIBI_SLOPES = tuple(float(2.0 ** (-8.0 * (h + 1) / N_ATT)) for h in range(N_ATT))


def _log2(n):
    assert n & (n - 1) == 0
    return n.bit_length() - 1


def _cparams(*sem):
    return pltpu.CompilerParams(dimension_semantics=sem, vmem_limit_bytes=VMEM_LIMIT)


def _rms(x, g):
    ms = jnp.mean(x * x, axis=-1, keepdims=True)
    return x * lax.rsqrt(ms + EPS) * g


def _expm1(x):
    e = jnp.exp(x)
    near = (e - 1.0) * x / jnp.log(e)
    return jnp.where(e == 1.0, x, jnp.where(jnp.abs(x) > 1.0, e - 1.0, near))


def _dot(a, b):
    return jnp.dot(a, b, preferred_element_type=F32)


def _dot_nt(a, b):
    return lax.dot_general(a, b, (((1,), (1,)), ((), ())), preferred_element_type=F32)


def _dot_tn(a, b):
    return lax.dot_general(a, b, (((0,), (0,)), ((), ())), preferred_element_type=F32)


def _split3(x):
    hi = x.astype(BF16)
    r1 = x - hi.astype(F32)
    mid = r1.astype(BF16)
    lo = (r1 - mid.astype(F32)).astype(BF16)
    return hi, mid, lo


def _dot_exact_lhs(a_bf16, x):
    return sum(_dot(a_bf16, p) for p in _split3(x))


def _dot_exact_tn(x, a_bf16):
    return sum(_dot_tn(p, a_bf16) for p in _split3(x))


def _ffn_kernel(*refs, final_norm):
    if final_norm:
        x_ref, g_ref, wg_ref, wu_ref, wd_ref, gf_ref, o_ref, y_ref = refs
    else:
        x_ref, g_ref, wg_ref, wu_ref, wd_ref, o_ref = refs
    x = x_ref[...]
    h = _rms(x, g_ref[...]).astype(BF16)
    gate = _dot(h, wg_ref[...])
    up = _dot(h, wu_ref[...])
    act = (gate * jax.nn.sigmoid(gate) * up).astype(BF16)
    out = x + 0.5 * _dot(act, wd_ref[...])
    o_ref[...] = out
    if final_norm:
        y_ref[...] = _rms(out, gf_ref[...])


def _resident(shape):
    return pl.BlockSpec(shape, lambda *_: (0,) * len(shape), pipeline_mode=pl.Buffered(1))


def _ffn(x, g, wg, wu, wd, tm, g_final=None):
    m = x.shape[0]
    final_norm = g_final is not None
    row = pl.BlockSpec((tm, D_MODEL), lambda i: (i, 0))
    vec = _resident((1, D_MODEL))
    in_specs = [row, vec, _resident((D_MODEL, D_FF)), _resident((D_MODEL, D_FF)), _resident((D_FF, D_MODEL))]
    args = [x, g.reshape(1, D_MODEL), wg, wu, wd]
    out_shape = jax.ShapeDtypeStruct((m, D_MODEL), F32)
    out_specs = row
    if final_norm:
        in_specs.append(vec)
        args.append(g_final.reshape(1, D_MODEL))
        out_shape = (out_shape, out_shape)
        out_specs = (row, row)
    return pl.pallas_call(
        functools.partial(_ffn_kernel, final_norm=final_norm),
        grid=(m // tm,),
        in_specs=in_specs, out_specs=out_specs, out_shape=out_shape,
        compiler_params=_cparams("parallel"),
        name="ffn",
    )(*args)


def _proj_kernel(x_ref, g_ref, w_ref, o_ref, *, norm):
    x = x_ref[...]
    if norm:
        x = _rms(x, g_ref[...])
    o_ref[...] = _dot(x.astype(BF16), w_ref[...])


def _proj(x, g, w, tm, norm=True):
    m, k = x.shape
    n = w.shape[1]
    return pl.pallas_call(
        functools.partial(_proj_kernel, norm=norm),
        grid=(m // tm,),
        in_specs=[pl.BlockSpec((tm, k), lambda i: (i, 0)),
                  pl.BlockSpec((1, k), lambda i: (0, 0)),
                  _resident((k, n))],
        out_specs=pl.BlockSpec((tm, n), lambda i: (i, 0)),
        out_shape=jax.ShapeDtypeStruct((m, n), F32),
        compiler_params=_cparams("parallel"),
        name="proj",
    )(x, g.reshape(1, k), w)


def _lru_kernel(xa_ref, ga_ref, cb_ref, h0_ref, cw_ref, cbias_ref, wa_ref, ba_ref, wx_ref, bx_ref,
                lam_ref, gn_ref, ya_ref, hl_ref, xbuf, abuf, ubuf, hc, *, tt, pad, valid):
    ti = pl.program_id(1)

    @pl.when(ti == 0)
    def _():
        xbuf[0:SUBLANE, :] = cb_ref[0]
        hc[...] = jnp.broadcast_to(h0_ref[0], (SUBLANE, D_LRU))
        for s in range(2):
            abuf[s, 0:pad, :] = jnp.ones((pad, D_LRU), F32)
            ubuf[s, 0:pad, :] = jnp.zeros((pad, D_LRU), F32)

    x = xa_ref[0]
    xbuf[SUBLANE:SUBLANE + tt, :] = x
    y = cbias_ref[...] + cw_ref[CONV_W - 1:CONV_W, :] * x
    for tap in range(CONV_W - 1):
        back = CONV_W - 1 - tap
        y = y + cw_ref[tap:tap + 1, :] * xbuf[SUBLANE - back:SUBLANE - back + tt, :]
    xbuf[0:SUBLANE, :] = x[tt - SUBLANE:tt, :]

    yb = y.astype(BF16)
    r = jax.nn.sigmoid(_dot(yb, wa_ref[...]) + ba_ref[...])
    i = jax.nn.sigmoid(_dot(yb, wx_ref[...]) + bx_ref[...])
    lam = lam_ref[...]
    softplus_neg_lam = jnp.maximum(-lam, 0.0) + jnp.log1p(jnp.exp(-jnp.abs(lam)))
    log_a = -LRU_C * r * softplus_neg_lam
    a = jnp.exp(log_a)
    u = jnp.sqrt(-_expm1(2.0 * log_a)) * (i * y)
    if valid is not None:
        keep = lax.broadcasted_iota(jnp.int32, (tt, D_LRU), 0) < valid
        a = jnp.where(keep, a, 1.0)
        u = jnp.where(keep, u, 0.0)

    abuf[0, pad:pad + tt, :] = a
    ubuf[0, pad:pad + tt, :] = u
    cur, k = 0, 1
    while k < min(tt, SCAN_GROUP):
        a_c = abuf[cur, pad:pad + tt, :]
        u_c = ubuf[cur, pad:pad + tt, :]
        a_s = abuf[cur, pad - k:pad - k + tt, :]
        u_s = ubuf[cur, pad - k:pad - k + tt, :]
        abuf[1 - cur, pad:pad + tt, :] = a_c * a_s
        ubuf[1 - cur, pad:pad + tt, :] = a_c * u_s + u_c
        cur, k = 1 - cur, 2 * k
    a_w = abuf[cur, pad:pad + tt, :]
    u_w = ubuf[cur, pad:pad + tt, :]
    h_before = hc[SUBLANE - 1:SUBLANE, :]
    groups = []
    for lo in range(0, tt, SCAN_GROUP):
        h_before = a_w[lo:lo + SCAN_GROUP] * h_before + u_w[lo:lo + SCAN_GROUP]
        groups.append(h_before)
    h = groups[0] if len(groups) == 1 else jnp.concatenate(groups, axis=0)
    hc[...] = h[tt - SUBLANE:tt, :]
    hl_ref[0] = h[tt - SUBLANE:tt, :]
    ya_ref[0] = _rms(h * jax.nn.gelu(ga_ref[0]), gn_ref[...])


def _lru(z3, conv_buf, h0, lp, tt, valid=None):
    b, t, _ = z3.shape
    pad = max(min(tt, SCAN_GROUP) // 2, SUBLANE)
    vec = pl.BlockSpec((1, D_LRU), lambda bi, ti: (0, 0))
    mat = pl.BlockSpec((D_LRU, D_LRU), lambda bi, ti: (0, 0))
    return pl.pallas_call(
        functools.partial(_lru_kernel, tt=tt, pad=pad, valid=valid),
        grid=(b, t // tt),
        in_specs=[pl.BlockSpec((1, tt, D_LRU), lambda bi, ti: (bi, ti, COL_XA)),
                  pl.BlockSpec((1, tt, D_LRU), lambda bi, ti: (bi, ti, COL_GA)),
                  pl.BlockSpec((1, SUBLANE, D_LRU), lambda bi, ti: (bi, 0, 0)),
                  pl.BlockSpec((1, 1, D_LRU), lambda bi, ti: (bi, 0, 0)),
                  pl.BlockSpec((CONV_W, D_LRU), lambda bi, ti: (0, 0)),
                  vec, mat, vec, mat, vec, vec, vec],
        out_specs=(pl.BlockSpec((1, tt, D_LRU), lambda bi, ti: (bi, ti, 0)),
                   pl.BlockSpec((1, SUBLANE, D_LRU), lambda bi, ti: (bi, 0, 0))),
        out_shape=(jax.ShapeDtypeStruct((b, t, D_LRU), F32),
                   jax.ShapeDtypeStruct((b, SUBLANE, D_LRU), F32)),
        scratch_shapes=[pltpu.VMEM((SUBLANE + tt, D_LRU), F32),
                        pltpu.VMEM((2, pad + tt, D_LRU), F32),
                        pltpu.VMEM((2, pad + tt, D_LRU), F32),
                        pltpu.VMEM((SUBLANE, D_LRU), F32)],
        compiler_params=_cparams("parallel", "arbitrary"),
        name="lru",
    )(z3, z3, conv_buf, h0, lp["conv_w"], lp["conv_b"], lp["wa"], lp["ba"], lp["wx"], lp["bx"],
      lp["lam"], lp["gn_a"])


@functools.lru_cache(maxsize=None)
def _hgrn_consts(c):
    t = np.arange(c)[:, None]
    j = np.arange(c)[None, :]
    blocks = [j <= t, j > t]
    masks = []
    w = c // 2
    while w >= 1:
        p = t % (2 * w)
        ref = t - p + w - 1
        blocks.append(np.where(p >= w, (j > ref) & (j <= t), (j > t) & (j <= ref)))
        masks.append((t // (2 * w) == j // (2 * w)) & (t % (2 * w) >= w) & (j % (2 * w) < w))
        w //= 2
    masks.append(t == j)
    dstack = np.concatenate(blocks, axis=0).astype(np.float32)
    masks = np.stack([np.tile(m, (1, N_HGRN)) for m in masks]).astype(np.float32)
    lane_head = np.arange(D_HGRN)[None, :] // HGRN_DK
    head_rows = np.repeat(np.arange(N_HGRN), c)[:, None]
    head_mask = (head_rows == lane_head).astype(np.float32)
    same_head = (np.arange(D_HGRN)[:, None] // HGRN_DV == lane_head).astype(np.float32)
    return dstack, masks, head_mask, same_head


def _hgrn_kernel(q_ref, f_ref, v_ref, g_ref, lbraw_ref, ng_ref, s0_ref, dst_ref, msk_ref, hm_ref, sh_ref,
                 y_ref, st_ref, st_scr, *, c, chunks, layer, valid):
    ti = pl.program_id(1)
    n_seq = q_ref.shape[0]

    @pl.when(ti == 0)
    def _():
        for si in range(n_seq):
            wide = jnp.concatenate([s0_ref[si]] * N_HGRN, axis=1) * sh_ref[...]
            st_scr[si] = wide.T

    lbraw = lbraw_ref[...]
    e = jnp.exp(lbraw - jnp.max(lbraw, axis=0, keepdims=True))
    sm = e / jnp.sum(e, axis=0, keepdims=True)
    lrow = lax.broadcasted_iota(jnp.int32, sm.shape, 0)
    lb = jnp.sum(jnp.where((lrow >= 1) & (lrow <= layer), sm, 0.0), axis=0, keepdims=True)

    n_levels = msk_ref.shape[0] - 1
    hm = hm_ref[...]
    hm_b = hm.astype(BF16)
    same_head = sh_ref[...]
    shb = same_head.astype(BF16)

    def per_head_rows(x):
        if c % 16 == 0:
            return jnp.concatenate([x.astype(BF16)] * N_HGRN, axis=0) * hm_b
        return (jnp.concatenate([x] * N_HGRN, axis=0) * hm).astype(BF16)

    items = [(si, slice(ci * c, (ci + 1) * c)) for ci in range(chunks) for si in range(n_seq)]

    def prepare(item):
        si, rows = item
        f = lb + (1.0 - lb) * jax.nn.sigmoid(f_ref[si, rows, :])
        log_f = jnp.log(f)
        k = 1.0 - f
        if valid is not None:
            keep = lax.broadcasted_iota(jnp.int32, (c, D_HGRN), 0) < valid
            log_f = jnp.where(keep, log_f, 0.0)
            k = jnp.where(keep, k, 0.0)
        ex = _dot_exact_lhs(dst_ref[...], log_f)
        return q_ref[si, rows, :], k, v_ref[si, rows, :], ex

    prepared = [prepare(item) for item in items]

    scores = [jnp.zeros((c, N_HGRN * c), F32) for _ in items]
    for li in range(n_levels + 1):
        for n, (q, k, v, ex) in enumerate(prepared):
            if li < n_levels:
                decay = jnp.exp(ex[(2 + li) * c:(3 + li) * c])
                qt, kt = q * decay, k * decay
            else:
                qt, kt = q, k
            scores[n] = scores[n] + _dot_nt(qt.astype(BF16), per_head_rows(kt)) * msk_ref[li]
    intra = [_dot(sc.astype(BF16), per_head_rows(v)) for sc, (q, k, v, ex) in zip(scores, prepared)]

    outs = []
    for (si, rows), (q, k, v, ex), o in zip(items, prepared, intra):
        cum = ex[0:c]
        rem = ex[c:2 * c]
        st = st_scr[si]
        outs.append(o + _dot_nt((q * jnp.exp(cum)).astype(BF16), st.astype(BF16)))
        upd = _dot_tn(v.astype(BF16), (k * jnp.exp(rem)).astype(BF16))
        st_scr[si] = st * jnp.exp(cum[c - 1:c, :]) + upd * same_head

    for (si, rows), o in zip(items, outs):
        g = g_ref[si, rows, :]
        o2 = o * o
        o2_hi = o2.astype(BF16)
        o2_lo = (o2 - o2_hi.astype(F32)).astype(BF16)
        ms = (_dot(o2_hi, shb) + _dot(o2_lo, shb)) * (1.0 / HGRN_DV)
        y_ref[si, rows, :] = o * lax.rsqrt(ms + EPS) * ng_ref[...] * (g * jax.nn.sigmoid(g))

    @pl.when(ti == pl.num_programs(1) - 1)
    def _():
        for si in range(n_seq):
            wide = st_scr[si].T * same_head
            st_ref[si] = functools.reduce(
                lambda a, b: a + b, [wide[:, hd * HGRN_DV:(hd + 1) * HGRN_DV] for hd in range(N_HGRN)])


def _hgrn(z3, s0t, lb_raw, norm_g, layer, c, chunks, n_seq, valid=None):
    b, t, _ = z3.shape
    tt = c * chunks
    dstack, masks, head_mask, same_head = _hgrn_consts(c)
    const2 = lambda bi, ti: (0, 0)
    col = lambda cidx: pl.BlockSpec((n_seq, tt, D_HGRN), lambda bi, ti: (bi, ti, cidx))
    state = pl.BlockSpec((n_seq, D_HGRN, HGRN_DV), lambda bi, ti: (bi, 0, 0))
    return pl.pallas_call(
        functools.partial(_hgrn_kernel, c=c, chunks=chunks, layer=layer, valid=valid),
        grid=(b // n_seq, t // tt),
        in_specs=[col(COL_QB), col(COL_FB), col(COL_IB), col(COL_GB),
                  pl.BlockSpec((DEPTH, D_HGRN), const2),
                  pl.BlockSpec((1, D_HGRN), const2),
                  state,
                  pl.BlockSpec(dstack.shape, const2),
                  pl.BlockSpec(masks.shape, lambda bi, ti: (0, 0, 0)),
                  pl.BlockSpec(head_mask.shape, const2),
                  pl.BlockSpec(same_head.shape, const2)],
        out_specs=(pl.BlockSpec((n_seq, tt, D_HGRN), lambda bi, ti: (bi, ti, 0)), state),
        out_shape=(jax.ShapeDtypeStruct((b, t, D_HGRN), F32),
                   jax.ShapeDtypeStruct((b, D_HGRN, HGRN_DV), F32)),
        scratch_shapes=[pltpu.VMEM((n_seq, D_HGRN, D_HGRN), F32)],
        compiler_params=_cparams("parallel", "arbitrary"),
        name="hgrn",
    )(z3, z3, z3, z3, lb_raw, jnp.tile(norm_g, N_HGRN).reshape(1, D_HGRN), s0t,
      jnp.asarray(dstack, BF16), jnp.asarray(masks), jnp.asarray(head_mask), jnp.asarray(same_head))


def _strided_rows(start, dil):
    return pl.ds(start, BAND, stride=dil) if dil > 1 else pl.ds(start, BAND)


def _dil_attn_kernel(q_ref, kp_ref, kc_ref, vp_ref, vc_ref, slope_ref, o_ref, o_scr, d_scr, m_scr):
    first_block = pl.program_id(1) == 0
    n2 = 2 * BAND
    row = lax.broadcasted_iota(jnp.int32, (n2, n2), 0)
    col = lax.broadcasted_iota(jnp.int32, (n2, n2), 1)
    back = (row & (BAND - 1)) - (col & (BAND - 1)) + jnp.where(col < BAND, BAND, 0)
    in_band = (back >= 0) & (back <= BAND)
    steps = back.astype(F32)
    slope = jnp.concatenate([slope_ref[0], slope_ref[0]], axis=1)
    lane_first = lax.broadcasted_iota(jnp.int32, (BAND, PAIR), 1) < ATT_DH
    scale = ATT_DH ** -0.5
    ones = jnp.ones((BAND, PAIR), BF16)

    def scores(unit):
        g, cur, prev, prev_refs, bias = unit
        kp_r, vp_r = prev_refs
        q = q_ref[0, cur, :] * scale
        q2 = jnp.concatenate([jnp.where(lane_first, q, 0.0), jnp.where(lane_first, 0.0, q)],
                             axis=0).astype(BF16)
        keys = jnp.concatenate([kp_r[0, prev, :], kc_ref[0, cur, :]], axis=0).astype(BF16)
        return _dot_nt(q2, keys) - bias

    def probs(s):
        m = jnp.max(s, axis=1, keepdims=True)
        return jnp.exp(s - m).astype(BF16), m

    def values(unit, pm):
        g, cur, prev, prev_refs, _ = unit
        kp_r, vp_r = prev_refs
        p, m = pm
        v1 = jnp.concatenate(
            [jnp.concatenate([vp_r[0, prev, :], vc_ref[0, cur, :]], axis=0).astype(BF16),
             jnp.concatenate([ones, ones], axis=0)], axis=1)
        pvd = _dot(p, v1)
        pv, den = pvd[:, 0:PAIR], pvd[:, PAIR:2 * PAIR]
        m_b = jnp.broadcast_to(m, (n2, PAIR))
        o_scr[g, cur, :] = jnp.where(lane_first, pv[0:BAND], pv[BAND:n2])
        d_scr[g, cur, :] = jnp.where(lane_first, den[0:BAND], den[BAND:n2])
        m_scr[g, cur, :] = jnp.where(lane_first, m_b[0:BAND], m_b[BAND:n2])

    units = []
    for g, dil in enumerate(DILATIONS):
        bias = jnp.where(in_band, slope * steps * float(dil), -NEG_BIG)
        bias_first = jnp.where((col < BAND) & first_block, -NEG_BIG, bias)
        span = BAND * dil
        for j in range(SUPER // span):
            for r in range(dil):
                cur = _strided_rows(j * span + r, dil)
                if j == 0:
                    units.append((g, cur, _strided_rows(SUPER - span + r, dil), (kp_ref, vp_ref), bias_first))
                else:
                    units.append((g, cur, _strided_rows((j - 1) * span + r, dil), (kc_ref, vc_ref), bias))

    groups = [units[i:i + UNITS_IN_FLIGHT] for i in range(0, len(units), UNITS_IN_FLIGHT)]
    ss_next = [scores(u) for u in groups[0]]
    for gi, group in enumerate(groups):
        ss = ss_next
        if gi + 1 < len(groups):
            ss_next = [scores(u) for u in groups[gi + 1]]
        pm = [probs(s) for s in ss]
        for u, v in zip(group, pm):
            values(u, v)

    rows_per_iter = 2 * BAND

    def mix(i, carry):
        rows = pl.ds(pl.multiple_of(i * rows_per_iter, rows_per_iter), rows_per_iter)
        ms = [m_scr[g, rows, :] for g in range(len(DILATIONS))]
        m = functools.reduce(jnp.maximum, ms)
        ws = [jnp.exp(mg - m) for mg in ms]
        num = sum(wgt * o_scr[g, rows, :] for g, wgt in enumerate(ws))
        den = sum(wgt * d_scr[g, rows, :] for g, wgt in enumerate(ws))
        o_ref[0, rows, :] = num / den
        return carry

    lax.fori_loop(0, SUPER // rows_per_iter, mix, 0)


def _dil_attn(z3):
    b, t, _ = z3.shape
    n_pairs = N_ATT // 2
    lanes_per_col = D_ATT // PAIR
    blk = lambda colblk, prev: pl.BlockSpec(
        (1, SUPER, PAIR),
        (lambda bi, sb, p: (bi, jnp.maximum(sb - 1, 0), colblk * lanes_per_col + p)) if prev
        else (lambda bi, sb, p: (bi, sb, colblk * lanes_per_col + p)))
    slopes = np.asarray(ALIBI_SLOPES, np.float32).reshape(n_pairs, 2, 1, 1)
    slope_tbl = np.broadcast_to(slopes, (n_pairs, 2, BAND, PAIR)).reshape(n_pairs, 2 * BAND, PAIR)
    return pl.pallas_call(
        _dil_attn_kernel,
        grid=(b, t // SUPER, n_pairs),
        in_specs=[blk(COL_QC, False), blk(COL_KC, True), blk(COL_KC, False),
                  blk(COL_VC, True), blk(COL_VC, False),
                  pl.BlockSpec((1, 2 * BAND, PAIR), lambda bi, sb, p: (p, 0, 0))],
        out_specs=pl.BlockSpec((1, SUPER, PAIR), lambda bi, sb, p: (bi, sb, p)),
        out_shape=jax.ShapeDtypeStruct((b, t, D_ATT), F32),
        scratch_shapes=[pltpu.VMEM((len(DILATIONS), SUPER, PAIR), F32)] * 3,
        compiler_params=_cparams("parallel", "parallel", "parallel"),
        name="dil_attn",
    )(z3, z3, z3, z3, z3, jnp.asarray(slope_tbl))


def _cache_attn_kernel(q_ref, kn_ref, vn_ref, kc_ref, vc_ref, slope_ref, *rest, n_new):
    o_ref, ko_ref, vo_ref = rest[-3:]
    w = kc_ref.shape[-1]
    kn = kn_ref[0]
    vn = vn_ref[0]
    kt = kc_ref[0, 0]
    vt = vc_ref[0, 0]

    sel_row = lax.broadcasted_iota(jnp.int32, (SAMPLE_ROWS, LANE), 0)
    sel_lane = lax.broadcasted_iota(jnp.int32, (SAMPLE_ROWS, LANE), 1)
    sel = jnp.where((sel_lane == LANE - n_new + sel_row) & (sel_row < n_new), 1.0, 0.0).astype(BF16)
    tail_new = lax.broadcasted_iota(jnp.int32, (HALF_ATT, LANE), 1) >= LANE - n_new

    def shift_in(cache_t, new_rows, out_ref):
        rolled = pltpu.roll(cache_t, w - n_new, axis=1)
        out_ref[0, 0, :, 0:w - LANE] = rolled[:, 0:w - LANE]
        out_ref[0, 0, :, w - LANE:w] = jnp.where(tail_new, _dot_exact_tn(new_rows, sel), rolled[:, w - LANE:w])

    shift_in(kt, kn, ko_ref)
    shift_in(vt, vn, vo_ref)

    heads = N_ATT // 2
    rows = heads * SAMPLE_ROWS
    lane_head = lax.broadcasted_iota(jnp.int32, (rows, HALF_ATT), 1) >> _log2(ATT_DH)
    row_head = lax.broadcasted_iota(jnp.int32, (rows, HALF_ATT), 0) >> _log2(SAMPLE_ROWS)
    own = lane_head == row_head
    q8 = jnp.where(own, jnp.concatenate([q_ref[0]] * heads, axis=0), 0.0).astype(BF16)
    scale = ATT_DH ** -0.5
    slope = slope_ref[0][:, 0:1]

    def weights(dist):
        mult = jnp.zeros(dist.shape, F32)
        for dil in DILATIONS:
            hit = (dist >= 0) & (dist <= BAND * dil) & ((dist & (dil - 1)) == 0)
            mult = mult + jnp.where(hit, 1.0, 0.0)
        return mult

    def mask(s, dist):
        mult = weights(dist)
        return jnp.where(mult > 0.0, s * scale - slope * dist.astype(F32), NEG_BIG), mult

    t_c = lax.broadcasted_iota(jnp.int32, (rows, w), 0) & (SAMPLE_ROWS - 1)
    pos_c = lax.broadcasted_iota(jnp.int32, (rows, w), 1)
    s_c, mult_c = mask(_dot(q8, kt.astype(BF16)), w + t_c - pos_c)
    t_n = lax.broadcasted_iota(jnp.int32, (rows, SAMPLE_ROWS), 0) & (SAMPLE_ROWS - 1)
    pos_n = lax.broadcasted_iota(jnp.int32, (rows, SAMPLE_ROWS), 1)
    s_n, mult_n = mask(_dot_nt(q8, kn.astype(BF16)), t_n - pos_n)

    m = jnp.maximum(jnp.max(s_c, axis=1, keepdims=True), jnp.max(s_n, axis=1, keepdims=True))
    e_c = jnp.exp(s_c - m) * mult_c
    e_n = jnp.exp(s_n - m) * mult_n
    den = jnp.sum(e_c, axis=1, keepdims=True) + jnp.sum(e_n, axis=1, keepdims=True)
    o8 = (_dot_nt(e_c.astype(BF16), vt.astype(BF16)) + _dot(e_n.astype(BF16), vn.astype(BF16))) / den
    o8 = jnp.where(own, o8, 0.0)
    o = o8[0:SAMPLE_ROWS]
    for hd in range(1, heads):
        o = o + o8[hd * SAMPLE_ROWS:(hd + 1) * SAMPLE_ROWS]
    o_ref[0] = o


def _cache_attn(z3, cache_kt, cache_vt, layer, out_prev, n_new):
    depth, b, _, w = cache_kt.shape
    halves = D_ATT // HALF_ATT
    cols = D_ATT // HALF_ATT
    new = lambda colblk: pl.BlockSpec((1, SAMPLE_ROWS, HALF_ATT), lambda bi, hf: (bi, 0, colblk * cols + hf))
    cache = pl.BlockSpec((1, 1, HALF_ATT, w), lambda bi, hf: (layer, bi, hf, 0))
    heads = N_ATT // halves
    slope_tbl = np.broadcast_to(np.repeat(np.asarray(ALIBI_SLOPES, np.float32), SAMPLE_ROWS)
                                .reshape(halves, heads * SAMPLE_ROWS, 1), (halves, heads * SAMPLE_ROWS, LANE))
    in_specs = [new(COL_QC), new(COL_KC), new(COL_VC), cache, cache,
                pl.BlockSpec((1, heads * SAMPLE_ROWS, LANE), lambda bi, hf: (hf, 0, 0))]
    args = [z3, z3, z3, cache_kt, cache_vt, jnp.asarray(slope_tbl)]
    aliases = {}
    if out_prev is not None:
        in_specs += [pl.BlockSpec(memory_space=pl.ANY)] * 2
        aliases = {len(args): 1, len(args) + 1: 2}
        args += list(out_prev)
    stacked = jax.ShapeDtypeStruct((depth, b, D_ATT, w), F32)
    return pl.pallas_call(
        functools.partial(_cache_attn_kernel, n_new=n_new),
        grid=(b, halves),
        in_specs=in_specs,
        out_specs=(pl.BlockSpec((1, SAMPLE_ROWS, HALF_ATT), lambda bi, hf: (bi, 0, hf)), cache, cache),
        out_shape=(jax.ShapeDtypeStruct((b, SAMPLE_ROWS, D_ATT), F32), stacked, stacked),
        input_output_aliases=aliases,
        compiler_params=_cparams("parallel", "parallel"),
        name="cache_attn",
    )(*args)


def _mix_out(x, ya, yb, yc, gn, w_ref):
    acc = _dot(ya.astype(BF16), w_ref[0:D_LRU, :])
    acc = acc + _dot(yb.astype(BF16), w_ref[D_LRU:D_LRU + D_HGRN, :])
    acc = acc + _dot(_rms(yc, gn).astype(BF16), w_ref[D_LRU + D_HGRN:D_MODEL, :])
    return x + acc


def _wout_kernel(x_ref, ya_ref, yb_ref, yc_ref, gn_ref, w_ref, out_ref):
    out_ref[...] = _mix_out(x_ref[...], ya_ref[...], yb_ref[...], yc_ref[...], gn_ref[...], w_ref)


def _wout(x, ya, yb, yc, gn_c, w_out, tm):
    m = x.shape[0]
    spec = lambda n: pl.BlockSpec((tm, n), lambda i: (i, 0))
    return pl.pallas_call(
        _wout_kernel,
        grid=(m // tm,),
        in_specs=[spec(D_MODEL), spec(D_LRU), spec(D_HGRN), spec(D_ATT),
                  pl.BlockSpec((1, D_ATT), lambda i: (0, 0)), _resident((D_MODEL, D_MODEL))],
        out_specs=spec(D_MODEL),
        out_shape=jax.ShapeDtypeStruct((m, D_MODEL), F32),
        compiler_params=_cparams("parallel"),
        name="wout",
    )(x, ya, yb, yc, gn_c.reshape(1, D_ATT), w_out)


def _xattn_kernel(x_ref, ya_ref, yb_ref, yc_ref, gnc_ref, wmix_ref, g_ref, wq_ref, mk_ref, mv_ref, wo_ref, o_ref):
    x = _mix_out(x_ref[0], ya_ref[0], yb_ref[0], yc_ref[0], gnc_ref[...], wmix_ref)
    q = _dot(_rms(x, g_ref[...]).astype(BF16), wq_ref[...])
    mk = mk_ref[0].astype(BF16)
    mv = mv_ref[0].astype(BF16)
    scale = XHEAD_DIM ** -0.5
    heads = [slice(hd * XHEAD_DIM, (hd + 1) * XHEAD_DIM) for hd in range(N_XHEADS)]
    ss = [_dot_nt(q[:, lanes].astype(BF16), mk[:, lanes]) * scale for lanes in heads]
    es = [jnp.exp(s - jnp.max(s, axis=1, keepdims=True)) for s in ss]
    ps = [(e / jnp.sum(e, axis=1, keepdims=True)).astype(BF16) for e in es]
    os_ = [_dot(p, mv[:, lanes]).astype(BF16) for p, lanes in zip(ps, heads)]
    acc = x
    for oh, lanes in zip(os_, heads):
        acc = acc + _dot(oh, wo_ref[lanes, :])
    o_ref[0] = acc


def _xattn(x3, ya, yb, yc, gn_c, w_mix, g, wq, mem_kv, wo, tm):
    b, t, _ = x3.shape
    const = lambda bi, ti: (0, 0)
    mat = _resident((D_MODEL, D_MODEL))
    rows = lambda n: pl.BlockSpec((1, tm, n), lambda bi, ti: (bi, ti, 0))
    return pl.pallas_call(
        _xattn_kernel,
        grid=(b, t // tm),
        in_specs=[rows(D_MODEL), rows(D_LRU), rows(D_HGRN), rows(D_ATT),
                  pl.BlockSpec((1, D_ATT), const), mat,
                  pl.BlockSpec((1, D_MODEL), const), mat,
                  pl.BlockSpec((1, N_MEM, D_MODEL), lambda bi, ti: (bi, 0, 0)),
                  pl.BlockSpec((1, N_MEM, D_MODEL), lambda bi, ti: (bi, 0, 1)),
                  mat],
        out_specs=rows(D_MODEL),
        out_shape=jax.ShapeDtypeStruct((b, t, D_MODEL), F32),
        compiler_params=_cparams("parallel", "parallel"),
        name="xattn",
    )(x3, ya, yb, yc, gn_c.reshape(1, D_ATT), w_mix, g.reshape(1, D_MODEL), wq, mem_kv, mem_kv, wo)


def _mem_attn_kernel(q_ref, mk_ref, mv_ref, o_ref):
    t = q_ref.shape[1]
    rows = N_MEM * N_XHEADS
    k2 = mk_ref.at[0, 0].reshape(rows, XHEAD_DIM)[...].astype(BF16)
    v2 = mv_ref.at[0, 0].reshape(rows, XHEAD_DIM)[...].astype(BF16)
    q = q_ref[0]
    qs = jnp.concatenate([q[:, hd * XHEAD_DIM:(hd + 1) * XHEAD_DIM] for hd in range(N_XHEADS)], axis=0)
    row_head = lax.broadcasted_iota(jnp.int32, (N_XHEADS * t, rows), 0) >> _log2(t)
    col_head = lax.broadcasted_iota(jnp.int32, (N_XHEADS * t, rows), 1) & (N_XHEADS - 1)
    s = _dot_nt(qs.astype(BF16), k2) * (XHEAD_DIM ** -0.5)
    s = jnp.where(row_head == col_head, s, NEG_BIG)
    e = jnp.exp(s - jnp.max(s, axis=1, keepdims=True))
    p = e / jnp.sum(e, axis=1, keepdims=True)
    o = _dot(p.astype(BF16), v2)
    for hd in range(N_XHEADS):
        o_ref[0, :, hd * XHEAD_DIM:(hd + 1) * XHEAD_DIM] = o[hd * t:(hd + 1) * t]


def _mem_attn(q3, mem_k, mem_v, layer):
    b, t, _ = q3.shape
    row = pl.BlockSpec((1, t, D_MODEL), lambda bi: (bi, 0, 0))
    mem = pl.BlockSpec((1, 1, N_MEM, N_XHEADS, XHEAD_DIM), lambda bi: (layer, bi, 0, 0, 0))
    return pl.pallas_call(
        _mem_attn_kernel,
        grid=(b,),
        in_specs=[row, mem, mem],
        out_specs=row,
        out_shape=jax.ShapeDtypeStruct((b, t, D_MODEL), F32),
        compiler_params=_cparams("parallel"),
        name="mem_attn",
    )(q3, mem_k, mem_v)


def _proj_res_kernel(x_ref, a_ref, w_ref, o_ref):
    o_ref[...] = x_ref[...] + _dot(a_ref[...].astype(BF16), w_ref[...])


def _proj_res(x, a, w, tm):
    m, k = a.shape
    n = w.shape[1]
    return pl.pallas_call(
        _proj_res_kernel,
        grid=(m // tm,),
        in_specs=[pl.BlockSpec((tm, n), lambda i: (i, 0)),
                  pl.BlockSpec((tm, k), lambda i: (i, 0)),
                  _resident((k, n))],
        out_specs=pl.BlockSpec((tm, n), lambda i: (i, 0)),
        out_shape=jax.ShapeDtypeStruct((m, n), F32),
        compiler_params=_cparams("parallel"),
        name="proj_res",
    )(x, a, w)


def _block_diag(w):
    n, c, d = w.shape
    eye = jnp.eye(n, dtype=w.dtype)
    return jnp.einsum("ncd,nm->ncmd", w, eye).reshape(n * c, n * d)


def kernel(x_prompt, x_sample, state_lru_h, state_lru_conv, state_hgrn, cache_swa_k, cache_swa_v, cache_mem_k, cache_mem_v, mem_prompt, n_ffn1, ffn1_wg, ffn1_wu, ffn1_wd, n_mix, w_in, lru_conv_w, lru_conv_b, lru_wa, lru_ba, lru_wx, lru_bx, lru_lambda, hgrn_lb, hgrn_norm, gn_a, gn_c, w_out, n_cross, x_wq, x_wk, x_wv, x_wo, n_ffn2, ffn2_wg, ffn2_wu, ffn2_wd, n_final):
    bp, tp, _ = x_prompt.shape
    bs, ts_new, _ = x_sample.shape
    w_buf = cache_swa_k.shape[2]
    assert ts_new <= SAMPLE_ROWS and tp % (BAND * max(DILATIONS)) == 0 and tp >= MAX_WINDOW
    tm_p = 512
    tm_s = bs * SAMPLE_ROWS

    xp = x_prompt.reshape(bp * tp, D_MODEL)
    xs = jnp.pad(x_sample, ((0, 0), (0, SAMPLE_ROWS - ts_new), (0, 0))).reshape(tm_s, D_MODEL)
    mem2 = mem_prompt.reshape(bp * N_MEM, D_MODEL)
    ones = jnp.ones((D_MODEL,), F32)

    outs = {name: [] for name in ("p_h", "p_c", "p_s", "p_k", "p_v", "p_mk", "p_mv", "s_h", "s_c", "s_s")}
    to_feature_major = lambda c: jnp.transpose(c, (0, 1, 3, 4, 2)).reshape(DEPTH, bs, D_ATT, w_buf)
    cache_kt, cache_vt = to_feature_major(cache_swa_k), to_feature_major(cache_swa_v)
    swa_out = None
    yp = ys = None
    for l in range(DEPTH):
        bf = lambda w: w[l].astype(BF16)
        wg1, wu1, wd1 = bf(ffn1_wg), bf(ffn1_wu), bf(ffn1_wd)
        wg2, wu2, wd2 = bf(ffn2_wg), bf(ffn2_wu), bf(ffn2_wd)
        win, wo_mix, wq, wo_x = bf(w_in), bf(w_out), bf(x_wq), bf(x_wo)
        wkv = jnp.concatenate([x_wk[l], x_wv[l]], axis=1).astype(BF16)
        lp = dict(conv_w=lru_conv_w[l], conv_b=lru_conv_b[l].reshape(1, D_LRU),
                  wa=_block_diag(lru_wa[l]).astype(BF16), ba=lru_ba[l].reshape(1, D_LRU),
                  wx=_block_diag(lru_wx[l]).astype(BF16), bx=lru_bx[l].reshape(1, D_LRU),
                  lam=lru_lambda[l].reshape(1, D_LRU), gn_a=gn_a[l].reshape(1, D_LRU))
        last = l == DEPTH - 1

        mkv = _proj(mem2, ones, wkv, tm=bp * N_MEM, norm=False).reshape(bp, N_MEM, 2 * D_MODEL)
        xp = _ffn(xp, n_ffn1[l], wg1, wu1, wd1, tm_p)
        z3 = _proj(xp, n_mix[l], win, tm_p).reshape(bp, tp, D_IN)
        ya, hl = _lru(z3, jnp.zeros((bp, SUBLANE, D_LRU), F32), jnp.zeros((bp, 1, D_LRU), F32), lp, tt=256)
        yb, st = _hgrn(z3, jnp.zeros((bp, D_HGRN, HGRN_DV), F32), hgrn_lb, hgrn_norm[l], l,
                       c=HGRN_CHUNK, chunks=8, n_seq=bp)
        yc = _dil_attn(z3)
        xp = _xattn(xp.reshape(bp, tp, D_MODEL), ya, yb, yc, gn_c[l], wo_mix, n_cross[l], wq, mkv, wo_x, tm_p)
        xp = xp.reshape(bp * tp, D_MODEL)
        if last:
            xp, yp = _ffn(xp, n_ffn2[l], wg2, wu2, wd2, tm_p, g_final=n_final)
        else:
            xp = _ffn(xp, n_ffn2[l], wg2, wu2, wd2, tm_p)
        outs["p_h"].append(hl[:, SUBLANE - 1])
        outs["p_c"].append(z3[:, tp - (CONV_W - 1):, 0:D_LRU])
        outs["p_s"].append(st.reshape(bp, N_HGRN, HGRN_DK, HGRN_DV))
        keep = min(MAX_WINDOW, tp)
        kcol = COL_KC * D_ATT
        vcol = COL_VC * D_ATT
        outs["p_k"].append(z3[:, tp - keep:, kcol:kcol + D_ATT].reshape(bp, keep, N_ATT, ATT_DH))
        outs["p_v"].append(z3[:, tp - keep:, vcol:vcol + D_ATT].reshape(bp, keep, N_ATT, ATT_DH))
        outs["p_mk"].append(mkv[:, :, 0:D_MODEL].reshape(bp, N_MEM, N_XHEADS, XHEAD_DIM))
        outs["p_mv"].append(mkv[:, :, D_MODEL:].reshape(bp, N_MEM, N_XHEADS, XHEAD_DIM))

        xs = _ffn(xs, n_ffn1[l], wg1, wu1, wd1, tm_s)
        zs3 = _proj(xs, n_mix[l], win, tm_s).reshape(bs, SAMPLE_ROWS, D_IN)
        conv_buf = jnp.pad(state_lru_conv[l], ((0, 0), (SUBLANE - (CONV_W - 1), 0), (0, 0)))
        ya, hl = _lru(zs3, conv_buf, state_lru_h[l].reshape(bs, 1, D_LRU), lp, tt=SAMPLE_ROWS, valid=ts_new)
        yb, st = _hgrn(zs3, state_hgrn[l].reshape(bs, D_HGRN, HGRN_DV), hgrn_lb, hgrn_norm[l], l,
                       c=SAMPLE_ROWS, chunks=1, n_seq=4, valid=ts_new)
        yc, ko, vo = _cache_attn(zs3, cache_kt, cache_vt, l, swa_out, ts_new)
        swa_out = (ko, vo)
        xs = _wout(xs, ya.reshape(tm_s, D_LRU), yb.reshape(tm_s, D_HGRN), yc.reshape(tm_s, D_ATT),
                   gn_c[l], wo_mix, tm_s)
        qs = _proj(xs, n_cross[l], wq, tm_s).reshape(bs, SAMPLE_ROWS, D_MODEL)
        xs = _proj_res(xs, _mem_attn(qs, cache_mem_k, cache_mem_v, l).reshape(tm_s, D_MODEL), wo_x, tm_s)
        if last:
            xs, ys = _ffn(xs, n_ffn2[l], wg2, wu2, wd2, tm_s, g_final=n_final)
        else:
            xs = _ffn(xs, n_ffn2[l], wg2, wu2, wd2, tm_s)
        outs["s_h"].append(hl[:, SUBLANE - 1])
        outs["s_c"].append(zs3[:, ts_new - (CONV_W - 1):ts_new, 0:D_LRU])
        outs["s_s"].append(st.reshape(bs, N_HGRN, HGRN_DK, HGRN_DV))

    y_prompt = yp.reshape(bp, tp, D_MODEL)
    y_sample = ys.reshape(bs, SAMPLE_ROWS, D_MODEL)[:, :ts_new]
    stack = lambda name: jnp.stack(outs[name])
    from_feature_major = lambda c: jnp.transpose(c.reshape(DEPTH, bs, N_ATT, ATT_DH, w_buf), (0, 1, 4, 2, 3))
    return (y_prompt, y_sample,
            stack("p_h"), stack("p_c"), stack("p_s"), stack("p_k"), stack("p_v"), stack("p_mk"), stack("p_mv"),
            stack("s_h"), stack("s_c"), stack("s_s"), from_feature_major(swa_out[0]), from_feature_major(swa_out[1]))
```

```python
import functools

import numpy as np
import jax
import jax.numpy as jnp
from jax import lax
from jax.experimental import pallas as pl
from jax.experimental.pallas import tpu as pltpu

F32 = jnp.float32
BF16 = jnp.bfloat16

D_MODEL = 1024
DEPTH = 4
D_LRU = 256
N_LRU_BLOCKS = 4
CONV_W = 4
LRU_C = 8.0
HGRN_DK = 64
HGRN_DV = 64
D_HGRN = 256
N_HGRN = 4
HGRN_CHUNK = 64
SCAN_GROUP = 16
ATT_DH = 64
D_ATT = 512
N_ATT = 8
DILATIONS = (1, 4, 16)
BAND = 128
SUPER = BAND * max(DILATIONS)
PAIR = 2 * ATT_DH
UNITS_IN_FLIGHT = 2
MAX_WINDOW = 2048
N_MEM = 256
N_XHEADS = 4
XHEAD_DIM = 256
D_FF = 2816
D_IN = 3072
EPS = 1e-6

COL_XA, COL_GA, COL_QB, COL_FB, COL_IB, COL_GB = 0, 1, 2, 3, 4, 5
COL_QC, COL_KC, COL_VC = 3, 4, 5
ATT_COLS_PER_ROW = D_IN // D_ATT

HALF_ATT = D_ATT
SUBLANE = 8
LANE = 128
SAMPLE_ROWS = 8
VMEM_LIMIT = 56 * 1024 * 1024
NEG_BIG = -1e30

ALIBI_SLOPES = tuple(float(2.0 ** (-8.0 * (h + 1) / N_ATT)) for h in range(N_ATT))


def _log2(n):
    assert n & (n - 1) == 0
    return n.bit_length() - 1


def _cparams(*sem):
    return pltpu.CompilerParams(dimension_semantics=sem, vmem_limit_bytes=VMEM_LIMIT)


def _rms(x, g):
    ms = jnp.mean(x * x, axis=-1, keepdims=True)
    return x * lax.rsqrt(ms + EPS) * g


def _expm1(x):
    e = jnp.exp(x)
    near = (e - 1.0) * x / jnp.log(e)
    return jnp.where(e == 1.0, x, jnp.where(jnp.abs(x) > 1.0, e - 1.0, near))


def _dot(a, b):
    return jnp.dot(a, b, preferred_element_type=F32)


def _dot_nt(a, b):
    return lax.dot_general(a, b, (((1,), (1,)), ((), ())), preferred_element_type=F32)


def _dot_tn(a, b):
    return lax.dot_general(a, b, (((0,), (0,)), ((), ())), preferred_element_type=F32)


def _split3(x):
    hi = x.astype(BF16)
    r1 = x - hi.astype(F32)
    mid = r1.astype(BF16)
    lo = (r1 - mid.astype(F32)).astype(BF16)
    return hi, mid, lo


def _dot_exact_lhs(a_bf16, x):
    return sum(_dot(a_bf16, p) for p in _split3(x))


def _dot_exact_tn(x, a_bf16):
    return sum(_dot_tn(p, a_bf16) for p in _split3(x))


def _ffn_kernel(*refs, final_norm):
    if final_norm:
        x_ref, g_ref, wg_ref, wu_ref, wd_ref, gf_ref, o_ref, y_ref = refs
    else:
        x_ref, g_ref, wg_ref, wu_ref, wd_ref, o_ref = refs
    x = x_ref[...]
    h = _rms(x, g_ref[...]).astype(BF16)
    gate = _dot(h, wg_ref[...])
    up = _dot(h, wu_ref[...])
    act = (gate * jax.nn.sigmoid(gate) * up).astype(BF16)
    out = x + 0.5 * _dot(act, wd_ref[...])
    o_ref[...] = out
    if final_norm:
        y_ref[...] = _rms(out, gf_ref[...])


def _resident(shape):
    return pl.BlockSpec(shape, lambda *_: (0,) * len(shape), pipeline_mode=pl.Buffered(1))


def _ffn(x, g, wg, wu, wd, tm, g_final=None):
    m = x.shape[0]
    final_norm = g_final is not None
    row = pl.BlockSpec((tm, D_MODEL), lambda i: (i, 0))
    vec = _resident((1, D_MODEL))
    in_specs = [row, vec, _resident((D_MODEL, D_FF)), _resident((D_MODEL, D_FF)), _resident((D_FF, D_MODEL))]
    args = [x, g.reshape(1, D_MODEL), wg, wu, wd]
    out_shape = jax.ShapeDtypeStruct((m, D_MODEL), F32)
    out_specs = row
    if final_norm:
        in_specs.append(vec)
        args.append(g_final.reshape(1, D_MODEL))
        out_shape = (out_shape, out_shape)
        out_specs = (row, row)
    return pl.pallas_call(
        functools.partial(_ffn_kernel, final_norm=final_norm),
        grid=(m // tm,),
        in_specs=in_specs, out_specs=out_specs, out_shape=out_shape,
        compiler_params=_cparams("parallel"),
        name="ffn",
    )(*args)


def _proj_kernel(x_ref, g_ref, w_ref, o_ref, *, norm):
    x = x_ref[...]
    if norm:
        x = _rms(x, g_ref[...])
    o_ref[...] = _dot(x.astype(BF16), w_ref[...])


def _proj(x, g, w, tm, norm=True):
    m, k = x.shape
    n = w.shape[1]
    return pl.pallas_call(
        functools.partial(_proj_kernel, norm=norm),
        grid=(m // tm,),
        in_specs=[pl.BlockSpec((tm, k), lambda i: (i, 0)),
                  pl.BlockSpec((1, k), lambda i: (0, 0)),
                  _resident((k, n))],
        out_specs=pl.BlockSpec((tm, n), lambda i: (i, 0)),
        out_shape=jax.ShapeDtypeStruct((m, n), F32),
        compiler_params=_cparams("parallel"),
        name="proj",
    )(x, g.reshape(1, k), w)


def _lru_kernel(xa_ref, ga_ref, cb_ref, h0_ref, cw_ref, cbias_ref, wa_ref, ba_ref, wx_ref, bx_ref,
                lam_ref, gn_ref, ya_ref, hl_ref, xbuf, abuf, ubuf, hc, *, tt, pad, valid):
    ti = pl.program_id(1)

    @pl.when(ti == 0)
    def _():
        xbuf[0:SUBLANE, :] = cb_ref[0]
        hc[...] = jnp.broadcast_to(h0_ref[0], (SUBLANE, D_LRU))
        for s in range(2):
            abuf[s, 0:pad, :] = jnp.ones((pad, D_LRU), F32)
            ubuf[s, 0:pad, :] = jnp.zeros((pad, D_LRU), F32)

    x = xa_ref[0]
    xbuf[SUBLANE:SUBLANE + tt, :] = x
    y = cbias_ref[...] + cw_ref[CONV_W - 1:CONV_W, :] * x
    for tap in range(CONV_W - 1):
        back = CONV_W - 1 - tap
        y = y + cw_ref[tap:tap + 1, :] * xbuf[SUBLANE - back:SUBLANE - back + tt, :]
    xbuf[0:SUBLANE, :] = x[tt - SUBLANE:tt, :]

    yb = y.astype(BF16)
    r = jax.nn.sigmoid(_dot(yb, wa_ref[...]) + ba_ref[...])
    i = jax.nn.sigmoid(_dot(yb, wx_ref[...]) + bx_ref[...])
    lam = lam_ref[...]
    softplus_neg_lam = jnp.maximum(-lam, 0.0) + jnp.log1p(jnp.exp(-jnp.abs(lam)))
    log_a = -LRU_C * r * softplus_neg_lam
    a = jnp.exp(log_a)
    u = jnp.sqrt(-_expm1(2.0 * log_a)) * (i * y)
    if valid is not None:
        keep = lax.broadcasted_iota(jnp.int32, (tt, D_LRU), 0) < valid
        a = jnp.where(keep, a, 1.0)
        u = jnp.where(keep, u, 0.0)

    abuf[0, pad:pad + tt, :] = a
    ubuf[0, pad:pad + tt, :] = u
    cur, k = 0, 1
    while k < min(tt, SCAN_GROUP):
        a_c = abuf[cur, pad:pad + tt, :]
        u_c = ubuf[cur, pad:pad + tt, :]
        a_s = abuf[cur, pad - k:pad - k + tt, :]
        u_s = ubuf[cur, pad - k:pad - k + tt, :]
        abuf[1 - cur, pad:pad + tt, :] = a_c * a_s
        ubuf[1 - cur, pad:pad + tt, :] = a_c * u_s + u_c
        cur, k = 1 - cur, 2 * k
    a_w = abuf[cur, pad:pad + tt, :]
    u_w = ubuf[cur, pad:pad + tt, :]
    h_before = hc[SUBLANE - 1:SUBLANE, :]
    groups = []
    for lo in range(0, tt, SCAN_GROUP):
        h_before = a_w[lo:lo + SCAN_GROUP] * h_before + u_w[lo:lo + SCAN_GROUP]
        groups.append(h_before)
    h = groups[0] if len(groups) == 1 else jnp.concatenate(groups, axis=0)
    hc[...] = h[tt - SUBLANE:tt, :]
    hl_ref[0] = h[tt - SUBLANE:tt, :]
    ya_ref[0] = _rms(h * jax.nn.gelu(ga_ref[0]), gn_ref[...])


def _lru(z3, conv_buf, h0, lp, tt, valid=None):
    b, t, _ = z3.shape
    pad = max(min(tt, SCAN_GROUP) // 2, SUBLANE)
    vec = pl.BlockSpec((1, D_LRU), lambda bi, ti: (0, 0))
    mat = pl.BlockSpec((D_LRU, D_LRU), lambda bi, ti: (0, 0))
    return pl.pallas_call(
        functools.partial(_lru_kernel, tt=tt, pad=pad, valid=valid),
        grid=(b, t // tt),
        in_specs=[pl.BlockSpec((1, tt, D_LRU), lambda bi, ti: (bi, ti, COL_XA)),
                  pl.BlockSpec((1, tt, D_LRU), lambda bi, ti: (bi, ti, COL_GA)),
                  pl.BlockSpec((1, SUBLANE, D_LRU), lambda bi, ti: (bi, 0, 0)),
                  pl.BlockSpec((1, 1, D_LRU), lambda bi, ti: (bi, 0, 0)),
                  pl.BlockSpec((CONV_W, D_LRU), lambda bi, ti: (0, 0)),
                  vec, mat, vec, mat, vec, vec, vec],
        out_specs=(pl.BlockSpec((1, tt, D_LRU), lambda bi, ti: (bi, ti, 0)),
                   pl.BlockSpec((1, SUBLANE, D_LRU), lambda bi, ti: (bi, 0, 0))),
        out_shape=(jax.ShapeDtypeStruct((b, t, D_LRU), F32),
                   jax.ShapeDtypeStruct((b, SUBLANE, D_LRU), F32)),
        scratch_shapes=[pltpu.VMEM((SUBLANE + tt, D_LRU), F32),
                        pltpu.VMEM((2, pad + tt, D_LRU), F32),
                        pltpu.VMEM((2, pad + tt, D_LRU), F32),
                        pltpu.VMEM((SUBLANE, D_LRU), F32)],
        compiler_params=_cparams("parallel", "arbitrary"),
        name="lru",
    )(z3, z3, conv_buf, h0, lp["conv_w"], lp["conv_b"], lp["wa"], lp["ba"], lp["wx"], lp["bx"],
      lp["lam"], lp["gn_a"])


@functools.lru_cache(maxsize=None)
def _hgrn_consts(c):
    t = np.arange(c)[:, None]
    j = np.arange(c)[None, :]
    blocks = [j <= t, j > t]
    masks = []
    w = c // 2
    while w >= 1:
        p = t % (2 * w)
        ref = t - p + w - 1
        blocks.append(np.where(p >= w, (j > ref) & (j <= t), (j > t) & (j <= ref)))
        masks.append((t // (2 * w) == j // (2 * w)) & (t % (2 * w) >= w) & (j % (2 * w) < w))
        w //= 2
    masks.append(t == j)
    dstack = np.concatenate(blocks, axis=0).astype(np.float32)
    masks = np.stack([np.tile(m, (1, N_HGRN)) for m in masks]).astype(np.float32)
    lane_head = np.arange(D_HGRN)[None, :] // HGRN_DK
    head_rows = np.repeat(np.arange(N_HGRN), c)[:, None]
    head_mask = (head_rows == lane_head).astype(np.float32)
    same_head = (np.arange(D_HGRN)[:, None] // HGRN_DV == lane_head).astype(np.float32)
    return dstack, masks, head_mask, same_head


def _hgrn_kernel(q_ref, f_ref, v_ref, g_ref, lbraw_ref, ng_ref, s0_ref, dst_ref, msk_ref, hm_ref, sh_ref,
                 y_ref, st_ref, st_scr, *, c, chunks, layer, valid):
    ti = pl.program_id(1)
    n_seq = q_ref.shape[0]

    @pl.when(ti == 0)
    def _():
        for si in range(n_seq):
            wide = jnp.concatenate([s0_ref[si]] * N_HGRN, axis=1) * sh_ref[...]
            st_scr[si] = wide.T

    lbraw = lbraw_ref[...]
    e = jnp.exp(lbraw - jnp.max(lbraw, axis=0, keepdims=True))
    sm = e / jnp.sum(e, axis=0, keepdims=True)
    lrow = lax.broadcasted_iota(jnp.int32, sm.shape, 0)
    lb = jnp.sum(jnp.where((lrow >= 1) & (lrow <= layer), sm, 0.0), axis=0, keepdims=True)

    n_levels = msk_ref.shape[0] - 1
    hm = hm_ref[...]
    hm_b = hm.astype(BF16)
    same_head = sh_ref[...]
    shb = same_head.astype(BF16)

    def per_head_rows(x):
        if c % 16 == 0:
            return jnp.concatenate([x.astype(BF16)] * N_HGRN, axis=0) * hm_b
        return (jnp.concatenate([x] * N_HGRN, axis=0) * hm).astype(BF16)

    items = [(si, slice(ci * c, (ci + 1) * c)) for ci in range(chunks) for si in range(n_seq)]

    def prepare(item):
        si, rows = item
        f = lb + (1.0 - lb) * jax.nn.sigmoid(f_ref[si, rows, :])
        log_f = jnp.log(f)
        k = 1.0 - f
        if valid is not None:
            keep = lax.broadcasted_iota(jnp.int32, (c, D_HGRN), 0) < valid
            log_f = jnp.where(keep, log_f, 0.0)
            k = jnp.where(keep, k, 0.0)
        ex = _dot_exact_lhs(dst_ref[...], log_f)
        return q_ref[si, rows, :], k, v_ref[si, rows, :], ex

    prepared = [prepare(item) for item in items]

    scores = [jnp.zeros((c, N_HGRN * c), F32) for _ in items]
    for li in range(n_levels + 1):
        for n, (q, k, v, ex) in enumerate(prepared):
            if li < n_levels:
                decay = jnp.exp(ex[(2 + li) * c:(3 + li) * c])
                qt, kt = q * decay, k * decay
            else:
                qt, kt = q, k
            scores[n] = scores[n] + _dot_nt(qt.astype(BF16), per_head_rows(kt)) * msk_ref[li]
    intra = [_dot(sc.astype(BF16), per_head_rows(v)) for sc, (q, k, v, ex) in zip(scores, prepared)]

    outs = []
    for (si, rows), (q, k, v, ex), o in zip(items, prepared, intra):
        cum = ex[0:c]
        rem = ex[c:2 * c]
        st = st_scr[si]
        outs.append(o + _dot_nt((q * jnp.exp(cum)).astype(BF16), st.astype(BF16)))
        upd = _dot_tn(v.astype(BF16), (k * jnp.exp(rem)).astype(BF16))
        st_scr[si] = st * jnp.exp(cum[c - 1:c, :]) + upd * same_head

    for (si, rows), o in zip(items, outs):
        g = g_ref[si, rows, :]
        o2 = o * o
        o2_hi = o2.astype(BF16)
        o2_lo = (o2 - o2_hi.astype(F32)).astype(BF16)
        ms = (_dot(o2_hi, shb) + _dot(o2_lo, shb)) * (1.0 / HGRN_DV)
        y_ref[si, rows, :] = o * lax.rsqrt(ms + EPS) * ng_ref[...] * (g * jax.nn.sigmoid(g))

    @pl.when(ti == pl.num_programs(1) - 1)
    def _():
        for si in range(n_seq):
            wide = st_scr[si].T * same_head
            st_ref[si] = functools.reduce(
                lambda a, b: a + b, [wide[:, hd * HGRN_DV:(hd + 1) * HGRN_DV] for hd in range(N_HGRN)])


def _hgrn(z3, s0t, lb_raw, norm_g, layer, c, chunks, n_seq, valid=None):
    b, t, _ = z3.shape
    tt = c * chunks
    dstack, masks, head_mask, same_head = _hgrn_consts(c)
    const2 = lambda bi, ti: (0, 0)
    col = lambda cidx: pl.BlockSpec((n_seq, tt, D_HGRN), lambda bi, ti: (bi, ti, cidx))
    state = pl.BlockSpec((n_seq, D_HGRN, HGRN_DV), lambda bi, ti: (bi, 0, 0))
    return pl.pallas_call(
        functools.partial(_hgrn_kernel, c=c, chunks=chunks, layer=layer, valid=valid),
        grid=(b // n_seq, t // tt),
        in_specs=[col(COL_QB), col(COL_FB), col(COL_IB), col(COL_GB),
                  pl.BlockSpec((DEPTH, D_HGRN), const2),
                  pl.BlockSpec((1, D_HGRN), const2),
                  state,
                  pl.BlockSpec(dstack.shape, const2),
                  pl.BlockSpec(masks.shape, lambda bi, ti: (0, 0, 0)),
                  pl.BlockSpec(head_mask.shape, const2),
                  pl.BlockSpec(same_head.shape, const2)],
        out_specs=(pl.BlockSpec((n_seq, tt, D_HGRN), lambda bi, ti: (bi, ti, 0)), state),
        out_shape=(jax.ShapeDtypeStruct((b, t, D_HGRN), F32),
                   jax.ShapeDtypeStruct((b, D_HGRN, HGRN_DV), F32)),
        scratch_shapes=[pltpu.VMEM((n_seq, D_HGRN, D_HGRN), F32)],
        compiler_params=_cparams("parallel", "arbitrary"),
        name="hgrn",
    )(z3, z3, z3, z3, lb_raw, jnp.tile(norm_g, N_HGRN).reshape(1, D_HGRN), s0t,
      jnp.asarray(dstack, BF16), jnp.asarray(masks), jnp.asarray(head_mask), jnp.asarray(same_head))


def _strided_rows(start, dil):
    return pl.ds(start, BAND, stride=dil) if dil > 1 else pl.ds(start, BAND)


def _dil_attn_kernel(q_ref, kp_ref, kc_ref, vp_ref, vc_ref, slope_ref, o_ref, o_scr, d_scr, m_scr):
    first_block = pl.program_id(1) == 0
    n2 = 2 * BAND
    row = lax.broadcasted_iota(jnp.int32, (n2, n2), 0)
    col = lax.broadcasted_iota(jnp.int32, (n2, n2), 1)
    back = (row & (BAND - 1)) - (col & (BAND - 1)) + jnp.where(col < BAND, BAND, 0)
    in_band = (back >= 0) & (back <= BAND)
    steps = back.astype(F32)
    slope = jnp.concatenate([slope_ref[0], slope_ref[0]], axis=1)
    lane_first = lax.broadcasted_iota(jnp.int32, (BAND, PAIR), 1) < ATT_DH
    scale = ATT_DH ** -0.5
    ones = jnp.ones((BAND, PAIR), BF16)

    def scores(unit):
        g, cur, prev, prev_refs, bias = unit
        kp_r, vp_r = prev_refs
        q = q_ref[0, cur, :] * scale
        q2 = jnp.concatenate([jnp.where(lane_first, q, 0.0), jnp.where(lane_first, 0.0, q)],
                             axis=0).astype(BF16)
        keys = jnp.concatenate([kp_r[0, prev, :], kc_ref[0, cur, :]], axis=0).astype(BF16)
        return _dot_nt(q2, keys) - bias

    def probs(s):
        m = jnp.max(s, axis=1, keepdims=True)
        return jnp.exp(s - m).astype(BF16), m

    def values(unit, pm):
        g, cur, prev, prev_refs, _ = unit
        kp_r, vp_r = prev_refs
        p, m = pm
        v1 = jnp.concatenate(
            [jnp.concatenate([vp_r[0, prev, :], vc_ref[0, cur, :]], axis=0).astype(BF16),
             jnp.concatenate([ones, ones], axis=0)], axis=1)
        pvd = _dot(p, v1)
        pv, den = pvd[:, 0:PAIR], pvd[:, PAIR:2 * PAIR]
        m_b = jnp.broadcast_to(m, (n2, PAIR))
        o_scr[g, cur, :] = jnp.where(lane_first, pv[0:BAND], pv[BAND:n2])
        d_scr[g, cur, :] = jnp.where(lane_first, den[0:BAND], den[BAND:n2])
        m_scr[g, cur, :] = jnp.where(lane_first, m_b[0:BAND], m_b[BAND:n2])

    units = []
    for g, dil in enumerate(DILATIONS):
        bias = jnp.where(in_band, slope * steps * float(dil), -NEG_BIG)
        bias_first = jnp.where((col < BAND) & first_block, -NEG_BIG, bias)
        span = BAND * dil
        for j in range(SUPER // span):
            for r in range(dil):
                cur = _strided_rows(j * span + r, dil)
                if j == 0:
                    units.append((g, cur, _strided_rows(SUPER - span + r, dil), (kp_ref, vp_ref), bias_first))
                else:
                    units.append((g, cur, _strided_rows((j - 1) * span + r, dil), (kc_ref, vc_ref), bias))

    groups = [units[i:i + UNITS_IN_FLIGHT] for i in range(0, len(units), UNITS_IN_FLIGHT)]
    ss_next = [scores(u) for u in groups[0]]
    for gi, group in enumerate(groups):
        ss = ss_next
        if gi + 1 < len(groups):
            ss_next = [scores(u) for u in groups[gi + 1]]
        pm = [probs(s) for s in ss]
        for u, v in zip(group, pm):
            values(u, v)

    rows_per_iter = 2 * BAND

    def mix(i, carry):
        rows = pl.ds(pl.multiple_of(i * rows_per_iter, rows_per_iter), rows_per_iter)
        ms = [m_scr[g, rows, :] for g in range(len(DILATIONS))]
        m = functools.reduce(jnp.maximum, ms)
        ws = [jnp.exp(mg - m) for mg in ms]
        num = sum(wgt * o_scr[g, rows, :] for g, wgt in enumerate(ws))
        den = sum(wgt * d_scr[g, rows, :] for g, wgt in enumerate(ws))
        o_ref[0, rows, :] = num / den
        return carry

    lax.fori_loop(0, SUPER // rows_per_iter, mix, 0)


def _dil_attn(z3):
    b, t, _ = z3.shape
    n_pairs = N_ATT // 2
    lanes_per_col = D_ATT // PAIR
    blk = lambda colblk, prev: pl.BlockSpec(
        (1, SUPER, PAIR),
        (lambda bi, sb, p: (bi, jnp.maximum(sb - 1, 0), colblk * lanes_per_col + p)) if prev
        else (lambda bi, sb, p: (bi, sb, colblk * lanes_per_col + p)))
    slopes = np.asarray(ALIBI_SLOPES, np.float32).reshape(n_pairs, 2, 1, 1)
    slope_tbl = np.broadcast_to(slopes, (n_pairs, 2, BAND, PAIR)).reshape(n_pairs, 2 * BAND, PAIR)
    return pl.pallas_call(
        _dil_attn_kernel,
        grid=(b, t // SUPER, n_pairs),
        in_specs=[blk(COL_QC, False), blk(COL_KC, True), blk(COL_KC, False),
                  blk(COL_VC, True), blk(COL_VC, False),
                  pl.BlockSpec((1, 2 * BAND, PAIR), lambda bi, sb, p: (p, 0, 0))],
        out_specs=pl.BlockSpec((1, SUPER, PAIR), lambda bi, sb, p: (bi, sb, p)),
        out_shape=jax.ShapeDtypeStruct((b, t, D_ATT), F32),
        scratch_shapes=[pltpu.VMEM((len(DILATIONS), SUPER, PAIR), F32)] * 3,
        compiler_params=_cparams("parallel", "parallel", "parallel"),
        name="dil_attn",
    )(z3, z3, z3, z3, z3, jnp.asarray(slope_tbl))


def _cache_attn_kernel(q_ref, kn_ref, vn_ref, kc_ref, vc_ref, slope_ref, *rest, n_new):
    o_ref, ko_ref, vo_ref = rest[-3:]
    w = kc_ref.shape[-1]
    kn = kn_ref[0]
    vn = vn_ref[0]
    kt = kc_ref[0, 0]
    vt = vc_ref[0, 0]

    sel_row = lax.broadcasted_iota(jnp.int32, (SAMPLE_ROWS, LANE), 0)
    sel_lane = lax.broadcasted_iota(jnp.int32, (SAMPLE_ROWS, LANE), 1)
    sel = jnp.where((sel_lane == LANE - n_new + sel_row) & (sel_row < n_new), 1.0, 0.0).astype(BF16)
    tail_new = lax.broadcasted_iota(jnp.int32, (HALF_ATT, LANE), 1) >= LANE - n_new

    def shift_in(cache_t, new_rows, out_ref):
        rolled = pltpu.roll(cache_t, w - n_new, axis=1)
        out_ref[0, 0, :, 0:w - LANE] = rolled[:, 0:w - LANE]
        out_ref[0, 0, :, w - LANE:w] = jnp.where(tail_new, _dot_exact_tn(new_rows, sel), rolled[:, w - LANE:w])

    shift_in(kt, kn, ko_ref)
    shift_in(vt, vn, vo_ref)

    heads = HALF_ATT // ATT_DH
    rows = heads * SAMPLE_ROWS
    lane_head = lax.broadcasted_iota(jnp.int32, (rows, HALF_ATT), 1) >> _log2(ATT_DH)
    row_head = lax.broadcasted_iota(jnp.int32, (rows, HALF_ATT), 0) >> _log2(SAMPLE_ROWS)
    own = lane_head == row_head
    q8 = jnp.where(own, jnp.concatenate([q_ref[0]] * heads, axis=0), 0.0).astype(BF16)
    scale = ATT_DH ** -0.5
    slope = slope_ref[0][:, 0:1]

    def weights(dist):
        mult = jnp.zeros(dist.shape, F32)
        for dil in DILATIONS:
            hit = (dist >= 0) & (dist <= BAND * dil) & ((dist & (dil - 1)) == 0)
            mult = mult + jnp.where(hit, 1.0, 0.0)
        return mult

    def mask(s, dist):
        mult = weights(dist)
        return jnp.where(mult > 0.0, s * scale - slope * dist.astype(F32), NEG_BIG), mult

    t_c = lax.broadcasted_iota(jnp.int32, (rows, w), 0) & (SAMPLE_ROWS - 1)
    pos_c = lax.broadcasted_iota(jnp.int32, (rows, w), 1)
    s_c, mult_c = mask(_dot(q8, kt.astype(BF16)), w + t_c - pos_c)
    t_n = lax.broadcasted_iota(jnp.int32, (rows, SAMPLE_ROWS), 0) & (SAMPLE_ROWS - 1)
    pos_n = lax.broadcasted_iota(jnp.int32, (rows, SAMPLE_ROWS), 1)
    s_n, mult_n = mask(_dot_nt(q8, kn.astype(BF16)), t_n - pos_n)

    m = jnp.maximum(jnp.max(s_c, axis=1, keepdims=True), jnp.max(s_n, axis=1, keepdims=True))
    e_c = jnp.exp(s_c - m) * mult_c
    e_n = jnp.exp(s_n - m) * mult_n
    den = jnp.sum(e_c, axis=1, keepdims=True) + jnp.sum(e_n, axis=1, keepdims=True)
    o8 = (_dot_nt(e_c.astype(BF16), vt.astype(BF16)) + _dot(e_n.astype(BF16), vn.astype(BF16))) / den
    o8 = jnp.where(own, o8, 0.0)
    o = o8[0:SAMPLE_ROWS]
    for hd in range(1, heads):
        o = o + o8[hd * SAMPLE_ROWS:(hd + 1) * SAMPLE_ROWS]
    o_ref[0] = o


def _cache_attn(z3, cache_kt, cache_vt, layer, out_prev, n_new):
    depth, b, _, w = cache_kt.shape
    halves = D_ATT // HALF_ATT
    cols = D_ATT // HALF_ATT
    new = lambda colblk: pl.BlockSpec((1, SAMPLE_ROWS, HALF_ATT), lambda bi, hf: (bi, 0, colblk * cols + hf))
    cache = pl.BlockSpec((1, 1, HALF_ATT, w), lambda bi, hf: (layer, bi, hf, 0))
    heads = N_ATT // halves
    slope_tbl = np.broadcast_to(np.repeat(np.asarray(ALIBI_SLOPES, np.float32), SAMPLE_ROWS)
                                .reshape(halves, heads * SAMPLE_ROWS, 1), (halves, heads * SAMPLE_ROWS, LANE))
    in_specs = [new(COL_QC), new(COL_KC), new(COL_VC), cache, cache,
                pl.BlockSpec((1, heads * SAMPLE_ROWS, LANE), lambda bi, hf: (hf, 0, 0))]
    args = [z3, z3, z3, cache_kt, cache_vt, jnp.asarray(slope_tbl)]
    aliases = {}
    if out_prev is not None:
        in_specs += [pl.BlockSpec(memory_space=pl.ANY)] * 2
        aliases = {len(args): 1, len(args) + 1: 2}
        args += list(out_prev)
    stacked = jax.ShapeDtypeStruct((depth, b, D_ATT, w), F32)
    return pl.pallas_call(
        functools.partial(_cache_attn_kernel, n_new=n_new),
        grid=(b, halves),
        in_specs=in_specs,
        out_specs=(pl.BlockSpec((1, SAMPLE_ROWS, HALF_ATT), lambda bi, hf: (bi, 0, hf)), cache, cache),
        out_shape=(jax.ShapeDtypeStruct((b, SAMPLE_ROWS, D_ATT), F32), stacked, stacked),
        input_output_aliases=aliases,
        compiler_params=_cparams("parallel", "parallel"),
        name="cache_attn",
    )(*args)


def _mix_out(x, ya, yb, yc, gn, w_ref):
    acc = _dot(ya.astype(BF16), w_ref[0:D_LRU, :])
    acc = acc + _dot(yb.astype(BF16), w_ref[D_LRU:D_LRU + D_HGRN, :])
    acc = acc + _dot(_rms(yc, gn).astype(BF16), w_ref[D_LRU + D_HGRN:D_MODEL, :])
    return x + acc


def _wout_kernel(x_ref, ya_ref, yb_ref, yc_ref, gn_ref, w_ref, out_ref):
    out_ref[...] = _mix_out(x_ref[...], ya_ref[...], yb_ref[...], yc_ref[...], gn_ref[...], w_ref)


def _wout(x, ya, yb, yc, gn_c, w_out, tm):
    m = x.shape[0]
    spec = lambda n: pl.BlockSpec((tm, n), lambda i: (i, 0))
    return pl.pallas_call(
        _wout_kernel,
        grid=(m // tm,),
        in_specs=[spec(D_MODEL), spec(D_LRU), spec(D_HGRN), spec(D_ATT),
                  pl.BlockSpec((1, D_ATT), lambda i: (0, 0)), _resident((D_MODEL, D_MODEL))],
        out_specs=spec(D_MODEL),
        out_shape=jax.ShapeDtypeStruct((m, D_MODEL), F32),
        compiler_params=_cparams("parallel"),
        name="wout",
    )(x, ya, yb, yc, gn_c.reshape(1, D_ATT), w_out)


def _xattn_kernel(x_ref, ya_ref, yb_ref, yc_ref, gnc_ref, wmix_ref, g_ref, wq_ref, mk_ref, mv_ref, wo_ref, o_ref):
    x = _mix_out(x_ref[0], ya_ref[0], yb_ref[0], yc_ref[0], gnc_ref[...], wmix_ref)
    q = _dot(_rms(x, g_ref[...]).astype(BF16), wq_ref[...])
    mk = mk_ref[0].astype(BF16)
    mv = mv_ref[0].astype(BF16)
    scale = XHEAD_DIM ** -0.5
    heads = [slice(hd * XHEAD_DIM, (hd + 1) * XHEAD_DIM) for hd in range(N_XHEADS)]
    ss = [_dot_nt(q[:, lanes].astype(BF16), mk[:, lanes]) * scale for lanes in heads]
    es = [jnp.exp(s - jnp.max(s, axis=1, keepdims=True)) for s in ss]
    ps = [(e / jnp.sum(e, axis=1, keepdims=True)).astype(BF16) for e in es]
    os_ = [_dot(p, mv[:, lanes]).astype(BF16) for p, lanes in zip(ps, heads)]
    acc = x
    for oh, lanes in zip(os_, heads):
        acc = acc + _dot(oh, wo_ref[lanes, :])
    o_ref[0] = acc


def _xattn(x3, ya, yb, yc, gn_c, w_mix, g, wq, mem_kv, wo, tm):
    b, t, _ = x3.shape
    const = lambda bi, ti: (0, 0)
    mat = _resident((D_MODEL, D_MODEL))
    rows = lambda n: pl.BlockSpec((1, tm, n), lambda bi, ti: (bi, ti, 0))
    return pl.pallas_call(
        _xattn_kernel,
        grid=(b, t // tm),
        in_specs=[rows(D_MODEL), rows(D_LRU), rows(D_HGRN), rows(D_ATT),
                  pl.BlockSpec((1, D_ATT), const), mat,
                  pl.BlockSpec((1, D_MODEL), const), mat,
                  pl.BlockSpec((1, N_MEM, D_MODEL), lambda bi, ti: (bi, 0, 0)),
                  pl.BlockSpec((1, N_MEM, D_MODEL), lambda bi, ti: (bi, 0, 1)),
                  mat],
        out_specs=rows(D_MODEL),
        out_shape=jax.ShapeDtypeStruct((b, t, D_MODEL), F32),
        compiler_params=_cparams("parallel", "parallel"),
        name="xattn",
    )(x3, ya, yb, yc, gn_c.reshape(1, D_ATT), w_mix, g.reshape(1, D_MODEL), wq, mem_kv, mem_kv, wo)


def _mem_attn_kernel(q_ref, mk_ref, mv_ref, o_ref):
    t = q_ref.shape[1]
    rows = N_MEM * N_XHEADS
    k2 = mk_ref.at[0, 0].reshape(rows, XHEAD_DIM)[...].astype(BF16)
    v2 = mv_ref.at[0, 0].reshape(rows, XHEAD_DIM)[...].astype(BF16)
    q = q_ref[0]
    qs = jnp.concatenate([q[:, hd * XHEAD_DIM:(hd + 1) * XHEAD_DIM] for hd in range(N_XHEADS)], axis=0)
    row_head = lax.broadcasted_iota(jnp.int32, (N_XHEADS * t, rows), 0) >> _log2(t)
    col_head = lax.broadcasted_iota(jnp.int32, (N_XHEADS * t, rows), 1) & (N_XHEADS - 1)
    s = _dot_nt(qs.astype(BF16), k2) * (XHEAD_DIM ** -0.5)
    s = jnp.where(row_head == col_head, s, NEG_BIG)
    e = jnp.exp(s - jnp.max(s, axis=1, keepdims=True))
    p = e / jnp.sum(e, axis=1, keepdims=True)
    o = _dot(p.astype(BF16), v2)
    for hd in range(N_XHEADS):
        o_ref[0, :, hd * XHEAD_DIM:(hd + 1) * XHEAD_DIM] = o[hd * t:(hd + 1) * t]


def _mem_attn(q3, mem_k, mem_v, layer):
    b, t, _ = q3.shape
    row = pl.BlockSpec((1, t, D_MODEL), lambda bi: (bi, 0, 0))
    mem = pl.BlockSpec((1, 1, N_MEM, N_XHEADS, XHEAD_DIM), lambda bi: (layer, bi, 0, 0, 0))
    return pl.pallas_call(
        _mem_attn_kernel,
        grid=(b,),
        in_specs=[row, mem, mem],
        out_specs=row,
        out_shape=jax.ShapeDtypeStruct((b, t, D_MODEL), F32),
        compiler_params=_cparams("parallel"),
        name="mem_attn",
    )(q3, mem_k, mem_v)


def _proj_res_kernel(x_ref, a_ref, w_ref, o_ref):
    o_ref[...] = x_ref[...] + _dot(a_ref[...].astype(BF16), w_ref[...])


def _proj_res(x, a, w, tm):
    m, k = a.shape
    n = w.shape[1]
    return pl.pallas_call(
        _proj_res_kernel,
        grid=(m // tm,),
        in_specs=[pl.BlockSpec((tm, n), lambda i: (i, 0)),
                  pl.BlockSpec((tm, k), lambda i: (i, 0)),
                  _resident((k, n))],
        out_specs=pl.BlockSpec((tm, n), lambda i: (i, 0)),
        out_shape=jax.ShapeDtypeStruct((m, n), F32),
        compiler_params=_cparams("parallel"),
        name="proj_res",
    )(x, a, w)


def _block_diag(w):
    n, c, d = w.shape
    eye = jnp.eye(n, dtype=w.dtype)
    return jnp.einsum("ncd,nm->ncmd", w, eye).reshape(n * c, n * d)


def kernel(x_prompt, x_sample, state_lru_h, state_lru_conv, state_hgrn, cache_swa_k, cache_swa_v, cache_mem_k, cache_mem_v, mem_prompt, n_ffn1, ffn1_wg, ffn1_wu, ffn1_wd, n_mix, w_in, lru_conv_w, lru_conv_b, lru_wa, lru_ba, lru_wx, lru_bx, lru_lambda, hgrn_lb, hgrn_norm, gn_a, gn_c, w_out, n_cross, x_wq, x_wk, x_wv, x_wo, n_ffn2, ffn2_wg, ffn2_wu, ffn2_wd, n_final):
    bp, tp, _ = x_prompt.shape
    bs, ts_new, _ = x_sample.shape
    w_buf = cache_swa_k.shape[2]
    assert ts_new <= SAMPLE_ROWS and tp % (BAND * max(DILATIONS)) == 0 and tp >= MAX_WINDOW
    tm_p = 512
    tm_s = bs * SAMPLE_ROWS

    xp = x_prompt.reshape(bp * tp, D_MODEL)
    xs = jnp.pad(x_sample, ((0, 0), (0, SAMPLE_ROWS - ts_new), (0, 0))).reshape(tm_s, D_MODEL)
    mem2 = mem_prompt.reshape(bp * N_MEM, D_MODEL)
    ones = jnp.ones((D_MODEL,), F32)

    outs = {name: [] for name in ("p_h", "p_c", "p_s", "p_k", "p_v", "p_mk", "p_mv", "s_h", "s_c", "s_s")}
    to_feature_major = lambda c: jnp.transpose(c, (0, 1, 3, 4, 2)).reshape(DEPTH, bs, D_ATT, w_buf)
    cache_kt, cache_vt = to_feature_major(cache_swa_k), to_feature_major(cache_swa_v)
    swa_out = None
    yp = ys = None
    for l in range(DEPTH):
        bf = lambda w: w[l].astype(BF16)
        wg1, wu1, wd1 = bf(ffn1_wg), bf(ffn1_wu), bf(ffn1_wd)
        wg2, wu2, wd2 = bf(ffn2_wg), bf(ffn2_wu), bf(ffn2_wd)
        win, wo_mix, wq, wo_x = bf(w_in), bf(w_out), bf(x_wq), bf(x_wo)
        wkv = jnp.concatenate([x_wk[l], x_wv[l]], axis=1).astype(BF16)
        lp = dict(conv_w=lru_conv_w[l], conv_b=lru_conv_b[l].reshape(1, D_LRU),
                  wa=_block_diag(lru_wa[l]).astype(BF16), ba=lru_ba[l].reshape(1, D_LRU),
                  wx=_block_diag(lru_wx[l]).astype(BF16), bx=lru_bx[l].reshape(1, D_LRU),
                  lam=lru_lambda[l].reshape(1, D_LRU), gn_a=gn_a[l].reshape(1, D_LRU))
        last = l == DEPTH - 1

        mkv = _proj(mem2, ones, wkv, tm=bp * N_MEM, norm=False).reshape(bp, N_MEM, 2 * D_MODEL)
        xp = _ffn(xp, n_ffn1[l], wg1, wu1, wd1, tm_p)
        z3 = _proj(xp, n_mix[l], win, tm_p).reshape(bp, tp, D_IN)
        ya, hl = _lru(z3, jnp.zeros((bp, SUBLANE, D_LRU), F32), jnp.zeros((bp, 1, D_LRU), F32), lp, tt=256)
        yb, st = _hgrn(z3, jnp.zeros((bp, D_HGRN, HGRN_DV), F32), hgrn_lb, hgrn_norm[l], l,
                       c=HGRN_CHUNK, chunks=8, n_seq=bp)
        yc = _dil_attn(z3)
        xp = _xattn(xp.reshape(bp, tp, D_MODEL), ya, yb, yc, gn_c[l], wo_mix, n_cross[l], wq, mkv, wo_x, tm_p)
        xp = xp.reshape(bp * tp, D_MODEL)
        if last:
            xp, yp = _ffn(xp, n_ffn2[l], wg2, wu2, wd2, tm_p, g_final=n_final)
        else:
            xp = _ffn(xp, n_ffn2[l], wg2, wu2, wd2, tm_p)
        outs["p_h"].append(hl[:, SUBLANE - 1])
        outs["p_c"].append(z3[:, tp - (CONV_W - 1):, 0:D_LRU])
        outs["p_s"].append(st.reshape(bp, N_HGRN, HGRN_DK, HGRN_DV))
        keep = min(MAX_WINDOW, tp)
        kcol = COL_KC * D_ATT
        vcol = COL_VC * D_ATT
        outs["p_k"].append(z3[:, tp - keep:, kcol:kcol + D_ATT].reshape(bp, keep, N_ATT, ATT_DH))
        outs["p_v"].append(z3[:, tp - keep:, vcol:vcol + D_ATT].reshape(bp, keep, N_ATT, ATT_DH))
        outs["p_mk"].append(mkv[:, :, 0:D_MODEL].reshape(bp, N_MEM, N_XHEADS, XHEAD_DIM))
        outs["p_mv"].append(mkv[:, :, D_MODEL:].reshape(bp, N_MEM, N_XHEADS, XHEAD_DIM))

        xs = _ffn(xs, n_ffn1[l], wg1, wu1, wd1, tm_s)
        zs3 = _proj(xs, n_mix[l], win, tm_s).reshape(bs, SAMPLE_ROWS, D_IN)
        conv_buf = jnp.pad(state_lru_conv[l], ((0, 0), (SUBLANE - (CONV_W - 1), 0), (0, 0)))
        ya, hl = _lru(zs3, conv_buf, state_lru_h[l].reshape(bs, 1, D_LRU), lp, tt=SAMPLE_ROWS, valid=ts_new)
        yb, st = _hgrn(zs3, state_hgrn[l].reshape(bs, D_HGRN, HGRN_DV), hgrn_lb, hgrn_norm[l], l,
                       c=SAMPLE_ROWS, chunks=1, n_seq=4, valid=ts_new)
        yc, ko, vo = _cache_attn(zs3, cache_kt, cache_vt, l, swa_out, ts_new)
        swa_out = (ko, vo)
        xs = _wout(xs, ya.reshape(tm_s, D_LRU), yb.reshape(tm_s, D_HGRN), yc.reshape(tm_s, D_ATT),
                   gn_c[l], wo_mix, tm_s)
        qs = _proj(xs, n_cross[l], wq, tm_s).reshape(bs, SAMPLE_ROWS, D_MODEL)
        xs = _proj_res(xs, _mem_attn(qs, cache_mem_k, cache_mem_v, l).reshape(tm_s, D_MODEL), wo_x, tm_s)
        if last:
            xs, ys = _ffn(xs, n_ffn2[l], wg2, wu2, wd2, tm_s, g_final=n_final)
        else:
            xs = _ffn(xs, n_ffn2[l], wg2, wu2, wd2, tm_s)
        outs["s_h"].append(hl[:, SUBLANE - 1])
        outs["s_c"].append(zs3[:, ts_new - (CONV_W - 1):ts_new, 0:D_LRU])
        outs["s_s"].append(st.reshape(bs, N_HGRN, HGRN_DK, HGRN_DV))

    y_prompt = yp.reshape(bp, tp, D_MODEL)
    y_sample = ys.reshape(bs, SAMPLE_ROWS, D_MODEL)[:, :ts_new]
    stack = lambda name: jnp.stack(outs[name])
    from_feature_major = lambda c: jnp.transpose(c.reshape(DEPTH, bs, N_ATT, ATT_DH, w_buf), (0, 1, 4, 2, 3))
    return (y_prompt, y_sample,
            stack("p_h"), stack("p_c"), stack("p_s"), stack("p_k"), stack("p_v"), stack("p_mk"), stack("p_mv"),
            stack("s_h"), stack("s_c"), stack("s_s"), from_feature_major(swa_out[0]), from_feature_major(swa_out[1]))
```
